```python
import math
import jax, jax.numpy as jnp
from jax import lax
import numpy as np

D_MODEL = 1024
BATCH = 4
SEQ = 4096
DEPTH = 2
DEC_BATCH = 32
DEC_SEQ = 4
PAST_LEN = 8192
PAGE_SIZE = 128

N_ATTN = (DEPTH + 1) // 2
N_SSM = DEPTH // 2
N_HEADS = 16
HEAD_DIM = D_MODEL // N_HEADS
MOBA_BLOCK = 256
MOBA_TOPK = 3
Q_BLOCK = 128
D_INNER = 2 * D_MODEL
SSM_HEAD_DIM = 64
SSM_HEADS = D_INNER // SSM_HEAD_DIM
N_GROUPS = 8
D_STATE = 128
CONV_W = 4
CONV_DIM = D_INNER + 2 * N_GROUPS * D_STATE
IN_DIM = D_INNER + CONV_DIM + SSM_HEADS
SSD_CHUNK = 128
D_FF = ((8 * D_MODEL // 3 + 255) // 256) * 256
EPS = 1e-5

kernel_name = 'moba_mamba2_hybrid_step'


def rms_norm(x, g):
    x32 = x.astype(jnp.float32)
    y = x32 * lax.rsqrt(jnp.mean(x32 * x32, axis=-1, keepdims=True) + EPS)
    return (y * g.astype(jnp.float32)).astype(x.dtype)


def alibi_slopes():
    return jnp.exp2(-8.0 * (jnp.arange(N_HEADS, dtype=jnp.float32) + 1.0) / N_HEADS)


def key_blocks(k, v):
    lk = k.shape[0]
    nb = -(-lk // MOBA_BLOCK)
    pad = nb * MOBA_BLOCK - lk
    kb = jnp.pad(k, ((0, pad), (0, 0), (0, 0))).reshape(nb, MOBA_BLOCK, N_HEADS, HEAD_DIM)
    vb = jnp.pad(v, ((0, pad), (0, 0), (0, 0))).reshape(nb, MOBA_BLOCK, N_HEADS, HEAD_DIM)
    k_mean = jnp.mean(kb.astype(jnp.float32), axis=1)
    return kb.transpose(2, 0, 1, 3), vb.transpose(2, 0, 1, 3), k_mean


def moba_queries(q, q_pos, kbt, vbt, k_mean):
    n = q.shape[0]
    nb = kbt.shape[1]
    own = q_pos // MOBA_BLOCK
    gate = jnp.einsum('nhd,bhd->nhb', q.astype(jnp.float32), k_mean)
    fully_past = jnp.arange(nb)[None, None, :] < own[:, None, None]
    gate = jnp.where(fully_past, gate, -jnp.inf)
    _, top = lax.top_k(gate, min(MOBA_TOPK, nb))
    own_b = jnp.broadcast_to(own[:, None, None], (n, N_HEADS, 1)).astype(top.dtype)
    idx = jnp.concatenate([top, own_b], axis=-1)
    blk_ok = jnp.concatenate([top < own_b, jnp.ones((n, N_HEADS, 1), dtype=bool)], axis=-1)
    heads = jnp.arange(N_HEADS)[None, :, None]
    k_sel = kbt[heads, idx]
    v_sel = vbt[heads, idx]
    dist = q_pos[:, None, None, None] - (idx[..., None] * MOBA_BLOCK + jnp.arange(MOBA_BLOCK))
    ok = blk_ok[..., None] & (dist >= 0)
    s = (jnp.einsum('nhd,nhjrd->nhjr', q, k_sel).astype(jnp.float32) * HEAD_DIM ** -0.5
         - alibi_slopes()[None, :, None, None] * dist.astype(jnp.float32))
    s = jnp.where(ok, s, -jnp.inf)
    p = jax.nn.softmax(s.reshape(n, N_HEADS, -1), axis=-1).reshape(s.shape)
    return jnp.einsum('nhjr,nhjrd->nhd', p.astype(v_sel.dtype), v_sel)


def moba_prompt(q, k, v):
    s = q.shape[1]
    nq = s // Q_BLOCK
    pos = jnp.arange(s, dtype=jnp.int32).reshape(nq, Q_BLOCK)

    def per_seq(args):
        q_s, k_s, v_s = args
        kbt, vbt, k_mean = key_blocks(k_s, v_s)
        out = lax.map(lambda a: moba_queries(a[0], a[1], kbt, vbt, k_mean),
                      (q_s.reshape(nq, Q_BLOCK, N_HEADS, HEAD_DIM), pos))
        return out.reshape(s, N_HEADS, HEAD_DIM)

    return lax.map(per_seq, (q, k, v))


def moba_sample(q, k_all, v_all, past_len):
    q_pos = past_len + jnp.arange(q.shape[1], dtype=jnp.int32)
    return jax.vmap(lambda q_b, k_b, v_b: moba_queries(q_b, q_pos, *key_blocks(k_b, v_b)))(q, k_all, v_all)


def split_qkv(h, w):
    q, k, v = jnp.split(h @ w, 3, axis=-1)
    shp = h.shape[:-1] + (N_HEADS, HEAD_DIM)
    return q.reshape(shp), k.reshape(shp), v.reshape(shp)


def causal_conv(x_ext, w, b):
    length = x_ext.shape[1] - (CONV_W - 1)
    out = x_ext[:, 0:length] * w[0]
    for tap in range(1, CONV_W):
        out = out + x_ext[:, tap:tap + length] * w[tap]
    return out + b


def ssd_scan(x, dt, a, b_in, c_in, h0):
    bsz, length = x.shape[:2]
    e = SSM_HEADS // N_GROUPS
    cl = math.gcd(SSD_CHUNK, length)
    nc = length // cl
    x = x.reshape(bsz, nc, cl, N_GROUPS, e, SSM_HEAD_DIM)
    dt = dt.reshape(bsz, nc, cl, N_GROUPS, e)
    bm = b_in.reshape(bsz, nc, cl, N_GROUPS, D_STATE)
    cm = c_in.reshape(bsz, nc, cl, N_GROUPS, D_STATE)
    a_cs = jnp.cumsum(dt * a.reshape(N_GROUPS, e), axis=2)
    seg = a_cs[:, :, :, None] - a_cs[:, :, None, :]
    causal = jnp.tril(jnp.ones((cl, cl), dtype=bool))[:, :, None, None]
    decay = jnp.exp(jnp.where(causal, seg, -jnp.inf))
    w_ts = jnp.einsum('bctgn,bcsgn->bctsg', cm, bm)[..., None] * decay * dt[:, :, None]
    y_diag = jnp.einsum('bctsge,bcsgep->bctgep', w_ts, x)
    to_end = jnp.exp(a_cs[:, :, -1:] - a_cs) * dt
    states = jnp.einsum('bclgn,bclge,bclgep->bcgepn', bm, to_end, x)
    chunk_decay = jnp.exp(a_cs[:, :, -1])

    def step(h, inp):
        st, dc = inp
        return h * dc[..., None, None] + st, h

    h_last, h_start = lax.scan(step, h0.reshape(bsz, N_GROUPS, e, SSM_HEAD_DIM, D_STATE),
                               (jnp.moveaxis(states, 1, 0), jnp.moveaxis(chunk_decay, 1, 0)))
    h_start = jnp.moveaxis(h_start, 0, 1)
    y_off = jnp.einsum('bctgn,bcgepn,bctge->bctgep', cm, h_start, jnp.exp(a_cs))
    y = (y_diag + y_off).reshape(bsz, length, SSM_HEADS, SSM_HEAD_DIM)
    return y, h_last.reshape(bsz, SSM_HEADS, SSM_HEAD_DIM, D_STATE)


def ssd_mixer(h, conv_buf, ssm_state, w_in, cw, cb, dtb, alog, dsk, gnorm, w_out):
    bsz, length = h.shape[:2]
    z, xbc, dt = jnp.split(h @ w_in, [D_INNER, D_INNER + CONV_DIM], axis=-1)
    xbc_ext = jnp.concatenate([conv_buf.astype(xbc.dtype), xbc], axis=1)
    new_buf = xbc_ext[:, -(CONV_W - 1):]
    xbc = jax.nn.silu(causal_conv(xbc_ext, cw, cb).astype(jnp.float32))
    xs, bm, cm = jnp.split(xbc, [D_INNER, D_INNER + N_GROUPS * D_STATE], axis=-1)
    dt = jax.nn.softplus(dt.astype(jnp.float32) + dtb.astype(jnp.float32))
    a = -jnp.exp(alog.astype(jnp.float32))
    xh = xs.reshape(bsz, length, SSM_HEADS, SSM_HEAD_DIM)
    y, new_state = ssd_scan(xh, dt, a, bm.reshape(bsz, length, N_GROUPS, D_STATE),
                            cm.reshape(bsz, length, N_GROUPS, D_STATE), ssm_state.astype(jnp.float32))
    y = y + dsk.astype(jnp.float32)[:, None] * xh
    yz = y.reshape(bsz, length, D_INNER) * jax.nn.silu(z.astype(jnp.float32))
    yg = yz.reshape(bsz, length, N_GROUPS, D_INNER // N_GROUPS)
    yg = yg * lax.rsqrt(jnp.mean(yg * yg, axis=-1, keepdims=True) + EPS)
    out = (yg.reshape(bsz, length, D_INNER) * gnorm.astype(jnp.float32)).astype(h.dtype) @ w_out
    return out, new_buf, new_state


def swiglu(h, w_gu, w_dn):
    g, u = jnp.split(h @ w_gu, 2, axis=-1)
    return (jax.nn.silu(g) * u) @ w_dn


def setup_inputs(seed: int = 0) -> dict:
    key = jax.random.key(seed)
    ks = jax.random.split(key, 24)
    f32 = jnp.float32
    n_pages = PAST_LEN // PAGE_SIZE
    n_used = DEC_BATCH * n_pages
    n_pool = (5 * n_used + 3) // 4

    def nrm(k, shape, scale):
        return scale * jax.random.normal(k, shape, f32)

    page_table = jax.random.permutation(ks[0], n_pool)[:n_used].reshape(DEC_BATCH, n_pages).astype(jnp.int32)
    dt0 = jnp.exp(jax.random.uniform(ks[1], (N_SSM, SSM_HEADS), f32, math.log(1e-3), math.log(1e-1)))
    dt_bias = dt0 + jnp.log(-jnp.expm1(-dt0))
    a_log = jnp.log(jax.random.uniform(ks[2], (N_SSM, SSM_HEADS), f32, 1.0, 16.0))
    return {
        'x_prompt': nrm(ks[3], (BATCH, SEQ, D_MODEL), 1.0),
        'x_sample': nrm(ks[4], (DEC_BATCH, DEC_SEQ, D_MODEL), 1.0),
        'cache_k': nrm(ks[5], (N_ATTN, n_pool, PAGE_SIZE, N_HEADS, HEAD_DIM), 1.0),
        'cache_v': nrm(ks[6], (N_ATTN, n_pool, PAGE_SIZE, N_HEADS, HEAD_DIM), 1.0),
        'state_conv': nrm(ks[7], (N_SSM, DEC_BATCH, CONV_W - 1, CONV_DIM), 1.0),
        'state_ssm': nrm(ks[8], (N_SSM, DEC_BATCH, SSM_HEADS, SSM_HEAD_DIM, D_STATE), 0.1),
        'page_table': page_table,
        'norm_mix': 1.0 + nrm(ks[9], (DEPTH, D_MODEL), 0.02),
        'norm_ffn': 1.0 + nrm(ks[10], (DEPTH, D_MODEL), 0.02),
        'w_qkv': nrm(ks[11], (N_ATTN, D_MODEL, 3 * D_MODEL), D_MODEL ** -0.5),
        'w_o': nrm(ks[12], (N_ATTN, D_MODEL, D_MODEL), D_MODEL ** -0.5),
        'w_in_ssm': nrm(ks[13], (N_SSM, D_MODEL, IN_DIM), D_MODEL ** -0.5),
        'conv_w': nrm(ks[14], (N_SSM, CONV_W, CONV_DIM), CONV_W ** -0.5),
        'conv_b': nrm(ks[15], (N_SSM, CONV_DIM), 0.02),
        'dt_bias': dt_bias,
        'a_log': a_log,
        'd_skip': 1.0 + nrm(ks[16], (N_SSM, SSM_HEADS), 0.02),
        'norm_ssm': 1.0 + nrm(ks[17], (N_SSM, D_INNER), 0.02),
        'w_out_ssm': nrm(ks[18], (N_SSM, D_INNER, D_MODEL), D_INNER ** -0.5),
        'w_gate_up': nrm(ks[19], (DEPTH, D_MODEL, 2 * D_FF), D_MODEL ** -0.5),
        'w_down': nrm(ks[20], (DEPTH, D_FF, D_MODEL), D_FF ** -0.5),
        'norm_final': 1.0 + nrm(ks[21], (D_MODEL,), 0.02),
    }


def reference(x_prompt, x_sample, cache_k, cache_v, state_conv, state_ssm, page_table,
              norm_mix, norm_ffn, w_qkv, w_o, w_in_ssm, conv_w, conv_b, dt_bias, a_log,
              d_skip, norm_ssm, w_out_ssm, w_gate_up, w_down, norm_final):
    past_len = page_table.shape[1] * PAGE_SIZE
    n_dec = x_sample.shape[0]
    n_pr = x_prompt.shape[0]
    hp, hs = x_prompt, x_sample
    k_pr, v_pr, k_sa, v_sa = [], [], [], []
    conv_pr, ssm_pr, conv_sa, ssm_sa = [], [], [], []
    for layer in range(DEPTH):
        j = layer // 2
        a_p = rms_norm(hp, norm_mix[layer])
        a_s = rms_norm(hs, norm_mix[layer])
        if layer % 2 == 0:
            q, k, v = split_qkv(a_p, w_qkv[j])
            mix_p = moba_prompt(q, k, v).reshape(hp.shape) @ w_o[j]
            k_pr.append(k)
            v_pr.append(v)
            q, k, v = split_qkv(a_s, w_qkv[j])
            k_past = cache_k[j, page_table].reshape(n_dec, past_len, N_HEADS, HEAD_DIM).astype(k.dtype)
            v_past = cache_v[j, page_table].reshape(n_dec, past_len, N_HEADS, HEAD_DIM).astype(v.dtype)
            mix_s = moba_sample(q, jnp.concatenate([k_past, k], axis=1),
                                jnp.concatenate([v_past, v], axis=1), past_len).reshape(hs.shape) @ w_o[j]
            k_sa.append(k)
            v_sa.append(v)
        else:
            zero_buf = jnp.zeros((n_pr, CONV_W - 1, CONV_DIM), hp.dtype)
            zero_state = jnp.zeros((n_pr, SSM_HEADS, SSM_HEAD_DIM, D_STATE), jnp.float32)
            mix_p, buf_p, st_p = ssd_mixer(a_p, zero_buf, zero_state, w_in_ssm[j], conv_w[j], conv_b[j],
                                           dt_bias[j], a_log[j], d_skip[j], norm_ssm[j], w_out_ssm[j])
            mix_s, buf_s, st_s = ssd_mixer(a_s, state_conv[j], state_ssm[j], w_in_ssm[j], conv_w[j], conv_b[j],
                                           dt_bias[j], a_log[j], d_skip[j], norm_ssm[j], w_out_ssm[j])
            conv_pr.append(buf_p)
            ssm_pr.append(st_p)
            conv_sa.append(buf_s)
            ssm_sa.append(st_s)
        hp = hp + mix_p
        hs = hs + mix_s
        hp = hp + swiglu(rms_norm(hp, norm_ffn[layer]), w_gate_up[layer], w_down[layer])
        hs = hs + swiglu(rms_norm(hs, norm_ffn[layer]), w_gate_up[layer], w_down[layer])
    y_prompt = rms_norm(hp, norm_final)
    y_sample = rms_norm(hs, norm_final)
    return (y_prompt, y_sample, jnp.stack(k_pr), jnp.stack(v_pr), jnp.stack(k_sa), jnp.stack(v_sa),
            jnp.stack(conv_pr), jnp.stack(ssm_pr), jnp.stack(conv_sa), jnp.stack(ssm_sa))
```

```python
import functools
import math

import jax
import jax.numpy as jnp
from jax import lax
from jax.experimental import pallas as pl
from jax.experimental.pallas import tpu as pltpu

F32 = jnp.float32
BF16 = jnp.bfloat16

EPS = 1e-5
HEAD_DIM = 64
MOBA_BLOCK = 256
MOBA_TOPK = 3
SSM_HEAD_DIM = 64
N_GROUPS = 8
D_STATE = 128
CONV_W = 4
SSD_CHUNK = 128
SAMPLE_PAD = 16
AUG = 128
NEG_BIAS = -1e30
LANES = 128
SUBLANES = 8
VMEM_LIMIT = 48 * 1024 * 1024

_NT = (((1,), (1,)), ((), ()))


def _silu(x):
    return x / (1.0 + jnp.exp(-x))


def _params(sem, vmem=VMEM_LIMIT):
    return pltpu.CompilerParams(dimension_semantics=sem, vmem_limit_bytes=vmem)


def _rms_to_scratch(x_ref, g_ref, xn_ref):
    x = x_ref[...]
    ms = jnp.mean(x * x, axis=-1, keepdims=True)
    xn_ref[...] = (x * lax.rsqrt(ms + EPS) * g_ref[...]).astype(BF16)


def _norm_mm_body(x_ref, g_ref, w_ref, o_ref, xn_ref):
    @pl.when(pl.program_id(1) == 0)
    def _():
        _rms_to_scratch(x_ref, g_ref, xn_ref)

    o_ref[...] = jnp.dot(xn_ref[...], w_ref[...], preferred_element_type=F32)


def norm_mm(x, g, w, tm, tn):
    m, d = x.shape
    n = w.shape[1]
    return pl.pallas_call(
        _norm_mm_body,
        grid=(m // tm, n // tn),
        in_specs=[pl.BlockSpec((tm, d), lambda i, j: (i, 0)),
                  pl.BlockSpec((1, d), lambda i, j: (0, 0)),
                  pl.BlockSpec((d, tn), lambda i, j: (0, j))],
        out_specs=pl.BlockSpec((tm, tn), lambda i, j: (i, j)),
        out_shape=jax.ShapeDtypeStruct((m, n), F32),
        scratch_shapes=[pltpu.VMEM((tm, d), BF16)],
        compiler_params=_params(("parallel", "arbitrary")),
        name="norm_mm",
    )(x, g.reshape(1, d), w)


def _swiglu_gu_body(x_ref, g_ref, wg_ref, wu_ref, o_ref, xn_ref):
    @pl.when(pl.program_id(1) == 0)
    def _():
        _rms_to_scratch(x_ref, g_ref, xn_ref)

    xn = xn_ref[...]
    gate = jnp.dot(xn, wg_ref[...], preferred_element_type=F32)
    up = jnp.dot(xn, wu_ref[...], preferred_element_type=F32)
    o_ref[...] = (_silu(gate) * up).astype(BF16)


def swiglu_gu(x, g, w_gu, tm, tn):
    m, d = x.shape
    dff = w_gu.shape[1] // 2
    nj = dff // tn
    return pl.pallas_call(
        _swiglu_gu_body,
        grid=(m // tm, nj),
        in_specs=[pl.BlockSpec((tm, d), lambda i, j: (i, 0)),
                  pl.BlockSpec((1, d), lambda i, j: (0, 0)),
                  pl.BlockSpec((d, tn), lambda i, j: (0, j)),
                  pl.BlockSpec((d, tn), lambda i, j: (0, j + nj))],
        out_specs=pl.BlockSpec((tm, tn), lambda i, j: (i, j)),
        out_shape=jax.ShapeDtypeStruct((m, dff), BF16),
        scratch_shapes=[pltpu.VMEM((tm, d), BF16)],
        compiler_params=_params(("parallel", "arbitrary")),
        name="swiglu_gu",
    )(x, g.reshape(1, d), w_gu, w_gu)


def _mm_res_body(a_ref, w_ref, r_ref, o_ref):
    o_ref[...] = r_ref[...] + jnp.dot(a_ref[...], w_ref[...], preferred_element_type=F32)


def _mm_res_norm_body(a_ref, w_ref, r_ref, g_ref, o_ref):
    y = r_ref[...] + jnp.dot(a_ref[...], w_ref[...], preferred_element_type=F32)
    ms = jnp.mean(y * y, axis=-1, keepdims=True)
    o_ref[...] = y * lax.rsqrt(ms + EPS) * g_ref[...]


def mm_res(a, w, res, tm, final_g=None):
    m, k = a.shape
    n = w.shape[1]
    in_specs = [pl.BlockSpec((tm, k), lambda i: (i, 0)),
                pl.BlockSpec((k, n), lambda i: (0, 0)),
                pl.BlockSpec((tm, n), lambda i: (i, 0))]
    args = [a, w, res]
    body = _mm_res_body
    if final_g is not None:
        in_specs.append(pl.BlockSpec((1, n), lambda i: (0, 0)))
        args.append(final_g.reshape(1, n))
        body = _mm_res_norm_body
    return pl.pallas_call(
        body,
        grid=(m // tm,),
        in_specs=in_specs,
        out_specs=pl.BlockSpec((tm, n), lambda i: (i, 0)),
        out_shape=jax.ShapeDtypeStruct((m, n), F32),
        compiler_params=_params(("parallel",)),
        name="mm_res_norm" if final_g is not None else "mm_res",
    )(*args)


def _topk_select(gate, valid, blk, n_blocks):
    g = jnp.where(valid, gate, -jnp.inf)
    rank = jnp.zeros(g.shape, jnp.int32)
    for other in range(n_blocks):
        row = g[other:other + 1, :]
        beats = jnp.where(row > g, 1, jnp.where(row == g, jnp.where(other < blk, 1, 0), 0))
        rank = rank + beats
    return jnp.logical_and(valid, rank < MOBA_TOPK)


def _split3(x):
    hi = x.astype(BF16).astype(F32)
    r1 = x - hi
    mid = r1.astype(BF16).astype(F32)
    lo = (r1 - mid).astype(BF16).astype(F32)
    return hi, mid, lo


def _moba_prep_body(slopes_ref, q_ref, k_ref, v_ref, qa_ref, ka_ref, va_ref, kmean_ref, *, n_blocks):
    hp = pl.program_id(1)
    sb = pl.program_id(2)
    half = AUG // 2

    @pl.when(sb == 0)
    def _():
        kmean_ref[...] = jnp.zeros_like(kmean_ref)

    q = q_ref[...]
    k = k_ref[...]
    v = v_ref[...]
    kmean = kmean_ref[...]
    blk = lax.broadcasted_iota(jnp.int32, (n_blocks, MOBA_BLOCK), 0)
    valid = blk < sb
    lane = lax.broadcasted_iota(jnp.int32, (MOBA_BLOCK, half), 1)
    row = lax.broadcasted_iota(jnp.int32, (MOBA_BLOCK, half), 0)
    arow = lax.broadcasted_iota(jnp.int32, (SUBLANES, MOBA_BLOCK), 0)
    ones_rows = jnp.where(arow < 3, 1.0, 0.0).astype(F32)
    pad_rows = jnp.zeros((AUG - n_blocks - SUBLANES, MOBA_BLOCK), F32)
    v_extra = jnp.where(lane == 0, 1.0, 0.0).astype(F32)
    for hh in range(2):
        qh = q[:, hh * HEAD_DIM:(hh + 1) * HEAD_DIM]
        kh = k[:, hh * HEAD_DIM:(hh + 1) * HEAD_DIM]
        vh = v[:, hh * HEAD_DIM:(hh + 1) * HEAD_DIM]
        kmh = kmean[:, hh * HEAD_DIM:(hh + 1) * HEAD_DIM]
        gate = lax.dot_general(kmh, qh, _NT, precision=lax.Precision.HIGHEST,
                               preferred_element_type=F32)
        sel = _topk_select(gate, valid, blk, n_blocks)
        bias_t = jnp.where(sel, 0.0, jnp.where(blk == sb, 0.0, NEG_BIAS)).astype(F32)
        aug_t = jnp.concatenate([bias_t, ones_rows, pad_rows], axis=0)
        aug = aug_t.T[:, :half]
        qa_ref[0, hh] = jnp.concatenate([qh * (HEAD_DIM ** -0.5), aug], axis=1).astype(BF16)

        slope = slopes_ref[2 * hp + hh]
        pos = (sb * MOBA_BLOCK + row).astype(F32) * slope
        hi, mid, lo = _split3(pos)
        k_extra = jnp.where(lane < n_blocks, jnp.where(lane == sb, 1.0, 0.0),
                            jnp.where(lane == n_blocks, hi,
                                      jnp.where(lane == n_blocks + 1, mid,
                                                jnp.where(lane == n_blocks + 2, lo, 0.0))))
        ka_ref[0, hh] = jnp.concatenate([kh, k_extra], axis=1).astype(BF16)
        va_ref[0, hh] = jnp.concatenate([vh, v_extra], axis=1).astype(BF16)

    kmean_ref[pl.ds(sb, 1), :] = jnp.mean(k, axis=0, keepdims=True)


def moba_prep(qkv, slopes, n_seq, seq, n_heads):
    n_blocks = seq // MOBA_BLOCK
    assert n_blocks + SUBLANES <= AUG // 2 and n_blocks % SUBLANES == 0
    hpairs = n_heads // 2
    aug_shape = jax.ShapeDtypeStruct((n_seq, n_heads, seq, AUG), BF16)
    aug_spec = pl.BlockSpec((1, 2, MOBA_BLOCK, AUG), lambda b, hp, sb: (b, hp, sb, 0))

    def col(which):
        return pl.BlockSpec((MOBA_BLOCK, LANES), lambda b, hp, sb: (b * n_blocks + sb, which * hpairs + hp))

    return pl.pallas_call(
        functools.partial(_moba_prep_body, n_blocks=n_blocks),
        grid=(n_seq, hpairs, n_blocks),
        in_specs=[pl.BlockSpec(memory_space=pltpu.SMEM), col(0), col(1), col(2)],
        out_specs=[aug_spec, aug_spec, aug_spec],
        out_shape=[aug_shape, aug_shape, aug_shape],
        scratch_shapes=[pltpu.VMEM((n_blocks, LANES), F32)],
        compiler_params=_params(("parallel", "parallel", "arbitrary")),
        name="moba_prep",
    )(slopes, qkv, qkv, qkv)


def _moba_attn_body(qa_ref, ka_ref, va_ref, o_ref, *, n_blocks):
    t = MOBA_BLOCK
    rr = lax.broadcasted_iota(jnp.int32, (t, t), 0)
    cc = lax.broadcasted_iota(jnp.int32, (t, t), 1)
    causal = cc <= rr

    def one_head(hh, qi):
        q = qa_ref[0, hh, pl.ds(pl.multiple_of(qi * t, t), t), :]

        def tile(kj, m, acc, diag):
            start = pl.multiple_of(kj * t, t)
            s = lax.dot_general(q, ka_ref[0, hh, pl.ds(start, t), :], _NT, preferred_element_type=F32)
            if diag:
                s = jnp.where(causal, s, -jnp.inf)
            m_new = jnp.maximum(m, jnp.max(s, axis=-1, keepdims=True))
            p = jnp.exp(s - m_new)
            acc = jnp.exp(m - m_new) * acc + jnp.dot(p.astype(BF16), va_ref[0, hh, pl.ds(start, t), :],
                                                     preferred_element_type=F32)
            return m_new, acc

        m0 = jnp.full((t, 1), -jnp.inf, F32)
        acc0 = jnp.zeros((t, AUG), F32)
        m, acc = lax.fori_loop(0, qi, lambda kj, c: tile(kj, c[0], c[1], False), (m0, acc0))
        _, acc = tile(qi, m, acc, True)
        return acc[:, :HEAD_DIM] / acc[:, HEAD_DIM:HEAD_DIM + 1]

    def q_tile(qi, carry):
        out = jnp.concatenate([one_head(0, qi), one_head(1, qi)], axis=1)
        o_ref[pl.ds(pl.multiple_of(qi * t, t), t), :] = out.astype(BF16)
        return carry

    lax.fori_loop(0, n_blocks, q_tile, 0)


def moba_attn(qa, ka, va):
    n_seq, n_heads, seq, _ = qa.shape
    n_blocks = seq // MOBA_BLOCK
    spec = pl.BlockSpec((1, 2, seq, AUG), lambda b, hp: (b, hp, 0, 0))
    return pl.pallas_call(
        functools.partial(_moba_attn_body, n_blocks=n_blocks),
        grid=(n_seq, n_heads // 2),
        in_specs=[spec, spec, spec],
        out_specs=pl.BlockSpec((seq, LANES), lambda b, hp: (b, hp)),
        out_shape=jax.ShapeDtypeStruct((n_seq * seq, n_heads * HEAD_DIM), BF16),
        compiler_params=_params(("parallel", "parallel")),
        name="moba_attn",
    )(qa, ka, va)


def _sample_attn_body(pt_ref, lane_ref, q_ref, kn_ref, vn_ref, *rest, n_pages, pps, n_new, n_heads):
    k_refs = rest[:pps]
    v_refs = rest[pps:2 * pps]
    o_ref = rest[2 * pps]
    qbd_ref, qbd32_ref, s_ref, snew_ref, kmean_ref, acc_ref, lacc_ref = rest[2 * pps + 1:]
    t = pl.program_id(1)
    tk = n_pages // pps
    page = s_ref.shape[0] // n_pages
    d = q_ref.shape[1]
    past = n_pages * page
    n_blocks = past // MOBA_BLOCK
    ppb = MOBA_BLOCK // page
    rows = n_heads

    @pl.when(t == 0)
    def _():
        q = q_ref[...]
        head_of_col = lax.broadcasted_iota(jnp.int32, (rows, d), 1) // HEAD_DIM
        head_of_row = lax.broadcasted_iota(jnp.int32, (rows, d), 0)
        parts = [jnp.where(head_of_col == head_of_row, jnp.broadcast_to(q[n:n + 1, :], (rows, d)), 0.0)
                 for n in range(n_new)]
        parts.append(jnp.zeros((LANES - n_new * rows, d), F32))
        qbd = jnp.concatenate(parts, axis=0)
        qbd32_ref[...] = qbd
        qbd_ref[...] = (qbd * (HEAD_DIM ** -0.5)).astype(BF16)
        acc_ref[...] = jnp.zeros_like(acc_ref)
        lacc_ref[...] = jnp.zeros_like(lacc_ref)

    @pl.when(t < tk)
    def _():
        qbd = qbd_ref[...]
        for blk_i in range(pps // ppb):
            ksum = jnp.zeros((1, d), F32)
            for pi in range(ppb):
                i = blk_i * ppb + pi
                kp = k_refs[i][0]
                ksum = ksum + jnp.sum(kp, axis=0, keepdims=True)
                start = pl.multiple_of((t * pps + i) * page, page)
                s_ref[pl.ds(start, page), :] = lax.dot_general(kp.astype(BF16), qbd, _NT,
                                                               preferred_element_type=F32)
            kmean_ref[pl.ds(t * (pps // ppb) + blk_i, 1), :] = ksum * (1.0 / MOBA_BLOCK)

    @pl.when(t == tk - 1)
    def _():
        lane_q = lane_ref[0:1, :]
        slope = lane_ref[1:2, :]
        gate = lax.dot_general(kmean_ref[...], qbd32_ref[...], _NT, precision=lax.Precision.HIGHEST,
                               preferred_element_type=F32)
        blk = lax.broadcasted_iota(jnp.int32, (n_blocks, LANES), 0)
        sel = _topk_select(gate, blk >= 0, blk, n_blocks)
        selbias = jnp.where(sel, 0.0, -jnp.inf).astype(F32)
        knew = kn_ref[...].astype(BF16)
        sn = lax.dot_general(knew, qbd_ref[...], _NT, preferred_element_type=F32)
        irow = lax.broadcasted_iota(jnp.int32, sn.shape, 0).astype(F32)
        dist_new = lane_q - irow
        sn = jnp.where((dist_new >= 0) & (irow < n_new), sn - slope * dist_new, -jnp.inf)
        krow = lax.broadcasted_iota(jnp.int32, (MOBA_BLOCK, LANES), 0).astype(F32)

        lacc_ref[0:n_blocks, :] = selbias

        def logits(j):
            start = pl.multiple_of(j * MOBA_BLOCK, MOBA_BLOCK)
            dist = (past + lane_q) - (krow + (j * MOBA_BLOCK).astype(F32))
            return start, s_ref[pl.ds(start, MOBA_BLOCK), :] - slope * dist + lacc_ref[pl.ds(j, 1), :]

        def max_step(j, m):
            return jnp.maximum(m, jnp.max(logits(j)[1], axis=0, keepdims=True))

        m = lax.fori_loop(0, n_blocks, max_step, jnp.max(sn, axis=0, keepdims=True))

        def exp_step(j, c):
            start, s = logits(j)
            s_ref[pl.ds(start, MOBA_BLOCK), :] = jnp.exp(s - m)
            return c

        lax.fori_loop(0, n_blocks, exp_step, 0)
        snew_ref[...] = jnp.concatenate([jnp.exp(sn - m), jnp.zeros((LANES - sn.shape[0], LANES), F32)], axis=0)
        lacc_ref[...] = jnp.zeros_like(lacc_ref)

    @pl.when(t >= tk)
    def _():
        ones = jnp.ones((page, LANES), BF16)
        for i in range(pps):
            start = pl.multiple_of(((t - tk) * pps + i) * page, page)
            p = s_ref[pl.ds(start, page), :].T.astype(BF16)
            acc_ref[...] += jnp.dot(p, v_refs[i][0].astype(BF16), preferred_element_type=F32)
            lacc_ref[...] += jnp.dot(p, ones, preferred_element_type=F32)

    @pl.when(t == 2 * tk - 1)
    def _():
        pn = snew_ref[...].T.astype(BF16)[:, :SAMPLE_PAD]
        acc = acc_ref[...] + jnp.dot(pn, vn_ref[...].astype(BF16), preferred_element_type=F32)
        den = lacc_ref[...] + jnp.dot(pn, jnp.ones((SAMPLE_PAD, LANES), BF16), preferred_element_type=F32)
        out = acc / den[:, 0:1]
        head_of_col = lax.broadcasted_iota(jnp.int32, (rows, d), 1) // HEAD_DIM
        head_of_row = lax.broadcasted_iota(jnp.int32, (rows, d), 0)
        outs = [jnp.sum(jnp.where(head_of_col == head_of_row, out[n * rows:(n + 1) * rows, :], 0.0),
                        axis=0, keepdims=True) for n in range(n_new)]
        outs.append(jnp.zeros((SAMPLE_PAD - n_new, d), F32))
        o_ref[...] = jnp.concatenate(outs, axis=0).astype(BF16)


def sample_attn(page_table, lane_info, qkv_s, cache_k, cache_v, n_new, n_heads, pps=4):
    n_dec, n_pages = page_table.shape
    _, page, d = cache_k.shape
    assert n_pages % pps == 0 and pps % (MOBA_BLOCK // page) == 0
    assert n_new * n_heads <= LANES and n_heads % SUBLANES == 0
    tk = n_pages // pps
    n_blocks = n_pages * page // MOBA_BLOCK
    assert n_blocks <= LANES

    def k_spec(i):
        return pl.BlockSpec((1, page, d), lambda b, t, pt: (pt[b, jnp.minimum(t, tk - 1) * pps + i], 0, 0))

    def v_spec(i):
        return pl.BlockSpec((1, page, d), lambda b, t, pt: (pt[b, jnp.maximum(t - tk, 0) * pps + i], 0, 0))

    def row_spec(which):
        return pl.BlockSpec((SAMPLE_PAD, d), lambda b, t, pt: (b, which))

    grid_spec = pltpu.PrefetchScalarGridSpec(
        num_scalar_prefetch=1,
        grid=(n_dec, 2 * tk),
        in_specs=[pl.BlockSpec((SUBLANES, LANES), lambda b, t, pt: (0, 0)),
                  row_spec(0), row_spec(1), row_spec(2)]
                 + [k_spec(i) for i in range(pps)] + [v_spec(i) for i in range(pps)],
        out_specs=pl.BlockSpec((SAMPLE_PAD, d), lambda b, t, pt: (b, 0)),
        scratch_shapes=[pltpu.VMEM((LANES, d), BF16),
                        pltpu.VMEM((LANES, d), F32),
                        pltpu.VMEM((n_pages * page, LANES), F32),
                        pltpu.VMEM((LANES, LANES), F32),
                        pltpu.VMEM((n_blocks, d), F32),
                        pltpu.VMEM((LANES, d), F32),
                        pltpu.VMEM((LANES, LANES), F32)],
    )
    return pl.pallas_call(
        functools.partial(_sample_attn_body, n_pages=n_pages, pps=pps, n_new=n_new, n_heads=n_heads),
        grid_spec=grid_spec,
        out_shape=jax.ShapeDtypeStruct((n_dec * SAMPLE_PAD, d), BF16),
        compiler_params=_params(("parallel", "arbitrary")),
        name="sample_attn",
    )(page_table, lane_info, qkv_s, qkv_s, qkv_s, *([cache_k] * pps), *([cache_v] * pps))


def _ssd_body(z_ref, xs_ref, bc_ref, dt_ref, cinit_ref, sinit_ref, cw_ref, cb_ref, dtb_ref, alog_ref,
              dsk_ref, gn_ref, y_ref, cout_ref, sout_ref, ext_ref, xc_ref, h_ref, *, rows, valid, n_chunks):
    cl = SSD_CHUNK
    c = pl.program_id(1)
    d_inner = xs_ref.shape[1]
    n_heads = d_inner // SSM_HEAD_DIM
    hpg = n_heads // N_GROUPS
    gw = hpg * SSM_HEAD_DIM
    tail = SUBLANES

    @pl.when(c == 0)
    def _():
        ext_ref[0:tail, :] = cinit_ref[0]
        h_ref[...] = sinit_ref[0].reshape(h_ref.shape)
        if rows < cl:
            ext_ref[tail + rows:, :] = jnp.zeros((cl - rows, ext_ref.shape[1]), F32)

    @pl.when(c > 0)
    def _():
        ext_ref[0:tail, :] = ext_ref[cl:cl + tail, :]

    ext_ref[tail:tail + rows, 0:d_inner] = xs_ref[...]
    ext_ref[tail:tail + rows, d_inner:] = bc_ref[...]

    cw = cw_ref[...]
    conv = cb_ref[...] + ext_ref[tail:tail + cl, :] * cw[CONV_W - 1:CONV_W, :]
    for back in range(1, CONV_W):
        conv = conv + ext_ref[tail - back:tail - back + cl, :] * cw[CONV_W - 1 - back:CONV_W - back, :]
    xc_ref[...] = _silu(conv)

    def pad(v):
        if rows == cl:
            return v
        return jnp.concatenate([v, jnp.zeros((cl - rows, v.shape[1]), v.dtype)], axis=0)

    trow = lax.broadcasted_iota(jnp.int32, (cl, LANES), 0)
    dt_raw = pad(dt_ref[...]) + dtb_ref[...]
    dt = jnp.maximum(dt_raw, 0.0) + jnp.log1p(jnp.exp(-jnp.abs(dt_raw)))
    dt = jnp.where(trow < valid, dt, 0.0)
    a = -jnp.exp(alog_ref[...])
    tri_r = lax.broadcasted_iota(jnp.int32, (cl, cl), 0)
    tri_c = lax.broadcasted_iota(jnp.int32, (cl, cl), 1)
    causal = tri_c <= tri_r
    a_cs = jnp.dot(jnp.where(causal, 1.0, 0.0).astype(F32), dt * a, precision=lax.Precision.HIGHEST,
                   preferred_element_type=F32)
    a_cs_t = a_cs.T
    dt_t = dt.T
    z = pad(z_ref[...])

    for g in range(N_GROUPS):
        bg = xc_ref[:, d_inner + g * D_STATE:d_inner + (g + 1) * D_STATE].astype(BF16)
        cg = xc_ref[:, d_inner + (N_GROUPS + g) * D_STATE:d_inner + (N_GROUPS + g + 1) * D_STATE].astype(BF16)
        cb = lax.dot_general(cg, bg, _NT, preferred_element_type=F32)
        xg = xc_ref[:, g * gw:(g + 1) * gw]
        xg_t = xg.T
        h_old = h_ref[g * gw:(g + 1) * gw, :]
        y_off = lax.dot_general(cg, h_old.astype(BF16), _NT, preferred_element_type=F32)
        ys, xw_rows, h_scaled = [], [], []
        for e in range(hpg):
            h = g * hpg + e
            acs_col = a_cs[:, h:h + 1]
            acs_row = a_cs_t[h:h + 1, :]
            dt_row = dt_t[h:h + 1, :]
            a_end = a_cs_t[h:h + 1, cl - 1:cl]
            decay = jnp.exp(jnp.where(causal, acs_col - acs_row, -jnp.inf))
            w = (cb * decay * dt_row).astype(BF16)
            xh = xg[:, e * SSM_HEAD_DIM:(e + 1) * SSM_HEAD_DIM]
            y_diag = jnp.dot(w, xh.astype(BF16), preferred_element_type=F32)
            ys.append(y_diag + y_off[:, e * SSM_HEAD_DIM:(e + 1) * SSM_HEAD_DIM] * jnp.exp(acs_col))
            to_end = jnp.exp(a_end - acs_row) * dt_row
            xw_rows.append(xg_t[e * SSM_HEAD_DIM:(e + 1) * SSM_HEAD_DIM, :] * to_end)
            h_scaled.append(h_old[e * SSM_HEAD_DIM:(e + 1) * SSM_HEAD_DIM, :] * jnp.exp(a_end))
        states = jnp.dot(jnp.concatenate(xw_rows, axis=0).astype(BF16), bg, preferred_element_type=F32)
        h_ref[g * gw:(g + 1) * gw, :] = jnp.concatenate(h_scaled, axis=0) + states
        y = jnp.concatenate(ys, axis=1) + dsk_ref[:, g * gw:(g + 1) * gw] * xg
        yz = y * _silu(z[:, g * gw:(g + 1) * gw])
        ms = jnp.mean(yz * yz, axis=-1, keepdims=True)
        yn = yz * lax.rsqrt(ms + EPS) * gn_ref[:, g * gw:(g + 1) * gw]
        y_ref[:, g * gw:(g + 1) * gw] = yn[:rows].astype(BF16)

    @pl.when(c == n_chunks - 1)
    def _():
        sout_ref[0] = h_ref[...].reshape(sout_ref.shape[1:])
        first = tail + ((valid - (CONV_W - 1)) // SUBLANES) * SUBLANES
        cout_ref[0] = ext_ref[first:first + SUBLANES, :]


def ssd_mixer(proj, conv_init, ssm_init, cw, cb, dtb, alog, dsk, gnorm, n_seq, rows, valid, d_inner):
    total_rows = proj.shape[0] // n_seq
    n_chunks = total_rows // rows
    assert rows == SSD_CHUNK or n_chunks == 1
    assert CONV_W - 1 <= valid <= rows
    n_heads = d_inner // SSM_HEAD_DIM
    conv_dim = d_inner + 2 * N_GROUPS * D_STATE
    assert conv_dim == 2 * d_inner

    def col(width, idx):
        return pl.BlockSpec((rows, width), lambda b, c: (b * n_chunks + c, idx))

    def const(shape):
        return pl.BlockSpec(shape, lambda b, c: (0,) * len(shape))

    outs = pl.pallas_call(
        functools.partial(_ssd_body, rows=rows, valid=valid, n_chunks=n_chunks),
        grid=(n_seq, n_chunks),
        in_specs=[col(d_inner, 0), col(d_inner, 1), col(d_inner, 2), col(LANES, 3 * d_inner // LANES),
                  pl.BlockSpec((1, SUBLANES, conv_dim), lambda b, c: (b, 0, 0)),
                  pl.BlockSpec((1, n_heads, SSM_HEAD_DIM, D_STATE), lambda b, c: (b, 0, 0, 0)),
                  const((CONV_W, conv_dim)), const((1, conv_dim)), const((1, LANES)), const((1, LANES)),
                  const((1, d_inner)), const((1, d_inner))],
        out_specs=[pl.BlockSpec((rows, d_inner), lambda b, c: (b * n_chunks + c, 0)),
                   pl.BlockSpec((1, SUBLANES, conv_dim), lambda b, c: (b, 0, 0)),
                   pl.BlockSpec((1, n_heads, SSM_HEAD_DIM, D_STATE), lambda b, c: (b, 0, 0, 0))],
        out_shape=[jax.ShapeDtypeStruct((proj.shape[0], d_inner), BF16),
                   jax.ShapeDtypeStruct((n_seq, SUBLANES, conv_dim), F32),
                   jax.ShapeDtypeStruct((n_seq, n_heads, SSM_HEAD_DIM, D_STATE), F32)],
        scratch_shapes=[pltpu.VMEM((SUBLANES + SSD_CHUNK, conv_dim), F32),
                        pltpu.VMEM((SSD_CHUNK, conv_dim), F32),
                        pltpu.VMEM((n_heads * SSM_HEAD_DIM, D_STATE), F32)],
        compiler_params=_params(("parallel", "arbitrary")),
        name="ssd_mixer",
    )(proj, proj, proj, proj, conv_init, ssm_init, cw, cb, dtb, alog, dsk, gnorm)
    return outs


def _pad_cols(w, n):
    return jnp.pad(w, ((0, 0), (0, n - w.shape[1])))


def _pad_lanes(v):
    return jnp.pad(v, (0, LANES - v.shape[0])).reshape(1, LANES)


def kernel(x_prompt, x_sample, cache_k, cache_v, state_conv, state_ssm, page_table, norm_mix, norm_ffn, w_qkv, w_o, w_in_ssm, conv_w, conv_b, dt_bias, a_log, d_skip, norm_ssm, w_out_ssm, w_gate_up, w_down, norm_final):
    n_seq, seq, d_model = x_prompt.shape
    n_dec, n_new, _ = x_sample.shape
    n_heads = d_model // HEAD_DIM
    n_pool, page = cache_k.shape[1], cache_k.shape[2]
    d_inner = norm_ssm.shape[1]
    ssm_heads = d_inner // SSM_HEAD_DIM
    conv_dim = conv_w.shape[2]
    d_ff = w_down.shape[1]
    tm_p = 512
    tm_s = n_dec * SAMPLE_PAD

    hp = x_prompt.reshape(n_seq * seq, d_model)
    hs = jnp.pad(x_sample, ((0, 0), (0, SAMPLE_PAD - n_new), (0, 0))).reshape(tm_s, d_model)

    wqkv = w_qkv[0].astype(BF16)
    qkv_p = norm_mm(hp, norm_mix[0], wqkv, tm_p, d_model)
    qkv_s = norm_mm(hs, norm_mix[0], wqkv, tm_s, d_model)
    slopes = jnp.exp2(-8.0 * (jnp.arange(n_heads, dtype=F32) + 1.0) / n_heads)
    qa, ka, va = moba_prep(qkv_p, slopes, n_seq, seq, n_heads)
    attn_p = moba_attn(qa, ka, va)
    lane = jnp.arange(LANES)
    used = lane < n_new * n_heads
    lane_info = jnp.zeros((SUBLANES, LANES), F32)
    lane_info = lane_info.at[0].set(jnp.where(used, lane // n_heads, 0).astype(F32))
    lane_info = lane_info.at[1].set(jnp.where(used, slopes[lane % n_heads], 0.0))
    attn_s = sample_attn(page_table, lane_info, qkv_s, cache_k[0].reshape(n_pool, page, d_model),
                         cache_v[0].reshape(n_pool, page, d_model), n_new, n_heads)
    wo = w_o[0].astype(BF16)
    hp = mm_res(attn_p, wo, hp, tm_p)
    hs = mm_res(attn_s, wo, hs, tm_s)

    def kv_out(qkv, which, nb, rows, keep):
        t = qkv[:, which * d_model:(which + 1) * d_model].reshape(nb, rows, n_heads, HEAD_DIM)
        return t[None, :, :keep]

    k_prompt, v_prompt = kv_out(qkv_p, 1, n_seq, seq, seq), kv_out(qkv_p, 2, n_seq, seq, seq)
    k_sample, v_sample = kv_out(qkv_s, 1, n_dec, SAMPLE_PAD, n_new), kv_out(qkv_s, 2, n_dec, SAMPLE_PAD, n_new)

    def ffn(h, layer, tm, final_g=None):
        act = swiglu_gu(h, norm_ffn[layer], w_gate_up[layer].astype(BF16), tm, 256)
        return mm_res(act, w_down[layer].astype(BF16), h, tm, final_g)

    hp = ffn(hp, 0, tm_p)
    hs = ffn(hs, 0, tm_s)

    in_cols = 2 * d_inner + conv_dim
    proj_w = ((in_cols + ssm_heads + 1279) // 1280) * 1280
    w_in = _pad_cols(w_in_ssm[0], proj_w).astype(BF16)
    proj_p = norm_mm(hp, norm_mix[1], w_in, tm_p, 1280)
    proj_s = norm_mm(hs, norm_mix[1], w_in, tm_s, 1280)
    cb = conv_b[0].reshape(1, conv_dim)
    dtb = _pad_lanes(dt_bias[0])
    alog = _pad_lanes(a_log[0])
    dsk = jnp.repeat(d_skip[0], SSM_HEAD_DIM).reshape(1, d_inner)
    gn = norm_ssm[0].reshape(1, d_inner)
    zero_conv = jnp.zeros((n_seq, SUBLANES, conv_dim), F32)
    zero_ssm = jnp.zeros((n_seq, ssm_heads, SSM_HEAD_DIM, D_STATE), F32)
    y_p, conv_p, ssm_p = ssd_mixer(proj_p, zero_conv, zero_ssm, conv_w[0], cb, dtb, alog, dsk, gn,
                                   n_seq, SSD_CHUNK, SSD_CHUNK, d_inner)
    conv_init_s = jnp.pad(state_conv[0], ((0, 0), (SUBLANES - (CONV_W - 1), 0), (0, 0)))
    y_s, conv_s, ssm_s = ssd_mixer(proj_s, conv_init_s, state_ssm[0], conv_w[0], cb, dtb, alog, dsk, gn,
                                   n_dec, SAMPLE_PAD, n_new, d_inner)
    w_out = w_out_ssm[0].astype(BF16)
    hp = mm_res(y_p, w_out, hp, tm_p)
    hs = mm_res(y_s, w_out, hs, tm_s)
    y_prompt = ffn(hp, 1, tm_p, norm_final).reshape(n_seq, seq, d_model)
    y_sample = ffn(hs, 1, tm_s, norm_final).reshape(n_dec, SAMPLE_PAD, d_model)[:, :n_new]

    lo_p = SUBLANES - (CONV_W - 1)
    lo_s = (n_new - (CONV_W - 1)) % SUBLANES
    return (y_prompt, y_sample, k_prompt, v_prompt, k_sample, v_sample,
            conv_p[None, :, lo_p:lo_p + CONV_W - 1], ssm_p[None],
            conv_s[None, :, lo_s:lo_s + CONV_W - 1], ssm_s[None])
```

```python
import functools
import math

import jax
import jax.numpy as jnp
from jax import lax
from jax.experimental import pallas as pl
from jax.experimental.pallas import tpu as pltpu

F32 = jnp.float32
BF16 = jnp.bfloat16

EPS = 1e-5
HEAD_DIM = 64
MOBA_BLOCK = 256
MOBA_TOPK = 3
SSM_HEAD_DIM = 64
N_GROUPS = 8
D_STATE = 128
CONV_W = 4
SSD_CHUNK = 128
SAMPLE_PAD = 16
AUG = 128
NEG_BIAS = -1e30
LOG2E = math.log2(math.e)
ATTN_GROUP = 4
LANES = 128
SUBLANES = 8
VMEM_LIMIT = 48 * 1024 * 1024
VMEM_LIMIT_BIG = 56 * 1024 * 1024

_NT = (((1,), (1,)), ((), ()))


def _silu(x):
    return x / (1.0 + jnp.exp(-x))


def _params(sem, vmem=VMEM_LIMIT):
    return pltpu.CompilerParams(dimension_semantics=sem, vmem_limit_bytes=vmem)


def _rms_bf16(x_ref, g_ref):
    x = x_ref[...]
    ms = jnp.mean(x * x, axis=-1, keepdims=True)
    return (x * lax.rsqrt(ms + EPS) * g_ref[...]).astype(BF16)


def _norm_mm_body(x_ref, g_ref, w_ref, o_ref, *, tn):
    xn = _rms_bf16(x_ref, g_ref)
    for c in range(w_ref.shape[1] // tn):
        o_ref[:, c * tn:(c + 1) * tn] = jnp.dot(xn, w_ref[:, c * tn:(c + 1) * tn], preferred_element_type=F32)


def norm_mm(x, g, w, tm, tn):
    m, d = x.shape
    n = w.shape[1]
    assert n % tn == 0
    return pl.pallas_call(
        functools.partial(_norm_mm_body, tn=tn),
        grid=(m // tm,),
        in_specs=[pl.BlockSpec((tm, d), lambda i: (i, 0)),
                  pl.BlockSpec((1, d), lambda i: (0, 0)),
                  pl.BlockSpec((d, n), lambda i: (0, 0), pipeline_mode=pl.Buffered(1))],
        out_specs=pl.BlockSpec((tm, n), lambda i: (i, 0)),
        out_shape=jax.ShapeDtypeStruct((m, n), F32),
        compiler_params=_params(("parallel",), VMEM_LIMIT_BIG),
        name="norm_mm",
    )(x, g.reshape(1, d), w)


def _swiglu_gu_body(x_ref, g_ref, w_ref, o_ref, *, tn):
    xn = _rms_bf16(x_ref, g_ref)
    dff = o_ref.shape[1]
    for c in range(dff // tn):
        gate = jnp.dot(xn, w_ref[:, c * tn:(c + 1) * tn], preferred_element_type=F32)
        up = jnp.dot(xn, w_ref[:, dff + c * tn:dff + (c + 1) * tn], preferred_element_type=F32)
        o_ref[:, c * tn:(c + 1) * tn] = (_silu(gate) * up).astype(BF16)


def swiglu_gu(x, g, w_gu, tm, tn):
    m, d = x.shape
    dff = w_gu.shape[1] // 2
    assert dff % tn == 0
    return pl.pallas_call(
        functools.partial(_swiglu_gu_body, tn=tn),
        grid=(m // tm,),
        in_specs=[pl.BlockSpec((tm, d), lambda i: (i, 0)),
                  pl.BlockSpec((1, d), lambda i: (0, 0)),
                  pl.BlockSpec((d, 2 * dff), lambda i: (0, 0), pipeline_mode=pl.Buffered(1))],
        out_specs=pl.BlockSpec((tm, dff), lambda i: (i, 0)),
        out_shape=jax.ShapeDtypeStruct((m, dff), BF16),
        compiler_params=_params(("parallel",)),
        name="swiglu_gu",
    )(x, g.reshape(1, d), w_gu)


def _mm_res_body(a_ref, w_ref, r_ref, o_ref):
    o_ref[...] = r_ref[...] + jnp.dot(a_ref[...], w_ref[...], preferred_element_type=F32)


def _mm_res_norm_body(a_ref, w_ref, r_ref, g_ref, o_ref):
    y = r_ref[...] + jnp.dot(a_ref[...], w_ref[...], preferred_element_type=F32)
    ms = jnp.mean(y * y, axis=-1, keepdims=True)
    o_ref[...] = y * lax.rsqrt(ms + EPS) * g_ref[...]


def mm_res(a, w, res, tm, final_g=None):
    m, k = a.shape
    n = w.shape[1]
    in_specs = [pl.BlockSpec((tm, k), lambda i: (i, 0)),
                pl.BlockSpec((k, n), lambda i: (0, 0)),
                pl.BlockSpec((tm, n), lambda i: (i, 0))]
    args = [a, w, res]
    body = _mm_res_body
    if final_g is not None:
        in_specs.append(pl.BlockSpec((1, n), lambda i: (0, 0)))
        args.append(final_g.reshape(1, n))
        body = _mm_res_norm_body
    return pl.pallas_call(
        body,
        grid=(m // tm,),
        in_specs=in_specs,
        out_specs=pl.BlockSpec((tm, n), lambda i: (i, 0)),
        out_shape=jax.ShapeDtypeStruct((m, n), F32),
        compiler_params=_params(("parallel",)),
        name="mm_res_norm" if final_g is not None else "mm_res",
    )(*args)


def _qkv_body(x_ref, g_ref, w_ref, wq_lo_ref, q_ref, k_ref, v_ref, xm_ref):
    d = x_ref.shape[1]
    x = x_ref[...]
    ms = jnp.mean(x * x, axis=-1, keepdims=True)
    xn = x * lax.rsqrt(ms + EPS) * g_ref[...]
    hi = xn.astype(BF16)
    lo = (xn - hi.astype(F32)).astype(BF16)
    wq = w_ref[:, 0:d]
    q_ref[...] = (jnp.dot(hi, wq, preferred_element_type=F32) + jnp.dot(lo, wq, preferred_element_type=F32)
                  + jnp.dot(hi, wq_lo_ref[...], preferred_element_type=F32))
    k_ref[...] = jnp.dot(hi, w_ref[:, d:2 * d], preferred_element_type=F32)
    v_ref[...] = jnp.dot(hi, w_ref[:, 2 * d:3 * d], preferred_element_type=F32)
    for blk in range(xm_ref.shape[0]):
        xm_ref[blk] = jnp.mean(xn[blk * MOBA_BLOCK:(blk + 1) * MOBA_BLOCK], axis=0, keepdims=True)


def qkv_proj(x, g, w, wq_lo, tm):
    m, d = x.shape
    assert tm % MOBA_BLOCK == 0
    row = pl.BlockSpec((tm, d), lambda i: (i, 0))
    out = jax.ShapeDtypeStruct((m, d), F32)
    return pl.pallas_call(
        _qkv_body,
        grid=(m // tm,),
        in_specs=[row, pl.BlockSpec((1, d), lambda i: (0, 0)),
                  pl.BlockSpec((d, 3 * d), lambda i: (0, 0), pipeline_mode=pl.Buffered(1)),
                  pl.BlockSpec((d, d), lambda i: (0, 0), pipeline_mode=pl.Buffered(1))],
        out_specs=[row, row, row, pl.BlockSpec((tm // MOBA_BLOCK, 1, d), lambda i: (i, 0, 0))],
        out_shape=[out, out, out, jax.ShapeDtypeStruct((m // MOBA_BLOCK, 1, d), F32)],
        compiler_params=_params(("parallel",)),
        name="qkv_proj",
    )(x, g.reshape(1, d), w, wq_lo)


def _mm_f32_body(a_ref, w_ref, o_ref):
    o_ref[...] = jnp.dot(a_ref[...], w_ref[...], precision=lax.Precision.HIGHEST, preferred_element_type=F32)


def mm_f32(a, w):
    m, k = a.shape
    n = w.shape[1]
    return pl.pallas_call(
        _mm_f32_body,
        grid=(1,),
        in_specs=[pl.BlockSpec((m, k), lambda i: (0, 0)), pl.BlockSpec((k, n), lambda i: (0, 0))],
        out_specs=pl.BlockSpec((m, n), lambda i: (0, 0)),
        out_shape=jax.ShapeDtypeStruct((m, n), F32),
        compiler_params=_params(("arbitrary",)),
        name="mm_f32",
    )(a, w)


def _topk_select(gate, valid, blk, n_blocks):
    g = jnp.where(valid, gate, -jnp.inf)
    rank = jnp.zeros(g.shape, jnp.int32)
    for other in range(n_blocks):
        row = g[other:other + 1, :]
        beats = jnp.where(row > g, 1, jnp.where(row == g, jnp.where(other < blk, 1, 0), 0))
        rank = rank + beats
    return jnp.logical_and(valid, rank < MOBA_TOPK)


def _split3(x):
    hi = x.astype(BF16).astype(F32)
    r1 = x - hi
    mid = r1.astype(BF16).astype(F32)
    lo = (r1 - mid).astype(BF16).astype(F32)
    return hi, mid, lo


def _moba_prep_body(slopes_ref, q_ref, k_ref, v_ref, kmean_ref, qa_ref, ka_ref, va_ref, *, n_blocks):
    hp = pl.program_id(1)
    sb = pl.program_id(2)
    half = AUG // 2

    q = q_ref[...]
    k = k_ref[...]
    v = v_ref[...]
    kmean = kmean_ref[...]
    blk = lax.broadcasted_iota(jnp.int32, (n_blocks, MOBA_BLOCK), 0)
    valid = blk < sb
    lane = lax.broadcasted_iota(jnp.int32, (MOBA_BLOCK, half), 1)
    row = lax.broadcasted_iota(jnp.int32, (MOBA_BLOCK, half), 0)
    arow = lax.broadcasted_iota(jnp.int32, (SUBLANES, MOBA_BLOCK), 0)
    ones_rows = jnp.where(arow < 3, 1.0, 0.0).astype(F32)
    pad_rows = jnp.zeros((AUG - n_blocks - SUBLANES, MOBA_BLOCK), F32)
    v_extra = jnp.where(lane == 0, 1.0, 0.0).astype(F32)
    for hh in range(2):
        qh = q[:, hh * HEAD_DIM:(hh + 1) * HEAD_DIM]
        kh = k[:, hh * HEAD_DIM:(hh + 1) * HEAD_DIM]
        vh = v[:, hh * HEAD_DIM:(hh + 1) * HEAD_DIM]
        kmh = kmean[:, hh * HEAD_DIM:(hh + 1) * HEAD_DIM]
        gate = lax.dot_general(kmh, qh, _NT, precision=lax.Precision.HIGHEST,
                               preferred_element_type=F32)
        sel = _topk_select(gate, valid, blk, n_blocks)
        bias_t = jnp.where(sel, 0.0, jnp.where(blk == sb, 0.0, NEG_BIAS)).astype(F32)
        aug_t = jnp.concatenate([bias_t, ones_rows, pad_rows], axis=0)
        aug = aug_t.T[:, :half]
        qa_ref[0, hh] = jnp.concatenate([qh * (LOG2E * HEAD_DIM ** -0.5), aug], axis=1).astype(BF16)

        slope = slopes_ref[2 * hp + hh] * LOG2E
        pos = (sb * MOBA_BLOCK + row).astype(F32) * slope
        hi, mid, lo = _split3(pos)
        k_extra = jnp.where(lane < n_blocks, jnp.where(lane == sb, 1.0, 0.0),
                            jnp.where(lane == n_blocks, hi,
                                      jnp.where(lane == n_blocks + 1, mid,
                                                jnp.where(lane == n_blocks + 2, lo, 0.0))))
        ka_ref[0, hh] = jnp.concatenate([kh, k_extra], axis=1).astype(BF16)
        va_ref[0, hh] = jnp.concatenate([vh, v_extra], axis=1).astype(BF16)


def moba_prep(q, k, v, kmean, slopes, n_seq, seq, n_heads):
    n_blocks = seq // MOBA_BLOCK
    assert n_blocks + SUBLANES <= AUG // 2 and n_blocks % SUBLANES == 0
    hpairs = n_heads // 2
    aug_shape = jax.ShapeDtypeStruct((n_seq, n_heads, seq, AUG), BF16)
    aug_spec = pl.BlockSpec((1, 2, MOBA_BLOCK, AUG), lambda b, hp, sb: (b, hp, sb, 0))
    col = pl.BlockSpec((MOBA_BLOCK, LANES), lambda b, hp, sb: (b * n_blocks + sb, hp))
    return pl.pallas_call(
        functools.partial(_moba_prep_body, n_blocks=n_blocks),
        grid=(n_seq, hpairs, n_blocks),
        in_specs=[pl.BlockSpec(memory_space=pltpu.SMEM), col, col, col,
                  pl.BlockSpec((n_blocks, LANES), lambda b, hp, sb: (b, hp))],
        out_specs=[aug_spec, aug_spec, aug_spec],
        out_shape=[aug_shape, aug_shape, aug_shape],
        compiler_params=_params(("parallel", "parallel", "parallel")),
        name="moba_prep",
    )(slopes, q, k, v, kmean)


def _moba_attn_body(qa_ref, ka_ref, va_ref, o_ref, *, n_blocks):
    t = MOBA_BLOCK
    grp = ATTN_GROUP
    rr = lax.broadcasted_iota(jnp.int32, (t, t), 0)
    cc = lax.broadcasted_iota(jnp.int32, (t, t), 1)
    causal = cc <= rr
    chains = [(hh, a) for hh in range(2) for a in range(grp)]

    def tile(q, m, acc, hh, kj, diag):
        start = pl.multiple_of(kj * t, t)
        s = lax.dot_general(q, ka_ref[0, hh, pl.ds(start, t), :], _NT, preferred_element_type=F32)
        if diag:
            s = jnp.where(causal, s, -jnp.inf)
        m_new = jnp.maximum(m, jnp.max(s, axis=-1, keepdims=True))
        p = jnp.exp2(s - m_new)
        acc = jnp.exp2(m - m_new) * acc + jnp.dot(p.astype(BF16), va_ref[0, hh, pl.ds(start, t), :],
                                                  preferred_element_type=F32)
        return m_new, acc

    def group(g, carry):
        base = g * grp
        qs = [qa_ref[0, hh, pl.ds(pl.multiple_of((base + a) * t, t), t), :] for hh, a in chains]

        def full_tiles(kj, state):
            out = []
            for c, (hh, _) in enumerate(chains):
                out.extend(tile(qs[c], state[2 * c], state[2 * c + 1], hh, kj, False))
            return tuple(out)

        init = []
        for _ in chains:
            init.extend([jnp.full((t, 1), -jnp.inf, F32), jnp.zeros((t, AUG), F32)])
        state = lax.fori_loop(0, base, full_tiles, tuple(init))

        outs = {}
        for c, (hh, a) in enumerate(chains):
            m, acc = state[2 * c], state[2 * c + 1]
            for b in range(a + 1):
                m, acc = tile(qs[c], m, acc, hh, base + b, b == a)
            outs[(hh, a)] = acc[:, :HEAD_DIM] / acc[:, HEAD_DIM:HEAD_DIM + 1]
        for a in range(grp):
            o_ref[pl.ds(pl.multiple_of((base + a) * t, t), t), :] = jnp.concatenate(
                [outs[(0, a)], outs[(1, a)]], axis=1).astype(BF16)
        return carry

    lax.fori_loop(0, n_blocks // grp, group, 0)


def moba_attn(qa, ka, va):
    n_seq, n_heads, seq, _ = qa.shape
    n_blocks = seq // MOBA_BLOCK
    assert n_blocks % ATTN_GROUP == 0
    spec = pl.BlockSpec((1, 2, seq, AUG), lambda b, hp: (b, hp, 0, 0))
    return pl.pallas_call(
        functools.partial(_moba_attn_body, n_blocks=n_blocks),
        grid=(n_seq, n_heads // 2),
        in_specs=[spec, spec, spec],
        out_specs=pl.BlockSpec((seq, LANES), lambda b, hp: (b, hp)),
        out_shape=jax.ShapeDtypeStruct((n_seq * seq, n_heads * HEAD_DIM), BF16),
        compiler_params=_params(("parallel", "parallel")),
        name="moba_attn",
    )(qa, ka, va)


def _sample_attn_body(pt_ref, rowinfo_ref, q_ref, kn_ref, vn_ref, *rest, n_pages, pps, n_new, n_heads):
    k_refs = rest[:pps]
    v_refs = rest[pps:2 * pps]
    o_ref = rest[2 * pps]
    qcol_ref, s_ref, km_ref, acc_ref, knt_ref, vnt_ref, gate_ref, bias_ref = rest[2 * pps + 1:]
    t = pl.program_id(1)
    tk = n_pages // pps
    page = k_refs[0].shape[3]
    d = q_ref.shape[1]
    past = n_pages * page
    n_blocks = past // MOBA_BLOCK
    ppb = MOBA_BLOCK // page
    n_rows = n_heads * n_new
    hpairs = n_heads // 2
    lane_i = lax.broadcasted_iota(jnp.int32, (1, LANES), 1)

    def head_major(rows_val):
        padded = jnp.concatenate([rows_val, jnp.zeros((LANES - rows_val.shape[0], d), F32)], axis=0)
        return padded.T.reshape(n_heads, HEAD_DIM, LANES)

    @pl.when(t == 0)
    def _():
        q_t = head_major(q_ref[...])
        for h in range(n_heads):
            for n in range(n_new):
                qcol_ref[h * n_new + n] = jnp.broadcast_to(q_t[h, :, n:n + 1], (HEAD_DIM, LANES))
        knt_ref[...] = head_major(kn_ref[...])
        vnt_ref[...] = head_major(vn_ref[...])
        acc_ref[...] = jnp.zeros_like(acc_ref)
        km_ref[...] = jnp.zeros_like(km_ref)

    def score_rows(kt, h):
        return [jnp.sum(kt * qcol_ref[h * n_new + n], axis=0, keepdims=True) for n in range(n_new)]

    @pl.when(t < tk)
    def _():
        def pair(hp, carry):
            row0 = pl.multiple_of(hp * 2 * n_new, 2 * n_new)
            for bi in range(pps // ppb):
                blk = t * (pps // ppb) + bi
                onehot = jnp.where(lane_i == blk, 1.0, 0.0).astype(F32)
                rows = [[] for _ in range(ppb)]
                for hh in range(2):
                    h = 2 * hp + hh
                    ksum = jnp.zeros((HEAD_DIM, LANES), F32)
                    for pi in range(ppb):
                        kt = k_refs[bi * ppb + pi][0, h]
                        ksum = ksum + kt
                        rows[pi].extend(score_rows(kt, h))
                    km_ref[h] += jnp.sum(ksum, axis=1, keepdims=True) * onehot
                for pi in range(ppb):
                    s_ref[t * pps + bi * ppb + pi, pl.ds(row0, 2 * n_new), :] = jnp.concatenate(rows[pi], axis=0)
            return carry

        lax.fori_loop(0, hpairs, pair, 0)

    @pl.when(t == tk - 1)
    def _():
        def new_and_gate(hp, carry):
            row0 = pl.multiple_of(hp * 2 * n_new, 2 * n_new)
            srows, grows = [], []
            for hh in range(2):
                h = 2 * hp + hh
                srows.extend(score_rows(knt_ref[h], h))
                grows.extend(score_rows(km_ref[h], h))
            s_ref[n_pages, pl.ds(row0, 2 * n_new), :] = jnp.concatenate(srows, axis=0)
            gate_ref[pl.ds(row0, 2 * n_new), :] = jnp.concatenate(grows, axis=0) * (1.0 / MOBA_BLOCK)
            return carry

        lax.fori_loop(0, hpairs, new_and_gate, 0)
        gate_t = jnp.concatenate([gate_ref[...], jnp.zeros((LANES - n_rows, LANES), F32)], axis=0).T
        blk = lax.broadcasted_iota(jnp.int32, (n_blocks, LANES), 0)
        sel_t = _topk_select(gate_t[:n_blocks], blk >= 0, blk, n_blocks)
        sel = jnp.concatenate([jnp.where(sel_t, 1.0, 0.0).astype(F32),
                               jnp.zeros((LANES - n_blocks, LANES), F32)], axis=0).T[:n_rows]
        q_idx = rowinfo_ref[0]
        slope = rowinfo_ref[1]
        lane_f = lax.broadcasted_iota(jnp.int32, (n_rows, LANES), 1).astype(F32)
        scale = HEAD_DIM ** -0.5

        for b in range(n_blocks):
            bias_ref[b] = jnp.broadcast_to(jnp.where(sel[:, b:b + 1] > 0.5, 0.0, -jnp.inf), (n_rows, LANES))

        def logits(p):
            dist = (past + q_idx) - ((p * page).astype(F32) + lane_f)
            return s_ref[p] * scale - slope * dist + bias_ref[p // ppb]

        dist_new = q_idx - lane_f
        s_new = jnp.where((dist_new >= 0) & (lane_f < n_new), s_ref[n_pages] * scale - slope * dist_new, -jnp.inf)
        m = jnp.max(lax.fori_loop(0, n_pages, lambda p, mv: jnp.maximum(mv, logits(p)), s_new),
                    axis=1, keepdims=True)

        def exp_step(p, den):
            e = jnp.exp(logits(p) - m)
            s_ref[p] = e
            return den + e

        e_new = jnp.exp(s_new - m)
        inv = 1.0 / jnp.sum(lax.fori_loop(0, n_pages, exp_step, e_new), axis=1, keepdims=True)
        s_ref[n_pages] = e_new * inv

        def norm_step(p, carry):
            s_ref[p] = s_ref[p] * inv
            return carry

        lax.fori_loop(0, n_pages, norm_step, 0)

    def accumulate(hp, pages, v_of):
        row0 = pl.multiple_of(hp * 2 * n_new, 2 * n_new)
        probs = [s_ref[p, pl.ds(row0, 2 * n_new), :] for p in pages]
        for hh in range(2):
            h = 2 * hp + hh
            accs = [acc_ref[h * n_new + n] for n in range(n_new)]
            for i, prob in enumerate(probs):
                vt = v_of(i, h)
                for n in range(n_new):
                    accs[n] = accs[n] + prob[hh * n_new + n:hh * n_new + n + 1, :] * vt
            for n in range(n_new):
                acc_ref[h * n_new + n] = accs[n]

    @pl.when(t >= tk)
    def _():
        def pair(hp, carry):
            accumulate(hp, [(t - tk) * pps + i for i in range(pps)], lambda i, h: v_refs[i][0, h])
            return carry

        lax.fori_loop(0, hpairs, pair, 0)

    @pl.when(t == 2 * tk - 1)
    def _():
        def pair(hp, carry):
            accumulate(hp, [n_pages], lambda i, h: vnt_ref[h])
            return carry

        lax.fori_loop(0, hpairs, pair, 0)
        ones = jnp.ones((SUBLANES, LANES), F32)
        outs = []
        for n in range(n_new):
            a_n = jnp.concatenate([acc_ref[h * n_new + n] for h in range(n_heads)], axis=0)
            outs.append(lax.dot_general(ones, a_n, _NT, precision=lax.Precision.HIGHEST,
                                        preferred_element_type=F32)[0:1])
        outs.append(jnp.zeros((SAMPLE_PAD - n_new, d), F32))
        o_ref[...] = jnp.concatenate(outs, axis=0).astype(BF16)


def sample_attn(page_table, rowinfo, q, k_new, v_new, cache_kt, cache_vt, n_new, pps=8):
    n_dec, n_pages = page_table.shape
    _, n_heads, _, page = cache_kt.shape
    d = n_heads * HEAD_DIM
    ppb = MOBA_BLOCK // page
    assert n_pages % pps == 0 and pps % ppb == 0 and page == LANES
    assert 2 * n_new == SUBLANES and n_new <= SAMPLE_PAD
    tk = n_pages // pps
    n_blocks = n_pages // ppb
    n_rows = n_heads * n_new
    assert n_blocks <= LANES and n_rows <= LANES

    def k_spec(i):
        return pl.BlockSpec((1, n_heads, HEAD_DIM, page),
                            lambda b, t, pt: (pt[b, jnp.minimum(t, tk - 1) * pps + i], 0, 0, 0))

    def v_spec(i):
        return pl.BlockSpec((1, n_heads, HEAD_DIM, page),
                            lambda b, t, pt: (pt[b, jnp.maximum(t - tk, 0) * pps + i], 0, 0, 0))

    row_spec = pl.BlockSpec((SAMPLE_PAD, d), lambda b, t, pt: (b, 0))
    grid_spec = pltpu.PrefetchScalarGridSpec(
        num_scalar_prefetch=1,
        grid=(n_dec, 2 * tk),
        in_specs=[pl.BlockSpec((2, n_rows, LANES), lambda b, t, pt: (0, 0, 0)), row_spec, row_spec, row_spec]
                 + [k_spec(i) for i in range(pps)] + [v_spec(i) for i in range(pps)],
        out_specs=row_spec,
        scratch_shapes=[pltpu.VMEM((n_rows, HEAD_DIM, LANES), F32),
                        pltpu.VMEM((n_pages + 1, n_rows, LANES), F32),
                        pltpu.VMEM((n_heads, HEAD_DIM, LANES), F32),
                        pltpu.VMEM((n_rows, HEAD_DIM, LANES), F32),
                        pltpu.VMEM((n_heads, HEAD_DIM, LANES), F32),
                        pltpu.VMEM((n_heads, HEAD_DIM, LANES), F32),
                        pltpu.VMEM((n_rows, LANES), F32),
                        pltpu.VMEM((n_blocks, n_rows, LANES), F32)],
    )
    return pl.pallas_call(
        functools.partial(_sample_attn_body, n_pages=n_pages, pps=pps, n_new=n_new, n_heads=n_heads),
        grid_spec=grid_spec,
        out_shape=jax.ShapeDtypeStruct((n_dec * SAMPLE_PAD, d), BF16),
        compiler_params=_params(("parallel", "arbitrary")),
        name="sample_attn",
    )(page_table, rowinfo, q, k_new, v_new, *([cache_kt] * pps), *([cache_vt] * pps))


def _ssd_body(z_ref, xs_ref, bc_ref, dt_ref, cinit_ref, sinit_ref, cw_ref, cb_ref, dtb_ref, alog_ref,
              dsk_ref, gn_ref, y_ref, cout_ref, sout_ref, ext_ref, xc_ref, h_ref, *, rows, valid, n_chunks):
    cl = SSD_CHUNK
    c = pl.program_id(1)
    d_inner = xs_ref.shape[1]
    n_heads = d_inner // SSM_HEAD_DIM
    hpg = n_heads // N_GROUPS
    gw = hpg * SSM_HEAD_DIM
    tail = SUBLANES

    @pl.when(c == 0)
    def _():
        ext_ref[0:tail, :] = cinit_ref[0]
        h_ref[...] = sinit_ref[0].reshape(h_ref.shape)
        if rows < cl:
            ext_ref[tail + rows:, :] = jnp.zeros((cl - rows, ext_ref.shape[1]), F32)

    @pl.when(c > 0)
    def _():
        ext_ref[0:tail, :] = ext_ref[cl:cl + tail, :]

    ext_ref[tail:tail + rows, 0:d_inner] = xs_ref[...]
    ext_ref[tail:tail + rows, d_inner:] = bc_ref[...]

    cw = cw_ref[...]
    conv = cb_ref[...] + ext_ref[tail:tail + cl, :] * cw[CONV_W - 1:CONV_W, :]
    for back in range(1, CONV_W):
        conv = conv + ext_ref[tail - back:tail - back + cl, :] * cw[CONV_W - 1 - back:CONV_W - back, :]
    xc_ref[...] = _silu(conv)

    def pad(v):
        if rows == cl:
            return v
        return jnp.concatenate([v, jnp.zeros((cl - rows, v.shape[1]), v.dtype)], axis=0)

    trow = lax.broadcasted_iota(jnp.int32, (cl, LANES), 0)
    dt_raw = pad(dt_ref[...]) + dtb_ref[...]
    dt = jnp.maximum(dt_raw, 0.0) + jnp.log1p(jnp.exp(-jnp.abs(dt_raw)))
    dt = jnp.where(trow < valid, dt, 0.0)
    a = -jnp.exp(alog_ref[...])
    tri_r = lax.broadcasted_iota(jnp.int32, (cl, cl), 0)
    tri_c = lax.broadcasted_iota(jnp.int32, (cl, cl), 1)
    causal = tri_c <= tri_r
    a_cs = jnp.dot(jnp.where(causal, 1.0, 0.0).astype(F32), dt * a, precision=lax.Precision.HIGHEST,
                   preferred_element_type=F32)
    a_cs_t = a_cs.T
    dt_t = dt.T
    z = pad(z_ref[...])

    for g in range(N_GROUPS):
        bg = xc_ref[:, d_inner + g * D_STATE:d_inner + (g + 1) * D_STATE].astype(BF16)
        cg = xc_ref[:, d_inner + (N_GROUPS + g) * D_STATE:d_inner + (N_GROUPS + g + 1) * D_STATE].astype(BF16)
        cb = lax.dot_general(cg, bg, _NT, preferred_element_type=F32)
        xg = xc_ref[:, g * gw:(g + 1) * gw]
        xg_t = xg.T
        h_old = h_ref[g * gw:(g + 1) * gw, :]
        y_off = lax.dot_general(cg, h_old.astype(BF16), _NT, preferred_element_type=F32)
        ys, xw_rows, h_scaled = [], [], []
        for e in range(hpg):
            h = g * hpg + e
            acs_col = a_cs[:, h:h + 1]
            acs_row = a_cs_t[h:h + 1, :]
            dt_row = dt_t[h:h + 1, :]
            a_end = a_cs_t[h:h + 1, cl - 1:cl]
            decay = jnp.exp(jnp.where(causal, acs_col - acs_row, -jnp.inf))
            w = (cb * decay * dt_row).astype(BF16)
            xh = xg[:, e * SSM_HEAD_DIM:(e + 1) * SSM_HEAD_DIM]
            y_diag = jnp.dot(w, xh.astype(BF16), preferred_element_type=F32)
            ys.append(y_diag + y_off[:, e * SSM_HEAD_DIM:(e + 1) * SSM_HEAD_DIM] * jnp.exp(acs_col))
            to_end = jnp.exp(a_end - acs_row) * dt_row
            xw_rows.append(xg_t[e * SSM_HEAD_DIM:(e + 1) * SSM_HEAD_DIM, :] * to_end)
            h_scaled.append(h_old[e * SSM_HEAD_DIM:(e + 1) * SSM_HEAD_DIM, :] * jnp.exp(a_end))
        states = jnp.dot(jnp.concatenate(xw_rows, axis=0).astype(BF16), bg, preferred_element_type=F32)
        h_ref[g * gw:(g + 1) * gw, :] = jnp.concatenate(h_scaled, axis=0) + states
        y = jnp.concatenate(ys, axis=1) + dsk_ref[:, g * gw:(g + 1) * gw] * xg
        yz = y * _silu(z[:, g * gw:(g + 1) * gw])
        ms = jnp.mean(yz * yz, axis=-1, keepdims=True)
        yn = yz * lax.rsqrt(ms + EPS) * gn_ref[:, g * gw:(g + 1) * gw]
        y_ref[:, g * gw:(g + 1) * gw] = yn[:rows].astype(BF16)

    @pl.when(c == n_chunks - 1)
    def _():
        sout_ref[0] = h_ref[...].reshape(sout_ref.shape[1:])
        first = tail + ((valid - (CONV_W - 1)) // SUBLANES) * SUBLANES
        cout_ref[0] = ext_ref[first:first + SUBLANES, :]


def ssd_mixer(proj, conv_init, ssm_init, cw, cb, dtb, alog, dsk, gnorm, n_seq, rows, valid, d_inner):
    total_rows = proj.shape[0] // n_seq
    n_chunks = total_rows // rows
    assert rows == SSD_CHUNK or n_chunks == 1
    assert CONV_W - 1 <= valid <= rows
    n_heads = d_inner // SSM_HEAD_DIM
    conv_dim = d_inner + 2 * N_GROUPS * D_STATE
    assert conv_dim == 2 * d_inner

    def col(width, idx):
        return pl.BlockSpec((rows, width), lambda b, c: (b * n_chunks + c, idx))

    def const(shape):
        return pl.BlockSpec(shape, lambda b, c: (0,) * len(shape))

    outs = pl.pallas_call(
        functools.partial(_ssd_body, rows=rows, valid=valid, n_chunks=n_chunks),
        grid=(n_seq, n_chunks),
        in_specs=[col(d_inner, 0), col(d_inner, 1), col(d_inner, 2), col(LANES, 3 * d_inner // LANES),
                  pl.BlockSpec((1, SUBLANES, conv_dim), lambda b, c: (b, 0, 0)),
                  pl.BlockSpec((1, n_heads, SSM_HEAD_DIM, D_STATE), lambda b, c: (b, 0, 0, 0)),
                  const((CONV_W, conv_dim)), const((1, conv_dim)), const((1, LANES)), const((1, LANES)),
                  const((1, d_inner)), const((1, d_inner))],
        out_specs=[pl.BlockSpec((rows, d_inner), lambda b, c: (b * n_chunks + c, 0)),
                   pl.BlockSpec((1, SUBLANES, conv_dim), lambda b, c: (b, 0, 0)),
                   pl.BlockSpec((1, n_heads, SSM_HEAD_DIM, D_STATE), lambda b, c: (b, 0, 0, 0))],
        out_shape=[jax.ShapeDtypeStruct((proj.shape[0], d_inner), BF16),
                   jax.ShapeDtypeStruct((n_seq, SUBLANES, conv_dim), F32),
                   jax.ShapeDtypeStruct((n_seq, n_heads, SSM_HEAD_DIM, D_STATE), F32)],
        scratch_shapes=[pltpu.VMEM((SUBLANES + SSD_CHUNK, conv_dim), F32),
                        pltpu.VMEM((SSD_CHUNK, conv_dim), F32),
                        pltpu.VMEM((n_heads * SSM_HEAD_DIM, D_STATE), F32)],
        compiler_params=_params(("parallel", "arbitrary")),
        name="ssd_mixer",
    )(proj, proj, proj, proj, conv_init, ssm_init, cw, cb, dtb, alog, dsk, gnorm)
    return outs


def _pad_cols(w, n):
    return jnp.pad(w, ((0, 0), (0, n - w.shape[1])))


def _pad_lanes(v):
    return jnp.pad(v, (0, LANES - v.shape[0])).reshape(1, LANES)


def kernel(x_prompt, x_sample, cache_k, cache_v, state_conv, state_ssm, page_table, norm_mix, norm_ffn, w_qkv, w_o, w_in_ssm, conv_w, conv_b, dt_bias, a_log, d_skip, norm_ssm, w_out_ssm, w_gate_up, w_down, norm_final):
    n_seq, seq, d_model = x_prompt.shape
    n_dec, n_new, _ = x_sample.shape
    n_heads = d_model // HEAD_DIM
    d_inner = norm_ssm.shape[1]
    ssm_heads = d_inner // SSM_HEAD_DIM
    conv_dim = conv_w.shape[2]
    tm_p = 512
    tm_s = n_dec * SAMPLE_PAD

    hp = x_prompt.reshape(n_seq * seq, d_model)
    hs = jnp.pad(x_sample, ((0, 0), (0, SAMPLE_PAD - n_new), (0, 0))).reshape(tm_s, d_model)

    wqkv = w_qkv[0]
    w_hi = wqkv.astype(BF16)
    wq_lo = (wqkv[:, :d_model] - w_hi[:, :d_model].astype(F32)).astype(BF16)
    q_p, k_p, v_p, xm_p = qkv_proj(hp, norm_mix[0], w_hi, wq_lo, tm_p)
    q_s, k_s, v_s, _ = qkv_proj(hs, norm_mix[0], w_hi, wq_lo, tm_s)
    kmean = mm_f32(xm_p.reshape(n_seq * seq // MOBA_BLOCK, d_model), wqkv[:, d_model:2 * d_model])
    slopes = jnp.exp2(-8.0 * (jnp.arange(n_heads, dtype=F32) + 1.0) / n_heads)
    qa, ka, va = moba_prep(q_p, k_p, v_p, kmean, slopes, n_seq, seq, n_heads)
    attn_p = moba_attn(qa, ka, va)
    rows = jnp.arange(n_heads * n_new)
    rowinfo = jnp.stack([(rows % n_new).astype(F32), slopes[rows // n_new]])
    rowinfo = jnp.broadcast_to(rowinfo[:, :, None], (2, n_heads * n_new, LANES))
    cache_kt = jnp.transpose(cache_k[0], (0, 2, 3, 1))
    cache_vt = jnp.transpose(cache_v[0], (0, 2, 3, 1))
    attn_s = sample_attn(page_table, rowinfo, q_s, k_s, v_s, cache_kt, cache_vt, n_new)
    wo = w_o[0].astype(BF16)
    hp = mm_res(attn_p, wo, hp, tm_p)
    hs = mm_res(attn_s, wo, hs, tm_s)

    k_prompt = k_p.reshape(1, n_seq, seq, n_heads, HEAD_DIM)
    v_prompt = v_p.reshape(1, n_seq, seq, n_heads, HEAD_DIM)
    k_sample = k_s.reshape(n_dec, SAMPLE_PAD, n_heads, HEAD_DIM)[None, :, :n_new]
    v_sample = v_s.reshape(n_dec, SAMPLE_PAD, n_heads, HEAD_DIM)[None, :, :n_new]

    def ffn(h, layer, tm, final_g=None):
        act = swiglu_gu(h, norm_ffn[layer], w_gate_up[layer].astype(BF16), tm, 256)
        return mm_res(act, w_down[layer].astype(BF16), h, tm, final_g)

    hp = ffn(hp, 0, tm_p)
    hs = ffn(hs, 0, tm_s)

    in_cols = d_inner + conv_dim + ssm_heads
    assert w_in_ssm.shape[2] == in_cols
    proj_w = ((in_cols + 1279) // 1280) * 1280
    w_in = _pad_cols(w_in_ssm[0], proj_w).astype(BF16)
    proj_p = norm_mm(hp, norm_mix[1], w_in, tm_p, 1280)
    proj_s = norm_mm(hs, norm_mix[1], w_in, tm_s, 1280)
    cb = conv_b[0].reshape(1, conv_dim)
    dtb = _pad_lanes(dt_bias[0])
    alog = _pad_lanes(a_log[0])
    dsk = jnp.repeat(d_skip[0], SSM_HEAD_DIM).reshape(1, d_inner)
    gn = norm_ssm[0].reshape(1, d_inner)
    zero_conv = jnp.zeros((n_seq, SUBLANES, conv_dim), F32)
    zero_ssm = jnp.zeros((n_seq, ssm_heads, SSM_HEAD_DIM, D_STATE), F32)
    y_p, conv_p, ssm_p = ssd_mixer(proj_p, zero_conv, zero_ssm, conv_w[0], cb, dtb, alog, dsk, gn,
                                   n_seq, SSD_CHUNK, SSD_CHUNK, d_inner)
    conv_init_s = jnp.pad(state_conv[0], ((0, 0), (SUBLANES - (CONV_W - 1), 0), (0, 0)))
    y_s, conv_s, ssm_s = ssd_mixer(proj_s, conv_init_s, state_ssm[0], conv_w[0], cb, dtb, alog, dsk, gn,
                                   n_dec, SAMPLE_PAD, n_new, d_inner)
    w_out = w_out_ssm[0].astype(BF16)
    hp = mm_res(y_p, w_out, hp, tm_p)
    hs = mm_res(y_s, w_out, hs, tm_s)
    y_prompt = ffn(hp, 1, tm_p, norm_final).reshape(n_seq, seq, d_model)
    y_sample = ffn(hs, 1, tm_s, norm_final).reshape(n_dec, SAMPLE_PAD, d_model)[:, :n_new]

    lo_p = SUBLANES - (CONV_W - 1)
    lo_s = (n_new - (CONV_W - 1)) % SUBLANES
    return (y_prompt, y_sample, k_prompt, v_prompt, k_sample, v_sample,
            conv_p[None, :, lo_p:lo_p + CONV_W - 1], ssm_p[None],
            conv_s[None, :, lo_s:lo_s + CONV_W - 1], ssm_s[None])
```

```python
import functools
import math

import jax
import jax.numpy as jnp
from jax import lax
from jax.experimental import pallas as pl
from jax.experimental.pallas import tpu as pltpu

F32 = jnp.float32
BF16 = jnp.bfloat16

EPS = 1e-5
HEAD_DIM = 64
MOBA_BLOCK = 256
MOBA_TOPK = 3
SSM_HEAD_DIM = 64
N_GROUPS = 8
D_STATE = 128
CONV_W = 4
SSD_CHUNK = 128
SAMPLE_PAD = 16
AUG = 128
NEG_BIAS = -1e30
LOG2E = math.log2(math.e)
ATTN_GROUP = 4
PREP_BLOCKS = 2
LANES = 128
SUBLANES = 8
VMEM_LIMIT = 48 * 1024 * 1024
VMEM_LIMIT_BIG = 56 * 1024 * 1024

_NT = (((1,), (1,)), ((), ()))


def _silu(x):
    return x / (1.0 + jnp.exp(-x))


def _params(sem, vmem=VMEM_LIMIT):
    return pltpu.CompilerParams(dimension_semantics=sem, vmem_limit_bytes=vmem)


def _rms_bf16(x_ref, g_ref):
    x = x_ref[...]
    ms = jnp.mean(x * x, axis=-1, keepdims=True)
    return (x * lax.rsqrt(ms + EPS) * g_ref[...]).astype(BF16)


def _norm_mm_body(x_ref, g_ref, w_ref, o_ref, *, tn):
    xn = _rms_bf16(x_ref, g_ref)
    for c in range(w_ref.shape[1] // tn):
        o_ref[:, c * tn:(c + 1) * tn] = jnp.dot(xn, w_ref[:, c * tn:(c + 1) * tn], preferred_element_type=F32)


def norm_mm(x, g, w, tm, tn):
    m, d = x.shape
    n = w.shape[1]
    assert n % tn == 0
    return pl.pallas_call(
        functools.partial(_norm_mm_body, tn=tn),
        grid=(m // tm,),
        in_specs=[pl.BlockSpec((tm, d), lambda i: (i, 0)),
                  pl.BlockSpec((1, d), lambda i: (0, 0)),
                  pl.BlockSpec((d, n), lambda i: (0, 0), pipeline_mode=pl.Buffered(1))],
        out_specs=pl.BlockSpec((tm, n), lambda i: (i, 0)),
        out_shape=jax.ShapeDtypeStruct((m, n), F32),
        compiler_params=_params(("parallel",), VMEM_LIMIT_BIG),
        name="norm_mm",
    )(x, g.reshape(1, d), w)


def _swiglu_gu_body(x_ref, g_ref, w_ref, o_ref, *, tn):
    xn = _rms_bf16(x_ref, g_ref)
    dff = o_ref.shape[1]
    for c in range(dff // tn):
        gate = jnp.dot(xn, w_ref[:, c * tn:(c + 1) * tn], preferred_element_type=F32)
        up = jnp.dot(xn, w_ref[:, dff + c * tn:dff + (c + 1) * tn], preferred_element_type=F32)
        o_ref[:, c * tn:(c + 1) * tn] = (_silu(gate) * up).astype(BF16)


def swiglu_gu(x, g, w_gu, tm, tn):
    m, d = x.shape
    dff = w_gu.shape[1] // 2
    assert dff % tn == 0
    return pl.pallas_call(
        functools.partial(_swiglu_gu_body, tn=tn),
        grid=(m // tm,),
        in_specs=[pl.BlockSpec((tm, d), lambda i: (i, 0)),
                  pl.BlockSpec((1, d), lambda i: (0, 0)),
                  pl.BlockSpec((d, 2 * dff), lambda i: (0, 0), pipeline_mode=pl.Buffered(1))],
        out_specs=pl.BlockSpec((tm, dff), lambda i: (i, 0)),
        out_shape=jax.ShapeDtypeStruct((m, dff), BF16),
        compiler_params=_params(("parallel",)),
        name="swiglu_gu",
    )(x, g.reshape(1, d), w_gu)


def _mm_res_body(a_ref, w_ref, r_ref, o_ref):
    o_ref[...] = r_ref[...] + jnp.dot(a_ref[...], w_ref[...], preferred_element_type=F32)


def _mm_res_norm_body(a_ref, w_ref, r_ref, g_ref, o_ref):
    y = r_ref[...] + jnp.dot(a_ref[...], w_ref[...], preferred_element_type=F32)
    ms = jnp.mean(y * y, axis=-1, keepdims=True)
    o_ref[...] = y * lax.rsqrt(ms + EPS) * g_ref[...]


def mm_res(a, w, res, tm, final_g=None):
    m, k = a.shape
    n = w.shape[1]
    in_specs = [pl.BlockSpec((tm, k), lambda i: (i, 0)),
                pl.BlockSpec((k, n), lambda i: (0, 0)),
                pl.BlockSpec((tm, n), lambda i: (i, 0))]
    args = [a, w, res]
    body = _mm_res_body
    if final_g is not None:
        in_specs.append(pl.BlockSpec((1, n), lambda i: (0, 0)))
        args.append(final_g.reshape(1, n))
        body = _mm_res_norm_body
    return pl.pallas_call(
        body,
        grid=(m // tm,),
        in_specs=in_specs,
        out_specs=pl.BlockSpec((tm, n), lambda i: (i, 0)),
        out_shape=jax.ShapeDtypeStruct((m, n), F32),
        compiler_params=_params(("parallel",)),
        name="mm_res_norm" if final_g is not None else "mm_res",
    )(*args)


def _qkv_body(x_ref, g_ref, w_ref, wq_lo_ref, q_ref, k_ref, v_ref, xm_ref):
    d = x_ref.shape[1]
    x = x_ref[...]
    ms = jnp.mean(x * x, axis=-1, keepdims=True)
    xn = x * lax.rsqrt(ms + EPS) * g_ref[...]
    hi = xn.astype(BF16)
    lo = (xn - hi.astype(F32)).astype(BF16)
    wq = w_ref[:, 0:d]
    q_ref[...] = (jnp.dot(hi, wq, preferred_element_type=F32) + jnp.dot(lo, wq, preferred_element_type=F32)
                  + jnp.dot(hi, wq_lo_ref[...], preferred_element_type=F32))
    k_ref[...] = jnp.dot(hi, w_ref[:, d:2 * d], preferred_element_type=F32)
    v_ref[...] = jnp.dot(hi, w_ref[:, 2 * d:3 * d], preferred_element_type=F32)
    for blk in range(xm_ref.shape[0]):
        xm_ref[blk] = jnp.mean(xn[blk * MOBA_BLOCK:(blk + 1) * MOBA_BLOCK], axis=0, keepdims=True)


def qkv_proj(x, g, w, wq_lo, tm):
    m, d = x.shape
    assert tm % MOBA_BLOCK == 0
    row = pl.BlockSpec((tm, d), lambda i: (i, 0))
    out = jax.ShapeDtypeStruct((m, d), F32)
    return pl.pallas_call(
        _qkv_body,
        grid=(m // tm,),
        in_specs=[row, pl.BlockSpec((1, d), lambda i: (0, 0)),
                  pl.BlockSpec((d, 3 * d), lambda i: (0, 0), pipeline_mode=pl.Buffered(1)),
                  pl.BlockSpec((d, d), lambda i: (0, 0), pipeline_mode=pl.Buffered(1))],
        out_specs=[row, row, row, pl.BlockSpec((tm // MOBA_BLOCK, 1, d), lambda i: (i, 0, 0))],
        out_shape=[out, out, out, jax.ShapeDtypeStruct((m // MOBA_BLOCK, 1, d), F32)],
        compiler_params=_params(("parallel",)),
        name="qkv_proj",
    )(x, g.reshape(1, d), w, wq_lo)


def _mm_f32_body(a_ref, w_ref, o_ref):
    o_ref[...] = jnp.dot(a_ref[...], w_ref[...], precision=lax.Precision.HIGHEST, preferred_element_type=F32)


def mm_f32(a, w):
    m, k = a.shape
    n = w.shape[1]
    return pl.pallas_call(
        _mm_f32_body,
        grid=(1,),
        in_specs=[pl.BlockSpec((m, k), lambda i: (0, 0)), pl.BlockSpec((k, n), lambda i: (0, 0))],
        out_specs=pl.BlockSpec((m, n), lambda i: (0, 0)),
        out_shape=jax.ShapeDtypeStruct((m, n), F32),
        compiler_params=_params(("arbitrary",)),
        name="mm_f32",
    )(a, w)


def _topk_select(gate, valid, blk, n_blocks):
    g = jnp.where(valid, gate, -jnp.inf)
    rank = jnp.zeros(g.shape, jnp.int32)
    for other in range(n_blocks):
        row = g[other:other + 1, :]
        beats = jnp.where(row > g, 1, jnp.where(row == g, jnp.where(other < blk, 1, 0), 0))
        rank = rank + beats
    return jnp.logical_and(valid, rank < MOBA_TOPK)


def _bf16_trunc(x):
    bits = lax.bitcast_convert_type(x, jnp.uint32) & jnp.uint32(0xFFFF0000)
    return lax.bitcast_convert_type(bits, F32)


def _bf16_round(x):
    bits = lax.bitcast_convert_type(x, jnp.uint32)
    bits = (bits + jnp.uint32(0x7FFF) + ((bits >> 16) & jnp.uint32(1))) & jnp.uint32(0xFFFF0000)
    return lax.bitcast_convert_type(bits, F32)


def _split3(x):
    hi = _bf16_trunc(x)
    mid = _bf16_trunc(x - hi)
    return hi, mid, x - hi - mid


def _moba_prep_body(q_ref, k_ref, v_ref, kmean_ref, kx_ref, qa_ref, ka_ref, va_ref, *, n_blocks):
    half = AUG // 2
    kmean = kmean_ref[...]
    blk = lax.broadcasted_iota(jnp.int32, (n_blocks, MOBA_BLOCK), 0)
    lane = lax.broadcasted_iota(jnp.int32, (MOBA_BLOCK, AUG), 1)
    arow = lax.broadcasted_iota(jnp.int32, (SUBLANES, MOBA_BLOCK), 0)
    ones_rows = jnp.where(arow < 3, 1.0, 0.0).astype(F32)
    gap = jnp.zeros((half - n_blocks - SUBLANES, MOBA_BLOCK), F32)
    other = jnp.zeros((half, MOBA_BLOCK), F32)
    for j in range(q_ref.shape[0] // MOBA_BLOCK):
        sb = pl.program_id(1) * (q_ref.shape[0] // MOBA_BLOCK) + j
        rows = slice(j * MOBA_BLOCK, (j + 1) * MOBA_BLOCK)
        valid = blk < sb
        q = q_ref[rows, :] * (LOG2E * HEAD_DIM ** -0.5)
        k = k_ref[rows, :]
        v = v_ref[rows, :]
        for hh in range(2):
            first = hh == 0
            data = (lane < half) if first else (lane >= half)
            gate = lax.dot_general(kmean[:, hh * half:(hh + 1) * half], q_ref[rows, hh * half:(hh + 1) * half], _NT,
                                   precision=lax.Precision.HIGHEST, preferred_element_type=F32)
            sel = _topk_select(gate, valid, blk, n_blocks)
            bias_t = jnp.where(sel, 0.0, jnp.where(blk == sb, 0.0, NEG_BIAS)).astype(F32)
            extras_t = [bias_t, ones_rows, gap]
            aug = jnp.concatenate([other] + extras_t if first else extras_t + [other], axis=0).T
            qa_ref[0, hh, rows, :] = jnp.where(data, q, aug).astype(BF16)
            ka_ref[0, hh, rows, :] = jnp.where(data, k, kx_ref[hh, rows, :]).astype(BF16)
            ones_lane = half if first else 0
            va_ref[0, hh, rows, :] = jnp.where(data, v, jnp.where(lane == ones_lane, 1.0, 0.0)).astype(BF16)


def _key_extras(slopes, seq, n_blocks):
    half = AUG // 2
    pos = jnp.arange(seq, dtype=F32)
    val = pos[None, :] * (slopes * LOG2E)[:, None]
    parts = jnp.stack(_split3(val), axis=-1)
    onehot = jax.nn.one_hot(jnp.arange(seq) // MOBA_BLOCK, n_blocks, dtype=F32)
    onehot = jnp.broadcast_to(onehot[None], (slopes.shape[0], seq, n_blocks))
    extras = jnp.concatenate([onehot, parts, jnp.zeros((slopes.shape[0], seq, half - n_blocks - 3), F32)], axis=-1)
    zeros = jnp.zeros_like(extras)
    even = jnp.concatenate([zeros, extras], axis=-1)
    odd = jnp.concatenate([extras, zeros], axis=-1)
    is_even = (jnp.arange(slopes.shape[0]) % 2 == 0)[:, None, None]
    return jnp.where(is_even, even, odd)


def moba_prep(q, k, v, kmean, slopes, n_seq, seq, n_heads):
    n_blocks = seq // MOBA_BLOCK
    assert n_blocks + SUBLANES <= AUG // 2 and n_blocks % SUBLANES == 0
    hpairs = n_heads // 2
    kx = _key_extras(slopes, seq, n_blocks)
    rows = PREP_BLOCKS * MOBA_BLOCK
    n_steps = seq // rows
    assert seq % rows == 0
    aug_shape = jax.ShapeDtypeStruct((n_seq, n_heads, seq, AUG), BF16)
    aug_spec = pl.BlockSpec((1, 2, rows, AUG), lambda hp, sb, b: (b, hp, sb, 0))
    col = pl.BlockSpec((rows, LANES), lambda hp, sb, b: (b * n_steps + sb, hp))
    return pl.pallas_call(
        functools.partial(_moba_prep_body, n_blocks=n_blocks),
        grid=(hpairs, n_steps, n_seq),
        in_specs=[col, col, col,
                  pl.BlockSpec((n_blocks, LANES), lambda hp, sb, b: (b, hp)),
                  pl.BlockSpec((2, rows, AUG), lambda hp, sb, b: (hp, sb, 0))],
        out_specs=[aug_spec, aug_spec, aug_spec],
        out_shape=[aug_shape, aug_shape, aug_shape],
        compiler_params=_params(("parallel", "parallel", "parallel")),
        name="moba_prep",
    )(q, k, v, kmean, kx)


def _moba_attn_body(qa_ref, ka_ref, va_ref, o_ref, *, n_blocks):
    t = MOBA_BLOCK
    grp = ATTN_GROUP
    rr = lax.broadcasted_iota(jnp.int32, (t, t), 0)
    cc = lax.broadcasted_iota(jnp.int32, (t, t), 1)
    causal = cc <= rr
    out_lane = lax.broadcasted_iota(jnp.int32, (t, AUG), 1)
    chains = [(hh, a) for hh in range(2) for a in range(grp)]

    def tile(q, m, acc, hh, kj, diag):
        start = pl.multiple_of(kj * t, t)
        s = lax.dot_general(q, ka_ref[0, hh, pl.ds(start, t), :], _NT, preferred_element_type=F32)
        if diag:
            s = jnp.where(causal, s, -jnp.inf)
        m_new = jnp.maximum(m, jnp.max(s, axis=-1, keepdims=True))
        p = jnp.exp2(s - m_new)
        acc = jnp.exp2(m - m_new) * acc + jnp.dot(p.astype(BF16), va_ref[0, hh, pl.ds(start, t), :],
                                                  preferred_element_type=F32)
        return m_new, acc

    def group(g, carry):
        base = g * grp
        qs = [qa_ref[0, hh, pl.ds(pl.multiple_of((base + a) * t, t), t), :] for hh, a in chains]

        def full_tiles(kj, state):
            out = []
            for c, (hh, _) in enumerate(chains):
                out.extend(tile(qs[c], state[2 * c], state[2 * c + 1], hh, kj, False))
            return tuple(out)

        init = []
        for _ in chains:
            init.extend([jnp.full((t, 1), -jnp.inf, F32), jnp.zeros((t, AUG), F32)])
        state = lax.fori_loop(0, base, full_tiles, tuple(init))

        outs = {}
        for c, (hh, a) in enumerate(chains):
            m, acc = state[2 * c], state[2 * c + 1]
            for b in range(a + 1):
                m, acc = tile(qs[c], m, acc, hh, base + b, b == a)
            den_lane = HEAD_DIM if hh == 0 else 0
            outs[(hh, a)] = acc / acc[:, den_lane:den_lane + 1]
        for a in range(grp):
            o_ref[pl.ds(pl.multiple_of((base + a) * t, t), t), :] = jnp.where(
                out_lane < HEAD_DIM, outs[(0, a)], outs[(1, a)]).astype(BF16)
        return carry

    lax.fori_loop(0, n_blocks // grp, group, 0)


def moba_attn(qa, ka, va):
    n_seq, n_heads, seq, _ = qa.shape
    n_blocks = seq // MOBA_BLOCK
    assert n_blocks % ATTN_GROUP == 0
    spec = pl.BlockSpec((1, 2, seq, AUG), lambda b, hp: (b, hp, 0, 0))
    return pl.pallas_call(
        functools.partial(_moba_attn_body, n_blocks=n_blocks),
        grid=(n_seq, n_heads // 2),
        in_specs=[spec, spec, spec],
        out_specs=pl.BlockSpec((seq, LANES), lambda b, hp: (b, hp)),
        out_shape=jax.ShapeDtypeStruct((n_seq * seq, n_heads * HEAD_DIM), BF16),
        compiler_params=_params(("parallel", "parallel")),
        name="moba_attn",
    )(qa, ka, va)


def _sample_attn_body(pt_ref, rowinfo_ref, q_ref, kn_ref, vn_ref, *rest, n_pages, pps, n_new, n_heads):
    k_refs = rest[:pps]
    v_refs = rest[pps:2 * pps]
    o_ref = rest[2 * pps]
    qbd_ref, qbd32_ref, s_ref, ksum_ref, acc_ref, knt_ref, vnt_ref, bias_ref = rest[2 * pps + 1:]
    t = pl.program_id(1)
    tk = n_pages // pps
    page = k_refs[0].shape[3]
    d = q_ref.shape[1]
    past = n_pages * page
    n_blocks = past // MOBA_BLOCK
    ppb = MOBA_BLOCK // page
    n_rows = n_heads * n_new
    head_of_col = lax.broadcasted_iota(jnp.int32, (n_heads, d), 1) // HEAD_DIM
    head_of_row = lax.broadcasted_iota(jnp.int32, (n_heads, d), 0)
    own_head = head_of_col == head_of_row

    def key_major(rows_val):
        padded = jnp.concatenate([rows_val, jnp.zeros((LANES - rows_val.shape[0], d), F32)], axis=0)
        return padded.T

    @pl.when(t == 0)
    def _():
        q = q_ref[...]
        qbd = jnp.concatenate([jnp.where(own_head, jnp.broadcast_to(q[n:n + 1, :], (n_heads, d)), 0.0)
                               for n in range(n_new)], axis=0)
        qbd32_ref[0:n_rows, :] = qbd
        qbd32_ref[n_rows:, :] = jnp.zeros((LANES - n_rows, d), F32)
        qbd_ref[...] = (qbd * HEAD_DIM ** -0.5).astype(BF16)
        knt_ref[...] = key_major(kn_ref[...])
        vnt_ref[...] = key_major(vn_ref[...])
        acc_ref[...] = jnp.zeros_like(acc_ref)
        ksum_ref[...] = jnp.zeros_like(ksum_ref)

    @pl.when(t < tk)
    def _():
        qbd = qbd_ref[...]
        lane_i = lax.broadcasted_iota(jnp.int32, (1, LANES), 1)
        for bi in range(pps // ppb):
            ksum = jnp.zeros((d, page), F32)
            for pi in range(ppb):
                i = bi * ppb + pi
                kp = k_refs[i][0].reshape(d, page)
                s_ref[t * pps + i] = jnp.dot(qbd, kp.astype(BF16), preferred_element_type=F32)
                ksum = ksum + kp
            onehot = jnp.where(lane_i == t * (pps // ppb) + bi, 1.0, 0.0).astype(F32)
            ksum_ref[...] += jnp.sum(ksum, axis=1, keepdims=True) * onehot

    @pl.when(t == tk - 1)
    def _():
        s_ref[n_pages] = jnp.dot(qbd_ref[...], knt_ref[...].astype(BF16), preferred_element_type=F32)
        gate = jnp.dot(qbd32_ref[...], ksum_ref[...], precision=lax.Precision.HIGHEST,
                       preferred_element_type=F32) * (1.0 / MOBA_BLOCK)
        gate_t = gate.T[:n_blocks]
        blk = lax.broadcasted_iota(jnp.int32, (n_blocks, LANES), 0)
        sel_t = _topk_select(gate_t, blk >= 0, blk, n_blocks)
        sel = jnp.concatenate([jnp.where(sel_t, 1.0, 0.0).astype(F32),
                               jnp.zeros((LANES - n_blocks, LANES), F32)], axis=0).T[:n_rows]
        q_idx = rowinfo_ref[0]
        slope = rowinfo_ref[1]
        lane_f = lax.broadcasted_iota(jnp.int32, (n_rows, LANES), 1).astype(F32)

        for b in range(n_blocks):
            bias_ref[b] = jnp.broadcast_to(jnp.where(sel[:, b:b + 1] > 0.5, 0.0, -jnp.inf), (n_rows, LANES))

        def logits(p):
            dist = (past + q_idx) - (jnp.asarray(p * page, F32) + lane_f)
            return s_ref[p] - slope * dist + bias_ref[p // ppb]

        dist_new = q_idx - lane_f
        s_new = jnp.where((dist_new >= 0) & (lane_f < n_new), s_ref[n_pages] - slope * dist_new, -jnp.inf)
        m = jnp.max(lax.fori_loop(0, n_pages, lambda p, mv: jnp.maximum(mv, logits(p)), s_new),
                    axis=1, keepdims=True)

        def exp_step(p, den):
            e = jnp.exp(logits(p) - m)
            s_ref[p] = e
            return den + e

        e_new = jnp.exp(s_new - m)
        inv = 1.0 / jnp.sum(lax.fori_loop(0, n_pages, exp_step, e_new), axis=1, keepdims=True)
        s_ref[n_pages] = e_new * inv

        def norm_step(p, carry):
            s_ref[p] = s_ref[p] * inv
            return carry

        lax.fori_loop(0, n_pages, norm_step, 0)

    def weighted(p_idx, v_t):
        return lax.dot_general(s_ref[p_idx].astype(BF16), v_t.astype(BF16), _NT, preferred_element_type=F32)

    @pl.when(t >= tk)
    def _():
        acc = acc_ref[...]
        for i in range(pps):
            acc = acc + weighted((t - tk) * pps + i, v_refs[i][0].reshape(d, page))
        acc_ref[...] = acc

    @pl.when(t == 2 * tk - 1)
    def _():
        acc = acc_ref[...] + weighted(n_pages, vnt_ref[...])
        outs = [jnp.sum(jnp.where(own_head, acc[n * n_heads:(n + 1) * n_heads, :], 0.0), axis=0, keepdims=True)
                for n in range(n_new)]
        outs.append(jnp.zeros((SAMPLE_PAD - n_new, d), F32))
        o_ref[...] = jnp.concatenate(outs, axis=0).astype(BF16)


def sample_attn(page_table, rowinfo, q, k_new, v_new, cache_kt, cache_vt, n_new, pps=8):
    n_dec, n_pages = page_table.shape
    _, n_heads, _, page = cache_kt.shape
    d = n_heads * HEAD_DIM
    ppb = MOBA_BLOCK // page
    assert n_pages % pps == 0 and pps % ppb == 0 and page == LANES
    assert n_new <= SAMPLE_PAD and n_heads % SUBLANES == 0
    tk = n_pages // pps
    n_blocks = n_pages // ppb
    n_rows = n_heads * n_new
    assert n_blocks <= LANES and n_rows <= LANES

    def k_spec(i):
        return pl.BlockSpec((1, n_heads, HEAD_DIM, page),
                            lambda b, t, pt: (pt[b, jnp.minimum(t, tk - 1) * pps + i], 0, 0, 0))

    def v_spec(i):
        return pl.BlockSpec((1, n_heads, HEAD_DIM, page),
                            lambda b, t, pt: (pt[b, jnp.maximum(t - tk, 0) * pps + i], 0, 0, 0))

    row_spec = pl.BlockSpec((SAMPLE_PAD, d), lambda b, t, pt: (b, 0))
    grid_spec = pltpu.PrefetchScalarGridSpec(
        num_scalar_prefetch=1,
        grid=(n_dec, 2 * tk),
        in_specs=[pl.BlockSpec((2, n_rows, LANES), lambda b, t, pt: (0, 0, 0)), row_spec, row_spec, row_spec]
                 + [k_spec(i) for i in range(pps)] + [v_spec(i) for i in range(pps)],
        out_specs=row_spec,
        scratch_shapes=[pltpu.VMEM((n_rows, d), BF16),
                        pltpu.VMEM((LANES, d), F32),
                        pltpu.VMEM((n_pages + 1, n_rows, LANES), F32),
                        pltpu.VMEM((d, LANES), F32),
                        pltpu.VMEM((n_rows, d), F32),
                        pltpu.VMEM((d, LANES), F32),
                        pltpu.VMEM((d, LANES), F32),
                        pltpu.VMEM((n_blocks, n_rows, LANES), F32)],
    )
    return pl.pallas_call(
        functools.partial(_sample_attn_body, n_pages=n_pages, pps=pps, n_new=n_new, n_heads=n_heads),
        grid_spec=grid_spec,
        out_shape=jax.ShapeDtypeStruct((n_dec * SAMPLE_PAD, d), BF16),
        compiler_params=_params(("parallel", "arbitrary")),
        name="sample_attn",
    )(page_table, rowinfo, q, k_new, v_new, *([cache_kt] * pps), *([cache_vt] * pps))


def _ssd_body(z_ref, xs_ref, bc_ref, dt_ref, cinit_ref, sinit_ref, cw_ref, cb_ref, dtb_ref, alog_ref,
              dsk_ref, gn_ref, y_ref, cout_ref, sout_ref, ext_ref, xc_ref, h_ref, *, rows, valid, n_chunks):
    cl = SSD_CHUNK
    c = pl.program_id(1)
    d_inner = xs_ref.shape[1]
    n_heads = d_inner // SSM_HEAD_DIM
    hpg = n_heads // N_GROUPS
    gw = hpg * SSM_HEAD_DIM
    tail = SUBLANES

    @pl.when(c == 0)
    def _():
        ext_ref[0:tail, :] = cinit_ref[0]
        h_ref[...] = sinit_ref[0].reshape(h_ref.shape)
        if rows < cl:
            ext_ref[tail + rows:, :] = jnp.zeros((cl - rows, ext_ref.shape[1]), F32)

    @pl.when(c > 0)
    def _():
        ext_ref[0:tail, :] = ext_ref[cl:cl + tail, :]

    ext_ref[tail:tail + rows, 0:d_inner] = xs_ref[...]
    ext_ref[tail:tail + rows, d_inner:] = bc_ref[...]

    cw = cw_ref[...]
    conv = cb_ref[...] + ext_ref[tail:tail + cl, :] * cw[CONV_W - 1:CONV_W, :]
    for back in range(1, CONV_W):
        conv = conv + ext_ref[tail - back:tail - back + cl, :] * cw[CONV_W - 1 - back:CONV_W - back, :]
    xc_ref[...] = _silu(conv)

    def pad(v):
        if rows == cl:
            return v
        return jnp.concatenate([v, jnp.zeros((cl - rows, v.shape[1]), v.dtype)], axis=0)

    trow = lax.broadcasted_iota(jnp.int32, (cl, LANES), 0)
    dt_raw = pad(dt_ref[...]) + dtb_ref[...]
    dt = jnp.maximum(dt_raw, 0.0) + jnp.log1p(jnp.exp(-jnp.abs(dt_raw)))
    dt = jnp.where(trow < valid, dt, 0.0)
    a = -jnp.exp(alog_ref[...])
    tri_r = lax.broadcasted_iota(jnp.int32, (cl, cl), 0)
    tri_c = lax.broadcasted_iota(jnp.int32, (cl, cl), 1)
    causal = tri_c <= tri_r
    a_cs = jnp.dot(jnp.where(causal, 1.0, 0.0).astype(F32), dt * a, precision=lax.Precision.HIGHEST,
                   preferred_element_type=F32)
    a_cs_t = a_cs.T
    dt_t = dt.T
    z = pad(z_ref[...])

    for g in range(N_GROUPS):
        bg = xc_ref[:, d_inner + g * D_STATE:d_inner + (g + 1) * D_STATE].astype(BF16)
        cg = xc_ref[:, d_inner + (N_GROUPS + g) * D_STATE:d_inner + (N_GROUPS + g + 1) * D_STATE].astype(BF16)
        cb = lax.dot_general(cg, bg, _NT, preferred_element_type=F32)
        xg = xc_ref[:, g * gw:(g + 1) * gw]
        xg_t = xg.T
        h_old = h_ref[g * gw:(g + 1) * gw, :]
        y_off = lax.dot_general(cg, h_old.astype(BF16), _NT, preferred_element_type=F32)
        ys, xw_rows, h_scaled = [], [], []
        for e in range(hpg):
            h = g * hpg + e
            acs_col = a_cs[:, h:h + 1]
            acs_row = a_cs_t[h:h + 1, :]
            dt_row = dt_t[h:h + 1, :]
            a_end = a_cs_t[h:h + 1, cl - 1:cl]
            decay = jnp.exp(jnp.where(causal, acs_col - acs_row, -jnp.inf))
            w = (cb * decay * dt_row).astype(BF16)
            xh = xg[:, e * SSM_HEAD_DIM:(e + 1) * SSM_HEAD_DIM]
            y_diag = jnp.dot(w, xh.astype(BF16), preferred_element_type=F32)
            ys.append(y_diag + y_off[:, e * SSM_HEAD_DIM:(e + 1) * SSM_HEAD_DIM] * jnp.exp(acs_col))
            to_end = jnp.exp(a_end - acs_row) * dt_row
            xw_rows.append(xg_t[e * SSM_HEAD_DIM:(e + 1) * SSM_HEAD_DIM, :] * to_end)
            h_scaled.append(h_old[e * SSM_HEAD_DIM:(e + 1) * SSM_HEAD_DIM, :] * jnp.exp(a_end))
        states = jnp.dot(jnp.concatenate(xw_rows, axis=0).astype(BF16), bg, preferred_element_type=F32)
        h_ref[g * gw:(g + 1) * gw, :] = jnp.concatenate(h_scaled, axis=0) + states
        y = jnp.concatenate(ys, axis=1) + dsk_ref[:, g * gw:(g + 1) * gw] * xg
        yz = y * _silu(z[:, g * gw:(g + 1) * gw])
        ms = jnp.mean(yz * yz, axis=-1, keepdims=True)
        yn = yz * lax.rsqrt(ms + EPS) * gn_ref[:, g * gw:(g + 1) * gw]
        y_ref[:, g * gw:(g + 1) * gw] = yn[:rows].astype(BF16)

    @pl.when(c == n_chunks - 1)
    def _():
        sout_ref[0] = h_ref[...].reshape(sout_ref.shape[1:])
        first = tail + ((valid - (CONV_W - 1)) // SUBLANES) * SUBLANES
        cout_ref[0] = ext_ref[first:first + SUBLANES, :]


def ssd_mixer(proj, conv_init, ssm_init, cw, cb, dtb, alog, dsk, gnorm, n_seq, rows, valid, d_inner):
    total_rows = proj.shape[0] // n_seq
    n_chunks = total_rows // rows
    assert rows == SSD_CHUNK or n_chunks == 1
    assert CONV_W - 1 <= valid <= rows
    n_heads = d_inner // SSM_HEAD_DIM
    conv_dim = d_inner + 2 * N_GROUPS * D_STATE
    assert conv_dim == 2 * d_inner

    def col(width, idx):
        return pl.BlockSpec((rows, width), lambda b, c: (b * n_chunks + c, idx))

    def const(shape):
        return pl.BlockSpec(shape, lambda b, c: (0,) * len(shape))

    outs = pl.pallas_call(
        functools.partial(_ssd_body, rows=rows, valid=valid, n_chunks=n_chunks),
        grid=(n_seq, n_chunks),
        in_specs=[col(d_inner, 0), col(d_inner, 1), col(d_inner, 2), col(LANES, 3 * d_inner // LANES),
                  pl.BlockSpec((1, SUBLANES, conv_dim), lambda b, c: (b, 0, 0)),
                  pl.BlockSpec((1, n_heads, SSM_HEAD_DIM, D_STATE), lambda b, c: (b, 0, 0, 0)),
                  const((CONV_W, conv_dim)), const((1, conv_dim)), const((1, LANES)), const((1, LANES)),
                  const((1, d_inner)), const((1, d_inner))],
        out_specs=[pl.BlockSpec((rows, d_inner), lambda b, c: (b * n_chunks + c, 0)),
                   pl.BlockSpec((1, SUBLANES, conv_dim), lambda b, c: (b, 0, 0)),
                   pl.BlockSpec((1, n_heads, SSM_HEAD_DIM, D_STATE), lambda b, c: (b, 0, 0, 0))],
        out_shape=[jax.ShapeDtypeStruct((proj.shape[0], d_inner), BF16),
                   jax.ShapeDtypeStruct((n_seq, SUBLANES, conv_dim), F32),
                   jax.ShapeDtypeStruct((n_seq, n_heads, SSM_HEAD_DIM, D_STATE), F32)],
        scratch_shapes=[pltpu.VMEM((SUBLANES + SSD_CHUNK, conv_dim), F32),
                        pltpu.VMEM((SSD_CHUNK, conv_dim), F32),
                        pltpu.VMEM((n_heads * SSM_HEAD_DIM, D_STATE), F32)],
        compiler_params=_params(("parallel", "arbitrary")),
        name="ssd_mixer",
    )(proj, proj, proj, proj, conv_init, ssm_init, cw, cb, dtb, alog, dsk, gnorm)
    return outs


def _pad_cols(w, n):
    return jnp.pad(w, ((0, 0), (0, n - w.shape[1])))


def _pad_lanes(v):
    return jnp.pad(v, (0, LANES - v.shape[0])).reshape(1, LANES)


def kernel(x_prompt, x_sample, cache_k, cache_v, state_conv, state_ssm, page_table, norm_mix, norm_ffn, w_qkv, w_o, w_in_ssm, conv_w, conv_b, dt_bias, a_log, d_skip, norm_ssm, w_out_ssm, w_gate_up, w_down, norm_final):
    n_seq, seq, d_model = x_prompt.shape
    n_dec, n_new, _ = x_sample.shape
    n_heads = d_model // HEAD_DIM
    d_inner = norm_ssm.shape[1]
    ssm_heads = d_inner // SSM_HEAD_DIM
    conv_dim = conv_w.shape[2]
    tm_p = 512
    tm_s = n_dec * SAMPLE_PAD

    hp = x_prompt.reshape(n_seq * seq, d_model)
    hs = jnp.pad(x_sample, ((0, 0), (0, SAMPLE_PAD - n_new), (0, 0))).reshape(tm_s, d_model)

    wqkv = w_qkv[0]
    w_hi = wqkv.astype(BF16)
    wq_lo = (wqkv[:, :d_model] - _bf16_round(wqkv[:, :d_model])).astype(BF16)
    q_p, k_p, v_p, xm_p = qkv_proj(hp, norm_mix[0], w_hi, wq_lo, tm_p)
    q_s, k_s, v_s, _ = qkv_proj(hs, norm_mix[0], w_hi, wq_lo, tm_s)
    kmean = mm_f32(xm_p.reshape(n_seq * seq // MOBA_BLOCK, d_model), wqkv[:, d_model:2 * d_model])
    slopes = jnp.exp2(-8.0 * (jnp.arange(n_heads, dtype=F32) + 1.0) / n_heads)
    qa, ka, va = moba_prep(q_p, k_p, v_p, kmean, slopes, n_seq, seq, n_heads)
    attn_p = moba_attn(qa, ka, va)
    rows = jnp.arange(n_heads * n_new)
    rowinfo = jnp.stack([(rows // n_heads).astype(F32), slopes[rows % n_heads]])
    rowinfo = jnp.broadcast_to(rowinfo[:, :, None], (2, n_heads * n_new, LANES))
    cache_kt = jnp.transpose(cache_k[0], (0, 2, 3, 1))
    cache_vt = jnp.transpose(cache_v[0], (0, 2, 3, 1))
    attn_s = sample_attn(page_table, rowinfo, q_s, k_s, v_s, cache_kt, cache_vt, n_new)
    wo = w_o[0].astype(BF16)
    hp = mm_res(attn_p, wo, hp, tm_p)
    hs = mm_res(attn_s, wo, hs, tm_s)

    k_prompt = k_p.reshape(1, n_seq, seq, n_heads, HEAD_DIM)
    v_prompt = v_p.reshape(1, n_seq, seq, n_heads, HEAD_DIM)
    k_sample = k_s.reshape(n_dec, SAMPLE_PAD, n_heads, HEAD_DIM)[None, :, :n_new]
    v_sample = v_s.reshape(n_dec, SAMPLE_PAD, n_heads, HEAD_DIM)[None, :, :n_new]

    def ffn(h, layer, tm, final_g=None):
        act = swiglu_gu(h, norm_ffn[layer], w_gate_up[layer].astype(BF16), tm, 256)
        return mm_res(act, w_down[layer].astype(BF16), h, tm, final_g)

    hp = ffn(hp, 0, tm_p)
    hs = ffn(hs, 0, tm_s)

    in_cols = d_inner + conv_dim + ssm_heads
    assert w_in_ssm.shape[2] == in_cols
    proj_w = ((in_cols + 1279) // 1280) * 1280
    w_in = _pad_cols(w_in_ssm[0], proj_w).astype(BF16)
    proj_p = norm_mm(hp, norm_mix[1], w_in, tm_p, 1280)
    proj_s = norm_mm(hs, norm_mix[1], w_in, tm_s, 1280)
    cb = conv_b[0].reshape(1, conv_dim)
    dtb = _pad_lanes(dt_bias[0])
    alog = _pad_lanes(a_log[0])
    dsk = jnp.repeat(d_skip[0], SSM_HEAD_DIM).reshape(1, d_inner)
    gn = norm_ssm[0].reshape(1, d_inner)
    zero_conv = jnp.zeros((n_seq, SUBLANES, conv_dim), F32)
    zero_ssm = jnp.zeros((n_seq, ssm_heads, SSM_HEAD_DIM, D_STATE), F32)
    y_p, conv_p, ssm_p = ssd_mixer(proj_p, zero_conv, zero_ssm, conv_w[0], cb, dtb, alog, dsk, gn,
                                   n_seq, SSD_CHUNK, SSD_CHUNK, d_inner)
    conv_init_s = jnp.pad(state_conv[0], ((0, 0), (SUBLANES - (CONV_W - 1), 0), (0, 0)))
    y_s, conv_s, ssm_s = ssd_mixer(proj_s, conv_init_s, state_ssm[0], conv_w[0], cb, dtb, alog, dsk, gn,
                                   n_dec, SAMPLE_PAD, n_new, d_inner)
    w_out = w_out_ssm[0].astype(BF16)
    hp = mm_res(y_p, w_out, hp, tm_p)
    hs = mm_res(y_s, w_out, hs, tm_s)
    y_prompt = ffn(hp, 1, tm_p, norm_final).reshape(n_seq, seq, d_model)
    y_sample = ffn(hs, 1, tm_s, norm_final).reshape(n_dec, SAMPLE_PAD, d_model)[:, :n_new]

    lo_p = SUBLANES - (CONV_W - 1)
    lo_s = (n_new - (CONV_W - 1)) % SUBLANES
    return (y_prompt, y_sample, k_prompt, v_prompt, k_sample, v_sample,
            conv_p[None, :, lo_p:lo_p + CONV_W - 1], ssm_p[None],
            conv_s[None, :, lo_s:lo_s + CONV_W - 1], ssm_s[None])
```

```python
import functools
import math

import jax
import jax.numpy as jnp
from jax import lax
from jax.experimental import pallas as pl
from jax.experimental.pallas import tpu as pltpu

F32 = jnp.float32
BF16 = jnp.bfloat16

EPS = 1e-5
HEAD_DIM = 64
MOBA_BLOCK = 256
MOBA_TOPK = 3
SSM_HEAD_DIM = 64
N_GROUPS = 8
D_STATE = 128
CONV_W = 4
SSD_CHUNK = 128
SAMPLE_PAD = 16
AUG = 128
NEG_BIAS = -1e30
LOG2E = math.log2(math.e)
ATTN_GROUP = 4
ATTN_KV_BLOCKS = 2
PREP_BLOCKS = 2
LANES = 128
SUBLANES = 8
VMEM_LIMIT = 48 * 1024 * 1024
VMEM_LIMIT_BIG = 56 * 1024 * 1024

_NT = (((1,), (1,)), ((), ()))


def _silu(x):
    return x / (1.0 + jnp.exp(-x))


def _params(sem, vmem=VMEM_LIMIT):
    return pltpu.CompilerParams(dimension_semantics=sem, vmem_limit_bytes=vmem)


def _rms_bf16(x_ref, g_ref):
    x = x_ref[...]
    ms = jnp.mean(x * x, axis=-1, keepdims=True)
    return (x * lax.rsqrt(ms + EPS) * g_ref[...]).astype(BF16)


def _norm_mm_body(x_ref, g_ref, w_ref, o_ref, *, tn):
    xn = _rms_bf16(x_ref, g_ref)
    for c in range(w_ref.shape[1] // tn):
        o_ref[:, c * tn:(c + 1) * tn] = jnp.dot(xn, w_ref[:, c * tn:(c + 1) * tn], preferred_element_type=F32)


def norm_mm(x, g, w, tm, tn):
    m, d = x.shape
    n = w.shape[1]
    assert n % tn == 0
    return pl.pallas_call(
        functools.partial(_norm_mm_body, tn=tn),
        grid=(m // tm,),
        in_specs=[pl.BlockSpec((tm, d), lambda i: (i, 0)),
                  pl.BlockSpec((1, d), lambda i: (0, 0)),
                  pl.BlockSpec((d, n), lambda i: (0, 0), pipeline_mode=pl.Buffered(1))],
        out_specs=pl.BlockSpec((tm, n), lambda i: (i, 0)),
        out_shape=jax.ShapeDtypeStruct((m, n), F32),
        compiler_params=_params(("parallel",), VMEM_LIMIT_BIG),
        name="norm_mm",
    )(x, g.reshape(1, d), w)


def _swiglu_gu_body(x_ref, g_ref, w_ref, o_ref, *, tn):
    xn = _rms_bf16(x_ref, g_ref)
    dff = o_ref.shape[1]
    for c in range(dff // tn):
        gate = jnp.dot(xn, w_ref[:, c * tn:(c + 1) * tn], preferred_element_type=F32)
        up = jnp.dot(xn, w_ref[:, dff + c * tn:dff + (c + 1) * tn], preferred_element_type=F32)
        o_ref[:, c * tn:(c + 1) * tn] = (_silu(gate) * up).astype(BF16)


def swiglu_gu(x, g, w_gu, tm, tn):
    m, d = x.shape
    dff = w_gu.shape[1] // 2
    assert dff % tn == 0
    return pl.pallas_call(
        functools.partial(_swiglu_gu_body, tn=tn),
        grid=(m // tm,),
        in_specs=[pl.BlockSpec((tm, d), lambda i: (i, 0)),
                  pl.BlockSpec((1, d), lambda i: (0, 0)),
                  pl.BlockSpec((d, 2 * dff), lambda i: (0, 0), pipeline_mode=pl.Buffered(1))],
        out_specs=pl.BlockSpec((tm, dff), lambda i: (i, 0)),
        out_shape=jax.ShapeDtypeStruct((m, dff), BF16),
        compiler_params=_params(("parallel",)),
        name="swiglu_gu",
    )(x, g.reshape(1, d), w_gu)


def _mm_res_body(a_ref, w_ref, r_ref, o_ref):
    o_ref[...] = r_ref[...] + jnp.dot(a_ref[...], w_ref[...], preferred_element_type=F32)


def _mm_res_norm_body(a_ref, w_ref, r_ref, g_ref, o_ref):
    y = r_ref[...] + jnp.dot(a_ref[...], w_ref[...], preferred_element_type=F32)
    ms = jnp.mean(y * y, axis=-1, keepdims=True)
    o_ref[...] = y * lax.rsqrt(ms + EPS) * g_ref[...]


def mm_res(a, w, res, tm, final_g=None):
    m, k = a.shape
    n = w.shape[1]
    in_specs = [pl.BlockSpec((tm, k), lambda i: (i, 0)),
                pl.BlockSpec((k, n), lambda i: (0, 0)),
                pl.BlockSpec((tm, n), lambda i: (i, 0))]
    args = [a, w, res]
    body = _mm_res_body
    if final_g is not None:
        in_specs.append(pl.BlockSpec((1, n), lambda i: (0, 0)))
        args.append(final_g.reshape(1, n))
        body = _mm_res_norm_body
    return pl.pallas_call(
        body,
        grid=(m // tm,),
        in_specs=in_specs,
        out_specs=pl.BlockSpec((tm, n), lambda i: (i, 0)),
        out_shape=jax.ShapeDtypeStruct((m, n), F32),
        compiler_params=_params(("parallel",)),
        name="mm_res_norm" if final_g is not None else "mm_res",
    )(*args)


def _qkv_body(x_ref, g_ref, w_ref, wq_lo_ref, q_ref, k_ref, v_ref, xm_ref):
    d = x_ref.shape[1]
    x = x_ref[...]
    ms = jnp.mean(x * x, axis=-1, keepdims=True)
    xn = x * lax.rsqrt(ms + EPS) * g_ref[...]
    hi = xn.astype(BF16)
    lo = (xn - hi.astype(F32)).astype(BF16)
    wq = w_ref[:, 0:d]
    q_ref[...] = (jnp.dot(hi, wq, preferred_element_type=F32) + jnp.dot(lo, wq, preferred_element_type=F32)
                  + jnp.dot(hi, wq_lo_ref[...], preferred_element_type=F32))
    k_ref[...] = jnp.dot(hi, w_ref[:, d:2 * d], preferred_element_type=F32)
    v_ref[...] = jnp.dot(hi, w_ref[:, 2 * d:3 * d], preferred_element_type=F32)
    for blk in range(xm_ref.shape[0]):
        xm_ref[blk] = jnp.mean(xn[blk * MOBA_BLOCK:(blk + 1) * MOBA_BLOCK], axis=0, keepdims=True)


def qkv_proj(x, g, w, wq_lo, tm):
    m, d = x.shape
    assert tm % MOBA_BLOCK == 0
    row = pl.BlockSpec((tm, d), lambda i: (i, 0))
    out = jax.ShapeDtypeStruct((m, d), F32)
    return pl.pallas_call(
        _qkv_body,
        grid=(m // tm,),
        in_specs=[row, pl.BlockSpec((1, d), lambda i: (0, 0)),
                  pl.BlockSpec((d, 3 * d), lambda i: (0, 0), pipeline_mode=pl.Buffered(1)),
                  pl.BlockSpec((d, d), lambda i: (0, 0), pipeline_mode=pl.Buffered(1))],
        out_specs=[row, row, row, pl.BlockSpec((tm // MOBA_BLOCK, 1, d), lambda i: (i, 0, 0))],
        out_shape=[out, out, out, jax.ShapeDtypeStruct((m // MOBA_BLOCK, 1, d), F32)],
        compiler_params=_params(("parallel",)),
        name="qkv_proj",
    )(x, g.reshape(1, d), w, wq_lo)


def _mm_f32_body(a_ref, w_ref, o_ref):
    o_ref[...] = jnp.dot(a_ref[...], w_ref[...], precision=lax.Precision.HIGHEST, preferred_element_type=F32)


def mm_f32(a, w):
    m, k = a.shape
    n = w.shape[1]
    return pl.pallas_call(
        _mm_f32_body,
        grid=(1,),
        in_specs=[pl.BlockSpec((m, k), lambda i: (0, 0)), pl.BlockSpec((k, n), lambda i: (0, 0))],
        out_specs=pl.BlockSpec((m, n), lambda i: (0, 0)),
        out_shape=jax.ShapeDtypeStruct((m, n), F32),
        compiler_params=_params(("arbitrary",)),
        name="mm_f32",
    )(a, w)


def _topk_select(gate, valid, blk, n_blocks):
    g = jnp.where(valid, gate, -jnp.inf)
    rank = jnp.zeros(g.shape, jnp.int32)
    for other in range(n_blocks):
        row = g[other:other + 1, :]
        beats = jnp.where(row > g, 1, jnp.where(row == g, jnp.where(other < blk, 1, 0), 0))
        rank = rank + beats
    return jnp.logical_and(valid, rank < MOBA_TOPK)


def _bf16_trunc(x):
    bits = lax.bitcast_convert_type(x, jnp.uint32) & jnp.uint32(0xFFFF0000)
    return lax.bitcast_convert_type(bits, F32)


def _bf16_round(x):
    bits = lax.bitcast_convert_type(x, jnp.uint32)
    bits = (bits + jnp.uint32(0x7FFF) + ((bits >> 16) & jnp.uint32(1))) & jnp.uint32(0xFFFF0000)
    return lax.bitcast_convert_type(bits, F32)


def _split3(x):
    hi = _bf16_trunc(x)
    mid = _bf16_trunc(x - hi)
    return hi, mid, x - hi - mid


def _moba_prep_body(q_ref, k_ref, v_ref, kmean_ref, kx_ref, qa_ref, ka_ref, va_ref, *, n_blocks):
    half = AUG // 2
    kmean = kmean_ref[...]
    blk = lax.broadcasted_iota(jnp.int32, (n_blocks, MOBA_BLOCK), 0)
    lane = lax.broadcasted_iota(jnp.int32, (MOBA_BLOCK, AUG), 1)
    arow = lax.broadcasted_iota(jnp.int32, (SUBLANES, MOBA_BLOCK), 0)
    ones_rows = jnp.where(arow < 3, 1.0, 0.0).astype(F32)
    gap = jnp.zeros((half - n_blocks - SUBLANES, MOBA_BLOCK), F32)
    other = jnp.zeros((half, MOBA_BLOCK), F32)
    for j in range(q_ref.shape[0] // MOBA_BLOCK):
        sb = pl.program_id(1) * (q_ref.shape[0] // MOBA_BLOCK) + j
        rows = slice(j * MOBA_BLOCK, (j + 1) * MOBA_BLOCK)
        valid = blk < sb
        q = q_ref[rows, :] * (LOG2E * HEAD_DIM ** -0.5)
        k = k_ref[rows, :]
        v = v_ref[rows, :]
        for hh in range(2):
            first = hh == 0
            data = (lane < half) if first else (lane >= half)
            gate = lax.dot_general(kmean[:, hh * half:(hh + 1) * half], q_ref[rows, hh * half:(hh + 1) * half], _NT,
                                   precision=lax.Precision.HIGHEST, preferred_element_type=F32)
            sel = _topk_select(gate, valid, blk, n_blocks)
            bias_t = jnp.where(sel, 0.0, jnp.where(blk == sb, 0.0, NEG_BIAS)).astype(F32)
            extras_t = [bias_t, ones_rows, gap]
            aug = jnp.concatenate([other] + extras_t if first else extras_t + [other], axis=0).T
            qa_ref[0, hh, rows, :] = jnp.where(data, q, aug).astype(BF16)
            ka_ref[0, hh, rows, :] = jnp.where(data, k, kx_ref[hh, rows, :]).astype(BF16)
            ones_lane = half if first else 0
            va_ref[0, hh, rows, :] = jnp.where(data, v, jnp.where(lane == ones_lane, 1.0, 0.0)).astype(BF16)


def _key_extras(slopes, seq, n_blocks):
    shape = (slopes.shape[0], seq, AUG)
    pos = jnp.arange(seq, dtype=F32)
    hi, mid, lo = _split3(pos[None, :] * (slopes * LOG2E)[:, None])
    head = lax.broadcasted_iota(jnp.int32, shape, 0)
    key = lax.broadcasted_iota(jnp.int32, shape, 1)
    lane = lax.broadcasted_iota(jnp.int32, shape, 2)
    e = lane - jnp.where(head % 2 == 0, AUG // 2, 0)
    kx = jnp.where(e == key // MOBA_BLOCK, 1.0, 0.0)
    for i, part in enumerate((hi, mid, lo)):
        kx = jnp.where(e == n_blocks + i, part[:, :, None], kx)
    return kx


def moba_prep(q, k, v, kmean, slopes, n_seq, seq, n_heads):
    n_blocks = seq // MOBA_BLOCK
    assert n_blocks + SUBLANES <= AUG // 2 and n_blocks % SUBLANES == 0
    hpairs = n_heads // 2
    kx = _key_extras(slopes, seq, n_blocks)
    rows = PREP_BLOCKS * MOBA_BLOCK
    n_steps = seq // rows
    assert seq % rows == 0
    aug_shape = jax.ShapeDtypeStruct((n_seq, n_heads, seq, AUG), BF16)
    aug_spec = pl.BlockSpec((1, 2, rows, AUG), lambda hp, sb, b: (b, hp, sb, 0))
    col = pl.BlockSpec((rows, LANES), lambda hp, sb, b: (b * n_steps + sb, hp))
    return pl.pallas_call(
        functools.partial(_moba_prep_body, n_blocks=n_blocks),
        grid=(hpairs, n_steps, n_seq),
        in_specs=[col, col, col,
                  pl.BlockSpec((n_blocks, LANES), lambda hp, sb, b: (b, hp)),
                  pl.BlockSpec((2, rows, AUG), lambda hp, sb, b: (hp, sb, 0))],
        out_specs=[aug_spec, aug_spec, aug_spec],
        out_shape=[aug_shape, aug_shape, aug_shape],
        compiler_params=_params(("parallel", "parallel", "parallel")),
        name="moba_prep",
    )(q, k, v, kmean, kx)


def _moba_attn_body(qa_ref, ka_ref, va_ref, o_ref, *, n_blocks):
    t = MOBA_BLOCK
    grp = ATTN_GROUP
    rows = grp * t
    kw = ATTN_KV_BLOCKS
    out_lane = lax.broadcasted_iota(jnp.int32, (rows, AUG), 1)

    def update(q, m, acc, hh, first_blk, n_blk, mask=None):
        width = n_blk * t
        start = pl.multiple_of(first_blk * t, t)
        s = lax.dot_general(q, ka_ref[0, hh, pl.ds(start, width), :], _NT, preferred_element_type=F32)
        if mask is not None:
            s = jnp.where(mask, s, -jnp.inf)
        m_new = jnp.maximum(m, jnp.max(s, axis=-1, keepdims=True))
        p = jnp.exp2(s - m_new)
        acc = jnp.exp2(m - m_new) * acc + jnp.dot(p.astype(BF16), va_ref[0, hh, pl.ds(start, width), :],
                                                  preferred_element_type=F32)
        return m_new, acc

    def group(g, carry):
        base = g * grp
        qs = [qa_ref[0, hh, pl.ds(pl.multiple_of(base * t, t), rows), :] for hh in range(2)]

        def past_tiles(kj, state):
            out = []
            for hh in range(2):
                out.extend(update(qs[hh], state[2 * hh], state[2 * hh + 1], hh, kj * kw, kw))
            return tuple(out)

        init = [jnp.full((rows, 1), -jnp.inf, F32), jnp.zeros((rows, AUG), F32)] * 2
        state = list(lax.fori_loop(0, base // kw, past_tiles, tuple(init)))

        rr = lax.broadcasted_iota(jnp.int32, (t, t), 0)
        cc = lax.broadcasted_iota(jnp.int32, (t, t), 1)
        causal = cc <= rr
        outs = []
        for hh in range(2):
            den_lane = HEAD_DIM if hh == 0 else 0
            parts = []
            for a in range(grp):
                sl = slice(a * t, (a + 1) * t)
                m, acc = state[2 * hh][sl], state[2 * hh + 1][sl]
                for b in range(a + 1):
                    m, acc = update(qs[hh][sl], m, acc, hh, base + b, 1, causal if b == a else None)
                parts.append(acc / acc[:, den_lane:den_lane + 1])
            outs.append(jnp.concatenate(parts, axis=0))
        o_ref[pl.ds(pl.multiple_of(base * t, t), rows), :] = jnp.where(
            out_lane < HEAD_DIM, outs[0], outs[1]).astype(BF16)
        return carry

    lax.fori_loop(0, n_blocks // grp, group, 0)


def moba_attn(qa, ka, va):
    n_seq, n_heads, seq, _ = qa.shape
    n_blocks = seq // MOBA_BLOCK
    assert n_blocks % ATTN_GROUP == 0 and ATTN_GROUP % ATTN_KV_BLOCKS == 0
    spec = pl.BlockSpec((1, 2, seq, AUG), lambda b, hp: (b, hp, 0, 0))
    return pl.pallas_call(
        functools.partial(_moba_attn_body, n_blocks=n_blocks),
        grid=(n_seq, n_heads // 2),
        in_specs=[spec, spec, spec],
        out_specs=pl.BlockSpec((seq, LANES), lambda b, hp: (b, hp)),
        out_shape=jax.ShapeDtypeStruct((n_seq * seq, n_heads * HEAD_DIM), BF16),
        compiler_params=_params(("parallel", "parallel")),
        name="moba_attn",
    )(qa, ka, va)


def _sample_attn_body(pt_ref, rowinfo_ref, q_ref, kn_ref, vn_ref, *rest, n_pages, pps, n_new, n_heads):
    k_refs = rest[:pps]
    v_refs = rest[pps:2 * pps]
    o_ref = rest[2 * pps]
    qbd_ref, qbd32_ref, s_ref, ksum_ref, acc_ref, knt_ref, vnt_ref, bias_ref = rest[2 * pps + 1:]
    t = pl.program_id(1)
    tk = n_pages // pps
    page = k_refs[0].shape[3]
    d = q_ref.shape[1]
    past = n_pages * page
    n_blocks = past // MOBA_BLOCK
    ppb = MOBA_BLOCK // page
    n_rows = n_heads * n_new
    head_of_col = lax.broadcasted_iota(jnp.int32, (n_heads, d), 1) // HEAD_DIM
    head_of_row = lax.broadcasted_iota(jnp.int32, (n_heads, d), 0)
    own_head = head_of_col == head_of_row

    def key_major(rows_val):
        padded = jnp.concatenate([rows_val, jnp.zeros((LANES - rows_val.shape[0], d), F32)], axis=0)
        return padded.T

    @pl.when(t == 0)
    def _():
        q = q_ref[...]
        qbd = jnp.concatenate([jnp.where(own_head, jnp.broadcast_to(q[n:n + 1, :], (n_heads, d)), 0.0)
                               for n in range(n_new)], axis=0)
        qbd32_ref[0:n_rows, :] = qbd
        qbd32_ref[n_rows:, :] = jnp.zeros((LANES - n_rows, d), F32)
        qbd_ref[...] = (qbd * HEAD_DIM ** -0.5).astype(BF16)
        knt_ref[...] = key_major(kn_ref[...])
        vnt_ref[...] = key_major(vn_ref[...])
        acc_ref[...] = jnp.zeros_like(acc_ref)
        ksum_ref[...] = jnp.zeros_like(ksum_ref)

    @pl.when(t < tk)
    def _():
        qbd = qbd_ref[...]
        lane_i = lax.broadcasted_iota(jnp.int32, (1, LANES), 1)
        for bi in range(pps // ppb):
            ksum = jnp.zeros((d, page), F32)
            for pi in range(ppb):
                i = bi * ppb + pi
                kp = k_refs[i][0].reshape(d, page)
                s_ref[t * pps + i] = jnp.dot(qbd, kp.astype(BF16), preferred_element_type=F32)
                ksum = ksum + kp
            onehot = jnp.where(lane_i == t * (pps // ppb) + bi, 1.0, 0.0).astype(F32)
            ksum_ref[...] += jnp.sum(ksum, axis=1, keepdims=True) * onehot

    @pl.when(t == tk - 1)
    def _():
        s_ref[n_pages] = jnp.dot(qbd_ref[...], knt_ref[...].astype(BF16), preferred_element_type=F32)
        gate = jnp.dot(qbd32_ref[...], ksum_ref[...], precision=lax.Precision.HIGHEST,
                       preferred_element_type=F32) * (1.0 / MOBA_BLOCK)
        gate_t = gate.T[:n_blocks]
        blk = lax.broadcasted_iota(jnp.int32, (n_blocks, LANES), 0)
        sel_t = _topk_select(gate_t, blk >= 0, blk, n_blocks)
        sel = jnp.concatenate([jnp.where(sel_t, 1.0, 0.0).astype(F32),
                               jnp.zeros((LANES - n_blocks, LANES), F32)], axis=0).T[:n_rows]
        q_idx = rowinfo_ref[0]
        slope = rowinfo_ref[1]
        lane_f = lax.broadcasted_iota(jnp.int32, (n_rows, LANES), 1).astype(F32)

        for b in range(n_blocks):
            bias_ref[b] = jnp.broadcast_to(jnp.where(sel[:, b:b + 1] > 0.5, 0.0, -jnp.inf), (n_rows, LANES))

        def logits(p):
            dist = (past + q_idx) - (jnp.asarray(p * page, F32) + lane_f)
            return s_ref[p] - slope * dist + bias_ref[p // ppb]

        dist_new = q_idx - lane_f
        s_new = jnp.where((dist_new >= 0) & (lane_f < n_new), s_ref[n_pages] - slope * dist_new, -jnp.inf)
        m = jnp.max(lax.fori_loop(0, n_pages, lambda p, mv: jnp.maximum(mv, logits(p)), s_new),
                    axis=1, keepdims=True)

        def exp_step(p, den):
            e = jnp.exp(logits(p) - m)
            s_ref[p] = e
            return den + e

        e_new = jnp.exp(s_new - m)
        inv = 1.0 / jnp.sum(lax.fori_loop(0, n_pages, exp_step, e_new), axis=1, keepdims=True)
        s_ref[n_pages] = e_new * inv

        def norm_step(p, carry):
            s_ref[p] = s_ref[p] * inv
            return carry

        lax.fori_loop(0, n_pages, norm_step, 0)

    def weighted(p_idx, v_t):
        return lax.dot_general(s_ref[p_idx].astype(BF16), v_t.astype(BF16), _NT, preferred_element_type=F32)

    @pl.when(t >= tk)
    def _():
        acc = acc_ref[...]
        for i in range(pps):
            acc = acc + weighted((t - tk) * pps + i, v_refs[i][0].reshape(d, page))
        acc_ref[...] = acc

    @pl.when(t == 2 * tk - 1)
    def _():
        acc = acc_ref[...] + weighted(n_pages, vnt_ref[...])
        outs = [jnp.sum(jnp.where(own_head, acc[n * n_heads:(n + 1) * n_heads, :], 0.0), axis=0, keepdims=True)
                for n in range(n_new)]
        outs.append(jnp.zeros((SAMPLE_PAD - n_new, d), F32))
        o_ref[...] = jnp.concatenate(outs, axis=0).astype(BF16)


def sample_attn(page_table, rowinfo, q, k_new, v_new, cache_kt, cache_vt, n_new, pps=16):
    n_dec, n_pages = page_table.shape
    _, n_heads, _, page = cache_kt.shape
    d = n_heads * HEAD_DIM
    ppb = MOBA_BLOCK // page
    assert n_pages % pps == 0 and pps % ppb == 0 and page == LANES
    assert n_new <= SAMPLE_PAD and n_heads % SUBLANES == 0
    tk = n_pages // pps
    n_blocks = n_pages // ppb
    n_rows = n_heads * n_new
    assert n_blocks <= LANES and n_rows <= LANES

    def k_spec(i):
        return pl.BlockSpec((1, n_heads, HEAD_DIM, page),
                            lambda b, t, pt: (pt[b, jnp.minimum(t, tk - 1) * pps + i], 0, 0, 0))

    def v_spec(i):
        return pl.BlockSpec((1, n_heads, HEAD_DIM, page),
                            lambda b, t, pt: (pt[b, jnp.maximum(t - tk, 0) * pps + i], 0, 0, 0))

    row_spec = pl.BlockSpec((SAMPLE_PAD, d), lambda b, t, pt: (b, 0))
    grid_spec = pltpu.PrefetchScalarGridSpec(
        num_scalar_prefetch=1,
        grid=(n_dec, 2 * tk),
        in_specs=[pl.BlockSpec((2, n_rows, LANES), lambda b, t, pt: (0, 0, 0)), row_spec, row_spec, row_spec]
                 + [k_spec(i) for i in range(pps)] + [v_spec(i) for i in range(pps)],
        out_specs=row_spec,
        scratch_shapes=[pltpu.VMEM((n_rows, d), BF16),
                        pltpu.VMEM((LANES, d), F32),
                        pltpu.VMEM((n_pages + 1, n_rows, LANES), F32),
                        pltpu.VMEM((d, LANES), F32),
                        pltpu.VMEM((n_rows, d), F32),
                        pltpu.VMEM((d, LANES), F32),
                        pltpu.VMEM((d, LANES), F32),
                        pltpu.VMEM((n_blocks, n_rows, LANES), F32)],
    )
    return pl.pallas_call(
        functools.partial(_sample_attn_body, n_pages=n_pages, pps=pps, n_new=n_new, n_heads=n_heads),
        grid_spec=grid_spec,
        out_shape=jax.ShapeDtypeStruct((n_dec * SAMPLE_PAD, d), BF16),
        compiler_params=_params(("parallel", "arbitrary")),
        name="sample_attn",
    )(page_table, rowinfo, q, k_new, v_new, *([cache_kt] * pps), *([cache_vt] * pps))


def _ssd_body(z_ref, xs_ref, bc_ref, dt_ref, cinit_ref, sinit_ref, cw_ref, cb_ref, dtb_ref, alog_ref,
              dsk_ref, gn_ref, y_ref, cout_ref, sout_ref, ext_ref, xc_ref, h_ref, *, rows, valid, n_chunks):
    cl = SSD_CHUNK
    c = pl.program_id(1)
    d_inner = xs_ref.shape[1]
    n_heads = d_inner // SSM_HEAD_DIM
    hpg = n_heads // N_GROUPS
    gw = hpg * SSM_HEAD_DIM
    tail = SUBLANES

    @pl.when(c == 0)
    def _():
        ext_ref[0:tail, :] = cinit_ref[0]
        h_ref[...] = sinit_ref[0].reshape(h_ref.shape)
        if rows < cl:
            ext_ref[tail + rows:, :] = jnp.zeros((cl - rows, ext_ref.shape[1]), F32)

    @pl.when(c > 0)
    def _():
        ext_ref[0:tail, :] = ext_ref[cl:cl + tail, :]

    ext_ref[tail:tail + rows, 0:d_inner] = xs_ref[...]
    ext_ref[tail:tail + rows, d_inner:] = bc_ref[...]

    cw = cw_ref[...]
    conv = cb_ref[...] + ext_ref[tail:tail + cl, :] * cw[CONV_W - 1:CONV_W, :]
    for back in range(1, CONV_W):
        conv = conv + ext_ref[tail - back:tail - back + cl, :] * cw[CONV_W - 1 - back:CONV_W - back, :]
    xc_ref[...] = _silu(conv)

    def pad(v):
        if rows == cl:
            return v
        return jnp.concatenate([v, jnp.zeros((cl - rows, v.shape[1]), v.dtype)], axis=0)

    trow = lax.broadcasted_iota(jnp.int32, (cl, LANES), 0)
    dt_raw = pad(dt_ref[...]) + dtb_ref[...]
    dt = jnp.maximum(dt_raw, 0.0) + jnp.log1p(jnp.exp(-jnp.abs(dt_raw)))
    dt = jnp.where(trow < valid, dt, 0.0)
    a = -jnp.exp(alog_ref[...])
    tri_r = lax.broadcasted_iota(jnp.int32, (cl, cl), 0)
    tri_c = lax.broadcasted_iota(jnp.int32, (cl, cl), 1)
    causal = tri_c <= tri_r
    a_cs = jnp.dot(jnp.where(causal, 1.0, 0.0).astype(F32), dt * a, precision=lax.Precision.HIGHEST,
                   preferred_element_type=F32)
    a_cs_t = a_cs.T
    dt_t = dt.T
    z = pad(z_ref[...])

    for g in range(N_GROUPS):
        bg = xc_ref[:, d_inner + g * D_STATE:d_inner + (g + 1) * D_STATE].astype(BF16)
        cg = xc_ref[:, d_inner + (N_GROUPS + g) * D_STATE:d_inner + (N_GROUPS + g + 1) * D_STATE].astype(BF16)
        cb = lax.dot_general(cg, bg, _NT, preferred_element_type=F32)
        xg = xc_ref[:, g * gw:(g + 1) * gw]
        xg_t = xg.T
        h_old = h_ref[g * gw:(g + 1) * gw, :]
        y_off = lax.dot_general(cg, h_old.astype(BF16), _NT, preferred_element_type=F32)
        ys, xw_rows, h_scaled = [], [], []
        for e in range(hpg):
            h = g * hpg + e
            acs_col = a_cs[:, h:h + 1]
            acs_row = a_cs_t[h:h + 1, :]
            dt_row = dt_t[h:h + 1, :]
            a_end = a_cs_t[h:h + 1, cl - 1:cl]
            decay = jnp.exp(jnp.where(causal, acs_col - acs_row, -jnp.inf))
            w = (cb * decay * dt_row).astype(BF16)
            xh = xg[:, e * SSM_HEAD_DIM:(e + 1) * SSM_HEAD_DIM]
            y_diag = jnp.dot(w, xh.astype(BF16), preferred_element_type=F32)
            ys.append(y_diag + y_off[:, e * SSM_HEAD_DIM:(e + 1) * SSM_HEAD_DIM] * jnp.exp(acs_col))
            to_end = jnp.exp(a_end - acs_row) * dt_row
            xw_rows.append(xg_t[e * SSM_HEAD_DIM:(e + 1) * SSM_HEAD_DIM, :] * to_end)
            h_scaled.append(h_old[e * SSM_HEAD_DIM:(e + 1) * SSM_HEAD_DIM, :] * jnp.exp(a_end))
        states = jnp.dot(jnp.concatenate(xw_rows, axis=0).astype(BF16), bg, preferred_element_type=F32)
        h_ref[g * gw:(g + 1) * gw, :] = jnp.concatenate(h_scaled, axis=0) + states
        y = jnp.concatenate(ys, axis=1) + dsk_ref[:, g * gw:(g + 1) * gw] * xg
        yz = y * _silu(z[:, g * gw:(g + 1) * gw])
        ms = jnp.mean(yz * yz, axis=-1, keepdims=True)
        yn = yz * lax.rsqrt(ms + EPS) * gn_ref[:, g * gw:(g + 1) * gw]
        y_ref[:, g * gw:(g + 1) * gw] = yn[:rows].astype(BF16)

    @pl.when(c == n_chunks - 1)
    def _():
        sout_ref[0] = h_ref[...].reshape(sout_ref.shape[1:])
        first = tail + ((valid - (CONV_W - 1)) // SUBLANES) * SUBLANES
        cout_ref[0] = ext_ref[first:first + SUBLANES, :]


def ssd_mixer(proj, conv_init, ssm_init, cw, cb, dtb, alog, dsk, gnorm, n_seq, rows, valid, d_inner):
    total_rows = proj.shape[0] // n_seq
    n_chunks = total_rows // rows
    assert rows == SSD_CHUNK or n_chunks == 1
    assert CONV_W - 1 <= valid <= rows
    n_heads = d_inner // SSM_HEAD_DIM
    conv_dim = d_inner + 2 * N_GROUPS * D_STATE
    assert conv_dim == 2 * d_inner

    def col(width, idx):
        return pl.BlockSpec((rows, width), lambda b, c: (b * n_chunks + c, idx))

    def const(shape):
        return pl.BlockSpec(shape, lambda b, c: (0,) * len(shape))

    outs = pl.pallas_call(
        functools.partial(_ssd_body, rows=rows, valid=valid, n_chunks=n_chunks),
        grid=(n_seq, n_chunks),
        in_specs=[col(d_inner, 0), col(d_inner, 1), col(d_inner, 2), col(LANES, 3 * d_inner // LANES),
                  pl.BlockSpec((1, SUBLANES, conv_dim), lambda b, c: (b, 0, 0)),
                  pl.BlockSpec((1, n_heads, SSM_HEAD_DIM, D_STATE), lambda b, c: (b, 0, 0, 0)),
                  const((CONV_W, conv_dim)), const((1, conv_dim)), const((1, LANES)), const((1, LANES)),
                  const((1, d_inner)), const((1, d_inner))],
        out_specs=[pl.BlockSpec((rows, d_inner), lambda b, c: (b * n_chunks + c, 0)),
                   pl.BlockSpec((1, SUBLANES, conv_dim), lambda b, c: (b, 0, 0)),
                   pl.BlockSpec((1, n_heads, SSM_HEAD_DIM, D_STATE), lambda b, c: (b, 0, 0, 0))],
        out_shape=[jax.ShapeDtypeStruct((proj.shape[0], d_inner), BF16),
                   jax.ShapeDtypeStruct((n_seq, SUBLANES, conv_dim), F32),
                   jax.ShapeDtypeStruct((n_seq, n_heads, SSM_HEAD_DIM, D_STATE), F32)],
        scratch_shapes=[pltpu.VMEM((SUBLANES + SSD_CHUNK, conv_dim), F32),
                        pltpu.VMEM((SSD_CHUNK, conv_dim), F32),
                        pltpu.VMEM((n_heads * SSM_HEAD_DIM, D_STATE), F32)],
        compiler_params=_params(("parallel", "arbitrary")),
        name="ssd_mixer",
    )(proj, proj, proj, proj, conv_init, ssm_init, cw, cb, dtb, alog, dsk, gnorm)
    return outs


def _pad_cols(w, n):
    return jnp.pad(w, ((0, 0), (0, n - w.shape[1])))


def _pad_lanes(v):
    return jnp.pad(v, (0, LANES - v.shape[0])).reshape(1, LANES)


def kernel(x_prompt, x_sample, cache_k, cache_v, state_conv, state_ssm, page_table, norm_mix, norm_ffn, w_qkv, w_o, w_in_ssm, conv_w, conv_b, dt_bias, a_log, d_skip, norm_ssm, w_out_ssm, w_gate_up, w_down, norm_final):
    n_seq, seq, d_model = x_prompt.shape
    n_dec, n_new, _ = x_sample.shape
    n_heads = d_model // HEAD_DIM
    d_inner = norm_ssm.shape[1]
    ssm_heads = d_inner // SSM_HEAD_DIM
    conv_dim = conv_w.shape[2]
    tm_p = 512
    tm_s = n_dec * SAMPLE_PAD

    hp = x_prompt.reshape(n_seq * seq, d_model)
    hs = jnp.pad(x_sample, ((0, 0), (0, SAMPLE_PAD - n_new), (0, 0))).reshape(tm_s, d_model)

    wqkv = w_qkv[0]
    wq_hi = _bf16_round(wqkv[:, :d_model])
    w_hi = jnp.concatenate([wq_hi, wqkv[:, d_model:]], axis=1).astype(BF16)
    wq_lo = (wqkv[:, :d_model] - wq_hi).astype(BF16)
    q_p, k_p, v_p, xm_p = qkv_proj(hp, norm_mix[0], w_hi, wq_lo, tm_p)
    q_s, k_s, v_s, _ = qkv_proj(hs, norm_mix[0], w_hi, wq_lo, tm_s)
    kmean = mm_f32(xm_p.reshape(n_seq * seq // MOBA_BLOCK, d_model), wqkv[:, d_model:2 * d_model])
    slopes = jnp.exp2(-8.0 * (jnp.arange(n_heads, dtype=F32) + 1.0) / n_heads)
    qa, ka, va = moba_prep(q_p, k_p, v_p, kmean, slopes, n_seq, seq, n_heads)
    attn_p = moba_attn(qa, ka, va)
    rows = jnp.arange(n_heads * n_new)
    rowinfo = jnp.stack([(rows // n_heads).astype(F32), slopes[rows % n_heads]])
    rowinfo = jnp.broadcast_to(rowinfo[:, :, None], (2, n_heads * n_new, LANES))
    cache_kt = jnp.transpose(cache_k[0], (0, 2, 3, 1))
    cache_vt = jnp.transpose(cache_v[0], (0, 2, 3, 1))
    attn_s = sample_attn(page_table, rowinfo, q_s, k_s, v_s, cache_kt, cache_vt, n_new)
    wo = w_o[0].astype(BF16)
    hp = mm_res(attn_p, wo, hp, tm_p)
    hs = mm_res(attn_s, wo, hs, tm_s)

    k_prompt = k_p.reshape(1, n_seq, seq, n_heads, HEAD_DIM)
    v_prompt = v_p.reshape(1, n_seq, seq, n_heads, HEAD_DIM)
    k_sample = k_s.reshape(n_dec, SAMPLE_PAD, n_heads, HEAD_DIM)[None, :, :n_new]
    v_sample = v_s.reshape(n_dec, SAMPLE_PAD, n_heads, HEAD_DIM)[None, :, :n_new]

    def ffn(h, layer, tm, final_g=None):
        act = swiglu_gu(h, norm_ffn[layer], w_gate_up[layer].astype(BF16), tm, 256)
        return mm_res(act, w_down[layer].astype(BF16), h, tm, final_g)

    hp = ffn(hp, 0, tm_p)
    hs = ffn(hs, 0, tm_s)

    in_cols = d_inner + conv_dim + ssm_heads
    assert w_in_ssm.shape[2] == in_cols
    proj_w = ((in_cols + 1279) // 1280) * 1280
    w_in = _pad_cols(w_in_ssm[0], proj_w).astype(BF16)
    proj_p = norm_mm(hp, norm_mix[1], w_in, tm_p, 1280)
    proj_s = norm_mm(hs, norm_mix[1], w_in, tm_s, 1280)
    cb = conv_b[0].reshape(1, conv_dim)
    dtb = _pad_lanes(dt_bias[0])
    alog = _pad_lanes(a_log[0])
    dsk = jnp.repeat(d_skip[0], SSM_HEAD_DIM).reshape(1, d_inner)
    gn = norm_ssm[0].reshape(1, d_inner)
    zero_conv = jnp.zeros((n_seq, SUBLANES, conv_dim), F32)
    zero_ssm = jnp.zeros((n_seq, ssm_heads, SSM_HEAD_DIM, D_STATE), F32)
    y_p, conv_p, ssm_p = ssd_mixer(proj_p, zero_conv, zero_ssm, conv_w[0], cb, dtb, alog, dsk, gn,
                                   n_seq, SSD_CHUNK, SSD_CHUNK, d_inner)
    conv_init_s = jnp.pad(state_conv[0], ((0, 0), (SUBLANES - (CONV_W - 1), 0), (0, 0)))
    y_s, conv_s, ssm_s = ssd_mixer(proj_s, conv_init_s, state_ssm[0], conv_w[0], cb, dtb, alog, dsk, gn,
                                   n_dec, SAMPLE_PAD, n_new, d_inner)
    w_out = w_out_ssm[0].astype(BF16)
    hp = mm_res(y_p, w_out, hp, tm_p)
    hs = mm_res(y_s, w_out, hs, tm_s)
    y_prompt = ffn(hp, 1, tm_p, norm_final).reshape(n_seq, seq, d_model)
    y_sample = ffn(hs, 1, tm_s, norm_final).reshape(n_dec, SAMPLE_PAD, d_model)[:, :n_new]

    lo_p = SUBLANES - (CONV_W - 1)
    lo_s = (n_new - (CONV_W - 1)) % SUBLANES
    return (y_prompt, y_sample, k_prompt, v_prompt, k_sample, v_sample,
            conv_p[None, :, lo_p:lo_p + CONV_W - 1], ssm_p[None],
            conv_s[None, :, lo_s:lo_s + CONV_W - 1], ssm_s[None])
```

```python
import functools
import math

import jax
import jax.numpy as jnp
from jax import lax
from jax.experimental import pallas as pl
from jax.experimental.pallas import tpu as pltpu

F32 = jnp.float32
BF16 = jnp.bfloat16

EPS = 1e-5
HEAD_DIM = 64
MOBA_BLOCK = 256
MOBA_TOPK = 3
SSM_HEAD_DIM = 64
N_GROUPS = 8
D_STATE = 128
CONV_W = 4
SSD_CHUNK = 128
SAMPLE_PAD = 16
AUG = 128
NEG_BIAS = -1e30
LOG2E = math.log2(math.e)
ATTN_GROUP = 4
ATTN_KV_BLOCKS = 2
PREP_BLOCKS = 8
LANES = 128
SUBLANES = 8
VMEM_LIMIT = 48 * 1024 * 1024
VMEM_LIMIT_BIG = 56 * 1024 * 1024

_NT = (((1,), (1,)), ((), ()))


def _silu(x):
    return x / (1.0 + jnp.exp(-x))


def _params(sem, vmem=VMEM_LIMIT):
    return pltpu.CompilerParams(dimension_semantics=sem, vmem_limit_bytes=vmem)


def _rms_bf16(x_ref, g_ref):
    x = x_ref[...]
    ms = jnp.mean(x * x, axis=-1, keepdims=True)
    return (x * lax.rsqrt(ms + EPS) * g_ref[...]).astype(BF16)


def _norm_mm_body(x_ref, g_ref, w_ref, o_ref, *, tn):
    xn = _rms_bf16(x_ref, g_ref)
    for c in range(w_ref.shape[1] // tn):
        o_ref[:, c * tn:(c + 1) * tn] = jnp.dot(xn, w_ref[:, c * tn:(c + 1) * tn], preferred_element_type=F32)


def norm_mm(x, g, w, tm, tn):
    m, d = x.shape
    n = w.shape[1]
    assert n % tn == 0
    return pl.pallas_call(
        functools.partial(_norm_mm_body, tn=tn),
        grid=(m // tm,),
        in_specs=[pl.BlockSpec((tm, d), lambda i: (i, 0)),
                  pl.BlockSpec((1, d), lambda i: (0, 0)),
                  pl.BlockSpec((d, n), lambda i: (0, 0), pipeline_mode=pl.Buffered(1))],
        out_specs=pl.BlockSpec((tm, n), lambda i: (i, 0)),
        out_shape=jax.ShapeDtypeStruct((m, n), F32),
        compiler_params=_params(("parallel",), VMEM_LIMIT_BIG),
        name="norm_mm",
    )(x, g.reshape(1, d), w)


def _swiglu_gu_body(x_ref, g_ref, w_ref, o_ref, *, tn):
    xn = _rms_bf16(x_ref, g_ref)
    dff = o_ref.shape[1]
    for c in range(dff // tn):
        gate = jnp.dot(xn, w_ref[:, c * tn:(c + 1) * tn], preferred_element_type=F32)
        up = jnp.dot(xn, w_ref[:, dff + c * tn:dff + (c + 1) * tn], preferred_element_type=F32)
        o_ref[:, c * tn:(c + 1) * tn] = (_silu(gate) * up).astype(BF16)


def swiglu_gu(x, g, w_gu, tm, tn):
    m, d = x.shape
    dff = w_gu.shape[1] // 2
    assert dff % tn == 0
    return pl.pallas_call(
        functools.partial(_swiglu_gu_body, tn=tn),
        grid=(m // tm,),
        in_specs=[pl.BlockSpec((tm, d), lambda i: (i, 0)),
                  pl.BlockSpec((1, d), lambda i: (0, 0)),
                  pl.BlockSpec((d, 2 * dff), lambda i: (0, 0), pipeline_mode=pl.Buffered(1))],
        out_specs=pl.BlockSpec((tm, dff), lambda i: (i, 0)),
        out_shape=jax.ShapeDtypeStruct((m, dff), BF16),
        compiler_params=_params(("parallel",)),
        name="swiglu_gu",
    )(x, g.reshape(1, d), w_gu)


def _mm_res_body(a_ref, w_ref, r_ref, o_ref):
    o_ref[...] = r_ref[...] + jnp.dot(a_ref[...], w_ref[...], preferred_element_type=F32)


def _mm_res_norm_body(a_ref, w_ref, r_ref, g_ref, o_ref):
    y = r_ref[...] + jnp.dot(a_ref[...], w_ref[...], preferred_element_type=F32)
    ms = jnp.mean(y * y, axis=-1, keepdims=True)
    o_ref[...] = y * lax.rsqrt(ms + EPS) * g_ref[...]


def mm_res(a, w, res, tm, final_g=None):
    m, k = a.shape
    n = w.shape[1]
    in_specs = [pl.BlockSpec((tm, k), lambda i: (i, 0)),
                pl.BlockSpec((k, n), lambda i: (0, 0)),
                pl.BlockSpec((tm, n), lambda i: (i, 0))]
    args = [a, w, res]
    body = _mm_res_body
    if final_g is not None:
        in_specs.append(pl.BlockSpec((1, n), lambda i: (0, 0)))
        args.append(final_g.reshape(1, n))
        body = _mm_res_norm_body
    return pl.pallas_call(
        body,
        grid=(m // tm,),
        in_specs=in_specs,
        out_specs=pl.BlockSpec((tm, n), lambda i: (i, 0)),
        out_shape=jax.ShapeDtypeStruct((m, n), F32),
        compiler_params=_params(("parallel",)),
        name="mm_res_norm" if final_g is not None else "mm_res",
    )(*args)


def _qkv_body(x_ref, g_ref, w_ref, wq_lo_ref, q_ref, k_ref, v_ref, xm_ref):
    d = x_ref.shape[1]
    x = x_ref[...]
    ms = jnp.mean(x * x, axis=-1, keepdims=True)
    xn = x * lax.rsqrt(ms + EPS) * g_ref[...]
    hi = xn.astype(BF16)
    lo = (xn - hi.astype(F32)).astype(BF16)
    wq = w_ref[:, 0:d]
    q_ref[...] = (jnp.dot(hi, wq, preferred_element_type=F32) + jnp.dot(lo, wq, preferred_element_type=F32)
                  + jnp.dot(hi, wq_lo_ref[...], preferred_element_type=F32))
    k_ref[...] = jnp.dot(hi, w_ref[:, d:2 * d], preferred_element_type=F32)
    v_ref[...] = jnp.dot(hi, w_ref[:, 2 * d:3 * d], preferred_element_type=F32)
    for blk in range(xm_ref.shape[0]):
        xm_ref[blk] = jnp.mean(xn[blk * MOBA_BLOCK:(blk + 1) * MOBA_BLOCK], axis=0, keepdims=True)


def qkv_proj(x, g, w, wq_lo, tm):
    m, d = x.shape
    assert tm % MOBA_BLOCK == 0
    row = pl.BlockSpec((tm, d), lambda i: (i, 0))
    out = jax.ShapeDtypeStruct((m, d), F32)
    return pl.pallas_call(
        _qkv_body,
        grid=(m // tm,),
        in_specs=[row, pl.BlockSpec((1, d), lambda i: (0, 0)),
                  pl.BlockSpec((d, 3 * d), lambda i: (0, 0), pipeline_mode=pl.Buffered(1)),
                  pl.BlockSpec((d, d), lambda i: (0, 0), pipeline_mode=pl.Buffered(1))],
        out_specs=[row, row, row, pl.BlockSpec((tm // MOBA_BLOCK, 1, d), lambda i: (i, 0, 0))],
        out_shape=[out, out, out, jax.ShapeDtypeStruct((m // MOBA_BLOCK, 1, d), F32)],
        compiler_params=_params(("parallel",)),
        name="qkv_proj",
    )(x, g.reshape(1, d), w, wq_lo)


def _qkv_prompt_body(x_ref, g_ref, w_ref, wq_lo_ref, wkvt_ref, kx_ref, q_ref, ka_ref, va_ref, kt_ref, vt_ref,
                     xm_ref):
    d = x_ref.shape[1]
    n_heads = ka_ref.shape[1]
    tm = x_ref.shape[0]
    half = AUG // 2
    x = x_ref[...]
    ms = jnp.mean(x * x, axis=-1, keepdims=True)
    xn = x * lax.rsqrt(ms + EPS) * g_ref[...]
    hi = xn.astype(BF16)
    lo = (xn - hi.astype(F32)).astype(BF16)
    wq = w_ref[:, 0:d]
    q_ref[...] = (jnp.dot(hi, wq, preferred_element_type=F32) + jnp.dot(lo, wq, preferred_element_type=F32)
                  + jnp.dot(hi, wq_lo_ref[...], preferred_element_type=F32))
    k = jnp.dot(hi, w_ref[:, d:2 * d], preferred_element_type=F32)
    v = jnp.dot(hi, w_ref[:, 2 * d:3 * d], preferred_element_type=F32)
    lane = lax.broadcasted_iota(jnp.int32, (tm, AUG), 1)
    for pair in range(n_heads // 2):
        kb = k[:, pair * AUG:(pair + 1) * AUG]
        vb = v[:, pair * AUG:(pair + 1) * AUG]
        for hh in range(2):
            h = 2 * pair + hh
            data = (lane < half) if hh == 0 else (lane >= half)
            ones_lane = half if hh == 0 else 0
            ka_ref[0, h] = jnp.where(data, kb, kx_ref[h]).astype(BF16)
            va_ref[0, h] = jnp.where(data, vb, jnp.where(lane == ones_lane, 1.0, 0.0)).astype(BF16)
    kt = lax.dot_general(wkvt_ref[0:d, :], hi, _NT, preferred_element_type=F32)
    vt = lax.dot_general(wkvt_ref[d:2 * d, :], hi, _NT, preferred_element_type=F32)
    kt_ref[0] = kt.reshape(n_heads, HEAD_DIM, tm)
    vt_ref[0] = vt.reshape(n_heads, HEAD_DIM, tm)
    for blk in range(xm_ref.shape[0]):
        xm_ref[blk] = jnp.mean(xn[blk * MOBA_BLOCK:(blk + 1) * MOBA_BLOCK], axis=0, keepdims=True)


def qkv_prompt(x, g, w, wq_lo, wkvt, kx, n_seq, seq, tm):
    m, d = x.shape
    n_heads = d // HEAD_DIM
    assert tm % MOBA_BLOCK == 0 and seq % tm == 0
    steps = seq // tm
    row = pl.BlockSpec((tm, d), lambda i: (i, 0))
    aug_spec = pl.BlockSpec((1, n_heads, tm, AUG), lambda i: (i // steps, 0, i % steps, 0))
    t_spec = pl.BlockSpec((1, n_heads, HEAD_DIM, tm), lambda i: (i // steps, 0, 0, i % steps))
    aug_shape = jax.ShapeDtypeStruct((n_seq, n_heads, seq, AUG), BF16)
    t_shape = jax.ShapeDtypeStruct((n_seq, n_heads, HEAD_DIM, seq), F32)

    def whole(shape):
        return pl.BlockSpec(shape, lambda i: (0,) * len(shape), pipeline_mode=pl.Buffered(1))

    return pl.pallas_call(
        _qkv_prompt_body,
        grid=(m // tm,),
        in_specs=[row, pl.BlockSpec((1, d), lambda i: (0, 0)), whole((d, 3 * d)), whole((d, d)), whole((2 * d, d)),
                  pl.BlockSpec((n_heads, tm, AUG), lambda i: (0, i % steps, 0))],
        out_specs=[row, aug_spec, aug_spec, t_spec, t_spec,
                   pl.BlockSpec((tm // MOBA_BLOCK, 1, d), lambda i: (i, 0, 0))],
        out_shape=[jax.ShapeDtypeStruct((m, d), F32), aug_shape, aug_shape, t_shape, t_shape,
                   jax.ShapeDtypeStruct((m // MOBA_BLOCK, 1, d), F32)],
        compiler_params=_params(("parallel",), VMEM_LIMIT_BIG),
        name="qkv_prompt",
    )(x, g.reshape(1, d), w, wq_lo, wkvt, kx)


def _mm_f32_body(a_ref, w_ref, o_ref):
    o_ref[...] = jnp.dot(a_ref[...], w_ref[...], precision=lax.Precision.HIGHEST, preferred_element_type=F32)


def mm_f32(a, w):
    m, k = a.shape
    n = w.shape[1]
    return pl.pallas_call(
        _mm_f32_body,
        grid=(1,),
        in_specs=[pl.BlockSpec((m, k), lambda i: (0, 0)), pl.BlockSpec((k, n), lambda i: (0, 0))],
        out_specs=pl.BlockSpec((m, n), lambda i: (0, 0)),
        out_shape=jax.ShapeDtypeStruct((m, n), F32),
        compiler_params=_params(("arbitrary",)),
        name="mm_f32",
    )(a, w)


def _topk_select(gate, valid, blk, n_blocks):
    tiles = n_blocks // SUBLANES

    def over_blocks(x, op):
        acc = x[0:SUBLANES]
        for i in range(1, tiles):
            acc = op(acc, x[i * SUBLANES:(i + 1) * SUBLANES])
        shift = SUBLANES // 2
        while shift:
            acc = op(acc, pltpu.roll(acc, shift, axis=0))
            shift //= 2
        return jnp.concatenate([acc] * tiles, axis=0) if tiles > 1 else acc

    g = jnp.where(valid, gate, -jnp.inf)
    taken = jnp.zeros(g.shape, jnp.int32)
    for _ in range(MOBA_TOPK):
        best = over_blocks(g, jnp.maximum)
        first = over_blocks(jnp.where(g == best, blk, n_blocks), jnp.minimum)
        pick = blk == first
        taken = jnp.where(pick, 1, taken)
        g = jnp.where(pick, -jnp.inf, g)
    return jnp.logical_and(valid, taken > 0)


def _bf16_trunc(x):
    bits = lax.bitcast_convert_type(x, jnp.uint32) & jnp.uint32(0xFFFF0000)
    return lax.bitcast_convert_type(bits, F32)


def _bf16_round(x):
    bits = lax.bitcast_convert_type(x, jnp.uint32)
    bits = (bits + jnp.uint32(0x7FFF) + ((bits >> 16) & jnp.uint32(1))) & jnp.uint32(0xFFFF0000)
    return lax.bitcast_convert_type(bits, F32)


def _split3(x):
    hi = _bf16_trunc(x)
    mid = _bf16_trunc(x - hi)
    return hi, mid, x - hi - mid


def _moba_prep_body(q_ref, kmean_ref, qa_ref, *, n_blocks):
    half = AUG // 2
    kmean = kmean_ref[...]
    blk = lax.broadcasted_iota(jnp.int32, (n_blocks, MOBA_BLOCK), 0)
    lane = lax.broadcasted_iota(jnp.int32, (MOBA_BLOCK, AUG), 1)
    arow = lax.broadcasted_iota(jnp.int32, (SUBLANES, MOBA_BLOCK), 0)
    ones_rows = jnp.where(arow < 3, 1.0, 0.0).astype(F32)
    gap = jnp.zeros((half - n_blocks - SUBLANES, MOBA_BLOCK), F32)
    other = jnp.zeros((half, MOBA_BLOCK), F32)
    for j in range(q_ref.shape[0] // MOBA_BLOCK):
        sb = pl.program_id(1) * (q_ref.shape[0] // MOBA_BLOCK) + j
        rows = slice(j * MOBA_BLOCK, (j + 1) * MOBA_BLOCK)
        valid = blk < sb
        q = q_ref[rows, :] * (LOG2E * HEAD_DIM ** -0.5)
        for hh in range(2):
            first = hh == 0
            data = (lane < half) if first else (lane >= half)
            gate = lax.dot_general(kmean[:, hh * half:(hh + 1) * half], q_ref[rows, hh * half:(hh + 1) * half], _NT,
                                   precision=lax.Precision.HIGHEST, preferred_element_type=F32)
            sel = _topk_select(gate, valid, blk, n_blocks)
            bias_t = jnp.where(sel, 0.0, jnp.where(blk == sb, 0.0, NEG_BIAS)).astype(F32)
            extras_t = [bias_t, ones_rows, gap]
            aug = jnp.concatenate([other] + extras_t if first else extras_t + [other], axis=0).T
            qa_ref[0, hh, rows, :] = jnp.where(data, q, aug).astype(BF16)


def _key_extras(slopes, seq, n_blocks):
    shape = (slopes.shape[0], seq, AUG)
    pos = jnp.arange(seq, dtype=F32)
    hi, mid, lo = _split3(pos[None, :] * (slopes * LOG2E)[:, None])
    head = lax.broadcasted_iota(jnp.int32, shape, 0)
    key = lax.broadcasted_iota(jnp.int32, shape, 1)
    lane = lax.broadcasted_iota(jnp.int32, shape, 2)
    e = lane - jnp.where(head % 2 == 0, AUG // 2, 0)
    kx = jnp.where(e == key // MOBA_BLOCK, 1.0, 0.0)
    for i, part in enumerate((hi, mid, lo)):
        kx = jnp.where(e == n_blocks + i, part[:, :, None], kx)
    return kx


def moba_prep(q, kmean, n_seq, seq, n_heads):
    n_blocks = seq // MOBA_BLOCK
    assert n_blocks + SUBLANES <= AUG // 2 and n_blocks % SUBLANES == 0
    hpairs = n_heads // 2
    rows = PREP_BLOCKS * MOBA_BLOCK
    n_steps = seq // rows
    assert seq % rows == 0
    return pl.pallas_call(
        functools.partial(_moba_prep_body, n_blocks=n_blocks),
        grid=(hpairs, n_steps, n_seq),
        in_specs=[pl.BlockSpec((rows, LANES), lambda hp, sb, b: (b * n_steps + sb, hp)),
                  pl.BlockSpec((n_blocks, LANES), lambda hp, sb, b: (b, hp))],
        out_specs=pl.BlockSpec((1, 2, rows, AUG), lambda hp, sb, b: (b, hp, sb, 0)),
        out_shape=jax.ShapeDtypeStruct((n_seq, n_heads, seq, AUG), BF16),
        compiler_params=_params(("parallel", "parallel", "parallel")),
        name="moba_prep",
    )(q, kmean)


def _moba_attn_body(qa_ref, ka_ref, va_ref, o_ref, *, n_blocks):
    t = MOBA_BLOCK
    grp = ATTN_GROUP
    rows = grp * t
    kw = ATTN_KV_BLOCKS
    out_lane = lax.broadcasted_iota(jnp.int32, (rows, AUG), 1)

    def update(q, m, acc, hh, first_blk, n_blk, mask=None):
        width = n_blk * t
        start = pl.multiple_of(first_blk * t, t)
        s = lax.dot_general(q, ka_ref[0, hh, pl.ds(start, width), :], _NT, preferred_element_type=F32)
        if mask is not None:
            s = jnp.where(mask, s, -jnp.inf)
        m_new = jnp.maximum(m, jnp.max(s, axis=-1, keepdims=True))
        p = jnp.exp2(s - m_new)
        acc = jnp.exp2(m - m_new) * acc + jnp.dot(p.astype(BF16), va_ref[0, hh, pl.ds(start, width), :],
                                                  preferred_element_type=F32)
        return m_new, acc

    def group(g, carry):
        base = g * grp
        qs = [qa_ref[0, hh, pl.ds(pl.multiple_of(base * t, t), rows), :] for hh in range(2)]

        def past_tiles(kj, state):
            out = []
            for hh in range(2):
                out.extend(update(qs[hh], state[2 * hh], state[2 * hh + 1], hh, kj * kw, kw))
            return tuple(out)

        init = [jnp.full((rows, 1), -jnp.inf, F32), jnp.zeros((rows, AUG), F32)] * 2
        state = list(lax.fori_loop(0, base // kw, past_tiles, tuple(init)))

        rr = lax.broadcasted_iota(jnp.int32, (t, t), 0)
        cc = lax.broadcasted_iota(jnp.int32, (t, t), 1)
        causal = cc <= rr
        outs = []
        for hh in range(2):
            den_lane = HEAD_DIM if hh == 0 else 0
            parts = []
            for a in range(grp):
                sl = slice(a * t, (a + 1) * t)
                m, acc = state[2 * hh][sl], state[2 * hh + 1][sl]
                for b in range(a + 1):
                    m, acc = update(qs[hh][sl], m, acc, hh, base + b, 1, causal if b == a else None)
                parts.append(acc / acc[:, den_lane:den_lane + 1])
            outs.append(jnp.concatenate(parts, axis=0))
        o_ref[pl.ds(pl.multiple_of(base * t, t), rows), :] = jnp.where(
            out_lane < HEAD_DIM, outs[0], outs[1]).astype(BF16)
        return carry

    lax.fori_loop(0, n_blocks // grp, group, 0)


def moba_attn(qa, ka, va):
    n_seq, n_heads, seq, _ = qa.shape
    n_blocks = seq // MOBA_BLOCK
    assert n_blocks % ATTN_GROUP == 0 and ATTN_GROUP % ATTN_KV_BLOCKS == 0
    spec = pl.BlockSpec((1, 2, seq, AUG), lambda b, hp: (b, hp, 0, 0))
    return pl.pallas_call(
        functools.partial(_moba_attn_body, n_blocks=n_blocks),
        grid=(n_seq, n_heads // 2),
        in_specs=[spec, spec, spec],
        out_specs=pl.BlockSpec((seq, LANES), lambda b, hp: (b, hp)),
        out_shape=jax.ShapeDtypeStruct((n_seq * seq, n_heads * HEAD_DIM), BF16),
        compiler_params=_params(("parallel", "parallel")),
        name="moba_attn",
    )(qa, ka, va)


def _sample_attn_body(pt_ref, rowinfo_ref, q_ref, kn_ref, vn_ref, *rest, n_pages, pps, n_new, n_heads):
    k_refs = rest[:pps]
    v_refs = rest[pps:2 * pps]
    o_ref = rest[2 * pps]
    qbd_ref, qbd32_ref, s_ref, ksum_ref, acc_ref, knt_ref, vnt_ref, bias_ref = rest[2 * pps + 1:]
    t = pl.program_id(1)
    tk = n_pages // pps
    page = k_refs[0].shape[3]
    d = q_ref.shape[1]
    past = n_pages * page
    n_blocks = past // MOBA_BLOCK
    ppb = MOBA_BLOCK // page
    n_rows = n_heads * n_new
    head_of_col = lax.broadcasted_iota(jnp.int32, (n_heads, d), 1) // HEAD_DIM
    head_of_row = lax.broadcasted_iota(jnp.int32, (n_heads, d), 0)
    own_head = head_of_col == head_of_row

    def key_major(rows_val):
        padded = jnp.concatenate([rows_val, jnp.zeros((LANES - rows_val.shape[0], d), F32)], axis=0)
        return padded.T

    @pl.when(t == 0)
    def _():
        q = q_ref[...]
        qbd = jnp.concatenate([jnp.where(own_head, jnp.broadcast_to(q[n:n + 1, :], (n_heads, d)), 0.0)
                               for n in range(n_new)], axis=0)
        qbd32_ref[0:n_rows, :] = qbd
        qbd32_ref[n_rows:, :] = jnp.zeros((LANES - n_rows, d), F32)
        qbd_ref[...] = (qbd * HEAD_DIM ** -0.5).astype(BF16)
        knt_ref[...] = key_major(kn_ref[...])
        vnt_ref[...] = key_major(vn_ref[...])
        acc_ref[...] = jnp.zeros_like(acc_ref)
        ksum_ref[...] = jnp.zeros_like(ksum_ref)

    @pl.when(t < tk)
    def _():
        qbd = qbd_ref[...]
        lane_i = lax.broadcasted_iota(jnp.int32, (1, LANES), 1)
        for bi in range(pps // ppb):
            ksum = jnp.zeros((d, page), F32)
            for pi in range(ppb):
                i = bi * ppb + pi
                kp = k_refs[i][0].reshape(d, page)
                s_ref[t * pps + i] = jnp.dot(qbd, kp.astype(BF16), preferred_element_type=F32)
                ksum = ksum + kp
            onehot = jnp.where(lane_i == t * (pps // ppb) + bi, 1.0, 0.0).astype(F32)
            ksum_ref[...] += jnp.sum(ksum, axis=1, keepdims=True) * onehot

    @pl.when(t == tk - 1)
    def _():
        s_ref[n_pages] = jnp.dot(qbd_ref[...], knt_ref[...].astype(BF16), preferred_element_type=F32)
        gate = jnp.dot(qbd32_ref[...], ksum_ref[...], precision=lax.Precision.HIGHEST,
                       preferred_element_type=F32) * (1.0 / MOBA_BLOCK)
        gate_t = gate.T[:n_blocks]
        blk = lax.broadcasted_iota(jnp.int32, (n_blocks, LANES), 0)
        sel_t = _topk_select(gate_t, blk >= 0, blk, n_blocks)
        sel = jnp.concatenate([jnp.where(sel_t, 1.0, 0.0).astype(F32),
                               jnp.zeros((LANES - n_blocks, LANES), F32)], axis=0).T[:n_rows]
        q_idx = rowinfo_ref[0]
        slope = rowinfo_ref[1]
        lane_f = lax.broadcasted_iota(jnp.int32, (n_rows, LANES), 1).astype(F32)

        for b in range(n_blocks):
            bias_ref[b] = jnp.broadcast_to(jnp.where(sel[:, b:b + 1] > 0.5, 0.0, -jnp.inf), (n_rows, LANES))

        def logits(p):
            dist = (past + q_idx) - (jnp.asarray(p * page, F32) + lane_f)
            return s_ref[p] - slope * dist + bias_ref[p // ppb]

        dist_new = q_idx - lane_f
        s_new = jnp.where((dist_new >= 0) & (lane_f < n_new), s_ref[n_pages] - slope * dist_new, -jnp.inf)
        m = jnp.max(lax.fori_loop(0, n_pages, lambda p, mv: jnp.maximum(mv, logits(p)), s_new),
                    axis=1, keepdims=True)

        def exp_step(p, den):
            e = jnp.exp(logits(p) - m)
            s_ref[p] = e
            return den + e

        e_new = jnp.exp(s_new - m)
        inv = 1.0 / jnp.sum(lax.fori_loop(0, n_pages, exp_step, e_new), axis=1, keepdims=True)
        s_ref[n_pages] = e_new * inv

        def norm_step(p, carry):
            s_ref[p] = s_ref[p] * inv
            return carry

        lax.fori_loop(0, n_pages, norm_step, 0)

    def weighted(p_idx, v_t):
        return lax.dot_general(s_ref[p_idx].astype(BF16), v_t.astype(BF16), _NT, preferred_element_type=F32)

    @pl.when(t >= tk)
    def _():
        acc = acc_ref[...]
        for i in range(pps):
            acc = acc + weighted((t - tk) * pps + i, v_refs[i][0].reshape(d, page))
        acc_ref[...] = acc

    @pl.when(t == 2 * tk - 1)
    def _():
        acc = acc_ref[...] + weighted(n_pages, vnt_ref[...])
        outs = [jnp.sum(jnp.where(own_head, acc[n * n_heads:(n + 1) * n_heads, :], 0.0), axis=0, keepdims=True)
                for n in range(n_new)]
        outs.append(jnp.zeros((SAMPLE_PAD - n_new, d), F32))
        o_ref[...] = jnp.concatenate(outs, axis=0).astype(BF16)


def sample_attn(page_table, rowinfo, q, k_new, v_new, cache_kt, cache_vt, n_new, pps=16):
    n_dec, n_pages = page_table.shape
    _, n_heads, _, page = cache_kt.shape
    d = n_heads * HEAD_DIM
    ppb = MOBA_BLOCK // page
    assert n_pages % pps == 0 and pps % ppb == 0 and page == LANES
    assert n_new <= SAMPLE_PAD and n_heads % SUBLANES == 0
    tk = n_pages // pps
    n_blocks = n_pages // ppb
    n_rows = n_heads * n_new
    assert n_blocks <= LANES and n_rows <= LANES

    def k_spec(i):
        return pl.BlockSpec((1, n_heads, HEAD_DIM, page),
                            lambda b, t, pt: (pt[b, jnp.minimum(t, tk - 1) * pps + i], 0, 0, 0))

    def v_spec(i):
        return pl.BlockSpec((1, n_heads, HEAD_DIM, page),
                            lambda b, t, pt: (pt[b, jnp.maximum(t - tk, 0) * pps + i], 0, 0, 0))

    row_spec = pl.BlockSpec((SAMPLE_PAD, d), lambda b, t, pt: (b, 0))
    grid_spec = pltpu.PrefetchScalarGridSpec(
        num_scalar_prefetch=1,
        grid=(n_dec, 2 * tk),
        in_specs=[pl.BlockSpec((2, n_rows, LANES), lambda b, t, pt: (0, 0, 0)), row_spec, row_spec, row_spec]
                 + [k_spec(i) for i in range(pps)] + [v_spec(i) for i in range(pps)],
        out_specs=row_spec,
        scratch_shapes=[pltpu.VMEM((n_rows, d), BF16),
                        pltpu.VMEM((LANES, d), F32),
                        pltpu.VMEM((n_pages + 1, n_rows, LANES), F32),
                        pltpu.VMEM((d, LANES), F32),
                        pltpu.VMEM((n_rows, d), F32),
                        pltpu.VMEM((d, LANES), F32),
                        pltpu.VMEM((d, LANES), F32),
                        pltpu.VMEM((n_blocks, n_rows, LANES), F32)],
    )
    return pl.pallas_call(
        functools.partial(_sample_attn_body, n_pages=n_pages, pps=pps, n_new=n_new, n_heads=n_heads),
        grid_spec=grid_spec,
        out_shape=jax.ShapeDtypeStruct((n_dec * SAMPLE_PAD, d), BF16),
        compiler_params=_params(("parallel", "arbitrary")),
        name="sample_attn",
    )(page_table, rowinfo, q, k_new, v_new, *([cache_kt] * pps), *([cache_vt] * pps))


def _ssd_body(z_ref, xs_ref, bc_ref, dt_ref, cinit_ref, sinit_ref, cw_ref, cb_ref, dtb_ref, alog_ref,
              dsk_ref, gn_ref, y_ref, cout_ref, sout_ref, ext_ref, xc_ref, h_ref, *, rows, valid, n_chunks):
    cl = SSD_CHUNK
    c = pl.program_id(1)
    d_inner = xs_ref.shape[1]
    n_heads = d_inner // SSM_HEAD_DIM
    hpg = n_heads // N_GROUPS
    gw = hpg * SSM_HEAD_DIM
    tail = SUBLANES

    @pl.when(c == 0)
    def _():
        ext_ref[0:tail, :] = cinit_ref[0]
        h_ref[...] = sinit_ref[0].reshape(h_ref.shape)
        if rows < cl:
            ext_ref[tail + rows:, :] = jnp.zeros((cl - rows, ext_ref.shape[1]), F32)

    @pl.when(c > 0)
    def _():
        ext_ref[0:tail, :] = ext_ref[cl:cl + tail, :]

    ext_ref[tail:tail + rows, 0:d_inner] = xs_ref[...]
    ext_ref[tail:tail + rows, d_inner:] = bc_ref[...]

    cw = cw_ref[...]
    conv = cb_ref[...] + ext_ref[tail:tail + cl, :] * cw[CONV_W - 1:CONV_W, :]
    for back in range(1, CONV_W):
        conv = conv + ext_ref[tail - back:tail - back + cl, :] * cw[CONV_W - 1 - back:CONV_W - back, :]
    xc_ref[...] = _silu(conv)

    def pad(v):
        if rows == cl:
            return v
        return jnp.concatenate([v, jnp.zeros((cl - rows, v.shape[1]), v.dtype)], axis=0)

    trow = lax.broadcasted_iota(jnp.int32, (cl, LANES), 0)
    dt_raw = pad(dt_ref[...]) + dtb_ref[...]
    dt = jnp.maximum(dt_raw, 0.0) + jnp.log1p(jnp.exp(-jnp.abs(dt_raw)))
    dt = jnp.where(trow < valid, dt, 0.0)
    a = -jnp.exp(alog_ref[...])
    tri_r = lax.broadcasted_iota(jnp.int32, (cl, cl), 0)
    tri_c = lax.broadcasted_iota(jnp.int32, (cl, cl), 1)
    causal = tri_c <= tri_r
    a_cs = jnp.dot(jnp.where(causal, 1.0, 0.0).astype(F32), dt * a, precision=lax.Precision.HIGHEST,
                   preferred_element_type=F32)
    a_cs_t = a_cs.T
    dt_t = dt.T
    z = pad(z_ref[...])

    for g in range(N_GROUPS):
        bg = xc_ref[:, d_inner + g * D_STATE:d_inner + (g + 1) * D_STATE].astype(BF16)
        cg = xc_ref[:, d_inner + (N_GROUPS + g) * D_STATE:d_inner + (N_GROUPS + g + 1) * D_STATE].astype(BF16)
        cb = lax.dot_general(cg, bg, _NT, preferred_element_type=F32)
        xg = xc_ref[:, g * gw:(g + 1) * gw]
        xg_t = xg.T
        h_old = h_ref[g * gw:(g + 1) * gw, :]
        y_off = lax.dot_general(cg, h_old.astype(BF16), _NT, preferred_element_type=F32)
        ys, xw_rows, h_scaled = [], [], []
        for e in range(hpg):
            h = g * hpg + e
            acs_col = a_cs[:, h:h + 1]
            acs_row = a_cs_t[h:h + 1, :]
            dt_row = dt_t[h:h + 1, :]
            a_end = a_cs_t[h:h + 1, cl - 1:cl]
            decay = jnp.exp(jnp.where(causal, acs_col - acs_row, -jnp.inf))
            w = (cb * decay * dt_row).astype(BF16)
            xh = xg[:, e * SSM_HEAD_DIM:(e + 1) * SSM_HEAD_DIM]
            y_diag = jnp.dot(w, xh.astype(BF16), preferred_element_type=F32)
            ys.append(y_diag + y_off[:, e * SSM_HEAD_DIM:(e + 1) * SSM_HEAD_DIM] * jnp.exp(acs_col))
            to_end = jnp.exp(a_end - acs_row) * dt_row
            xw_rows.append(xg_t[e * SSM_HEAD_DIM:(e + 1) * SSM_HEAD_DIM, :] * to_end)
            h_scaled.append(h_old[e * SSM_HEAD_DIM:(e + 1) * SSM_HEAD_DIM, :] * jnp.exp(a_end))
        states = jnp.dot(jnp.concatenate(xw_rows, axis=0).astype(BF16), bg, preferred_element_type=F32)
        h_ref[g * gw:(g + 1) * gw, :] = jnp.concatenate(h_scaled, axis=0) + states
        y = jnp.concatenate(ys, axis=1) + dsk_ref[:, g * gw:(g + 1) * gw] * xg
        yz = y * _silu(z[:, g * gw:(g + 1) * gw])
        ms = jnp.mean(yz * yz, axis=-1, keepdims=True)
        yn = yz * lax.rsqrt(ms + EPS) * gn_ref[:, g * gw:(g + 1) * gw]
        y_ref[:, g * gw:(g + 1) * gw] = yn[:rows].astype(BF16)

    @pl.when(c == n_chunks - 1)
    def _():
        sout_ref[0] = h_ref[...].reshape(sout_ref.shape[1:])
        first = tail + ((valid - (CONV_W - 1)) // SUBLANES) * SUBLANES
        cout_ref[0] = ext_ref[first:first + SUBLANES, :]


def ssd_mixer(proj, conv_init, ssm_init, cw, cb, dtb, alog, dsk, gnorm, n_seq, rows, valid, d_inner):
    total_rows = proj.shape[0] // n_seq
    n_chunks = total_rows // rows
    assert rows == SSD_CHUNK or n_chunks == 1
    assert CONV_W - 1 <= valid <= rows
    n_heads = d_inner // SSM_HEAD_DIM
    conv_dim = d_inner + 2 * N_GROUPS * D_STATE
    assert conv_dim == 2 * d_inner

    def col(width, idx):
        return pl.BlockSpec((rows, width), lambda b, c: (b * n_chunks + c, idx))

    def const(shape):
        return pl.BlockSpec(shape, lambda b, c: (0,) * len(shape))

    outs = pl.pallas_call(
        functools.partial(_ssd_body, rows=rows, valid=valid, n_chunks=n_chunks),
        grid=(n_seq, n_chunks),
        in_specs=[col(d_inner, 0), col(d_inner, 1), col(d_inner, 2), col(LANES, 3 * d_inner // LANES),
                  pl.BlockSpec((1, SUBLANES, conv_dim), lambda b, c: (b, 0, 0)),
                  pl.BlockSpec((1, n_heads, SSM_HEAD_DIM, D_STATE), lambda b, c: (b, 0, 0, 0)),
                  const((CONV_W, conv_dim)), const((1, conv_dim)), const((1, LANES)), const((1, LANES)),
                  const((1, d_inner)), const((1, d_inner))],
        out_specs=[pl.BlockSpec((rows, d_inner), lambda b, c: (b * n_chunks + c, 0)),
                   pl.BlockSpec((1, SUBLANES, conv_dim), lambda b, c: (b, 0, 0)),
                   pl.BlockSpec((1, n_heads, SSM_HEAD_DIM, D_STATE), lambda b, c: (b, 0, 0, 0))],
        out_shape=[jax.ShapeDtypeStruct((proj.shape[0], d_inner), BF16),
                   jax.ShapeDtypeStruct((n_seq, SUBLANES, conv_dim), F32),
                   jax.ShapeDtypeStruct((n_seq, n_heads, SSM_HEAD_DIM, D_STATE), F32)],
        scratch_shapes=[pltpu.VMEM((SUBLANES + SSD_CHUNK, conv_dim), F32),
                        pltpu.VMEM((SSD_CHUNK, conv_dim), F32),
                        pltpu.VMEM((n_heads * SSM_HEAD_DIM, D_STATE), F32)],
        compiler_params=_params(("parallel", "arbitrary")),
        name="ssd_mixer",
    )(proj, proj, proj, proj, conv_init, ssm_init, cw, cb, dtb, alog, dsk, gnorm)
    return outs


def _pad_cols(w, n):
    return jnp.pad(w, ((0, 0), (0, n - w.shape[1])))


def _pad_lanes(v):
    return jnp.pad(v, (0, LANES - v.shape[0])).reshape(1, LANES)


def kernel(x_prompt, x_sample, cache_k, cache_v, state_conv, state_ssm, page_table, norm_mix, norm_ffn, w_qkv, w_o, w_in_ssm, conv_w, conv_b, dt_bias, a_log, d_skip, norm_ssm, w_out_ssm, w_gate_up, w_down, norm_final):
    n_seq, seq, d_model = x_prompt.shape
    n_dec, n_new, _ = x_sample.shape
    n_heads = d_model // HEAD_DIM
    d_inner = norm_ssm.shape[1]
    ssm_heads = d_inner // SSM_HEAD_DIM
    conv_dim = conv_w.shape[2]
    tm_p = 512
    tm_s = n_dec * SAMPLE_PAD

    hp = x_prompt.reshape(n_seq * seq, d_model)
    hs = jnp.pad(x_sample, ((0, 0), (0, SAMPLE_PAD - n_new), (0, 0))).reshape(tm_s, d_model)

    wqkv = w_qkv[0]
    wq_hi = _bf16_round(wqkv[:, :d_model])
    w_hi = jnp.concatenate([wq_hi, wqkv[:, d_model:]], axis=1).astype(BF16)
    wq_lo = (wqkv[:, :d_model] - wq_hi).astype(BF16)
    wkvt = wqkv[:, d_model:].T.astype(BF16)
    slopes = jnp.exp2(-8.0 * (jnp.arange(n_heads, dtype=F32) + 1.0) / n_heads)
    kx = _key_extras(slopes, seq, seq // MOBA_BLOCK)
    q_p, ka, va, kt_p, vt_p, xm_p = qkv_prompt(hp, norm_mix[0], w_hi, wq_lo, wkvt, kx, n_seq, seq, tm_p)
    q_s, k_s, v_s, _ = qkv_proj(hs, norm_mix[0], w_hi, wq_lo, tm_s)
    kmean = mm_f32(xm_p.reshape(n_seq * seq // MOBA_BLOCK, d_model), wqkv[:, d_model:2 * d_model])
    qa = moba_prep(q_p, kmean, n_seq, seq, n_heads)
    attn_p = moba_attn(qa, ka, va)
    rows = jnp.arange(n_heads * n_new)
    rowinfo = jnp.stack([(rows // n_heads).astype(F32), slopes[rows % n_heads]])
    rowinfo = jnp.broadcast_to(rowinfo[:, :, None], (2, n_heads * n_new, LANES))
    cache_kt = jnp.transpose(cache_k[0], (0, 2, 3, 1))
    cache_vt = jnp.transpose(cache_v[0], (0, 2, 3, 1))
    attn_s = sample_attn(page_table, rowinfo, q_s, k_s, v_s, cache_kt, cache_vt, n_new)
    wo = w_o[0].astype(BF16)
    hp = mm_res(attn_p, wo, hp, tm_p)
    hs = mm_res(attn_s, wo, hs, tm_s)

    k_prompt = jnp.transpose(kt_p, (0, 3, 1, 2))[None]
    v_prompt = jnp.transpose(vt_p, (0, 3, 1, 2))[None]
    k_sample = k_s.reshape(n_dec, SAMPLE_PAD, n_heads, HEAD_DIM)[None, :, :n_new]
    v_sample = v_s.reshape(n_dec, SAMPLE_PAD, n_heads, HEAD_DIM)[None, :, :n_new]

    def ffn(h, layer, tm, final_g=None):
        act = swiglu_gu(h, norm_ffn[layer], w_gate_up[layer].astype(BF16), tm, 256)
        return mm_res(act, w_down[layer].astype(BF16), h, tm, final_g)

    hp = ffn(hp, 0, tm_p)
    hs = ffn(hs, 0, tm_s)

    in_cols = d_inner + conv_dim + ssm_heads
    assert w_in_ssm.shape[2] == in_cols
    proj_w = ((in_cols + 1279) // 1280) * 1280
    w_in = _pad_cols(w_in_ssm[0], proj_w).astype(BF16)
    proj_p = norm_mm(hp, norm_mix[1], w_in, tm_p, 1280)
    proj_s = norm_mm(hs, norm_mix[1], w_in, tm_s, 1280)
    cb = conv_b[0].reshape(1, conv_dim)
    dtb = _pad_lanes(dt_bias[0])
    alog = _pad_lanes(a_log[0])
    dsk = jnp.repeat(d_skip[0], SSM_HEAD_DIM).reshape(1, d_inner)
    gn = norm_ssm[0].reshape(1, d_inner)
    zero_conv = jnp.zeros((n_seq, SUBLANES, conv_dim), F32)
    zero_ssm = jnp.zeros((n_seq, ssm_heads, SSM_HEAD_DIM, D_STATE), F32)
    y_p, conv_p, ssm_p = ssd_mixer(proj_p, zero_conv, zero_ssm, conv_w[0], cb, dtb, alog, dsk, gn,
                                   n_seq, SSD_CHUNK, SSD_CHUNK, d_inner)
    conv_init_s = jnp.pad(state_conv[0], ((0, 0), (SUBLANES - (CONV_W - 1), 0), (0, 0)))
    y_s, conv_s, ssm_s = ssd_mixer(proj_s, conv_init_s, state_ssm[0], conv_w[0], cb, dtb, alog, dsk, gn,
                                   n_dec, SAMPLE_PAD, n_new, d_inner)
    w_out = w_out_ssm[0].astype(BF16)
    hp = mm_res(y_p, w_out, hp, tm_p)
    hs = mm_res(y_s, w_out, hs, tm_s)
    y_prompt = ffn(hp, 1, tm_p, norm_final).reshape(n_seq, seq, d_model)
    y_sample = ffn(hs, 1, tm_s, norm_final).reshape(n_dec, SAMPLE_PAD, d_model)[:, :n_new]

    lo_p = SUBLANES - (CONV_W - 1)
    lo_s = (n_new - (CONV_W - 1)) % SUBLANES
    return (y_prompt, y_sample, k_prompt, v_prompt, k_sample, v_sample,
            conv_p[None, :, lo_p:lo_p + CONV_W - 1], ssm_p[None],
            conv_s[None, :, lo_s:lo_s + CONV_W - 1], ssm_s[None])
```

```python
import functools
import math

import jax
import jax.numpy as jnp
from jax import lax
from jax.experimental import pallas as pl
from jax.experimental.pallas import tpu as pltpu

F32 = jnp.float32
BF16 = jnp.bfloat16

EPS = 1e-5
HEAD_DIM = 64
MOBA_BLOCK = 256
MOBA_TOPK = 3
SSM_HEAD_DIM = 64
N_GROUPS = 8
D_STATE = 128
CONV_W = 4
SSD_CHUNK = 128
SAMPLE_PAD = 16
AUG = 128
NEG_BIAS = -1e30
LOG2E = math.log2(math.e)
ATTN_GROUP = 4
ATTN_KV_BLOCKS = 4
PREP_BLOCKS = 8
LANES = 128
SUBLANES = 8
VMEM_LIMIT = 48 * 1024 * 1024
VMEM_LIMIT_BIG = 56 * 1024 * 1024

_NT = (((1,), (1,)), ((), ()))


def _silu(x):
    return x / (1.0 + jnp.exp(-x))


def _params(sem, vmem=VMEM_LIMIT):
    return pltpu.CompilerParams(dimension_semantics=sem, vmem_limit_bytes=vmem)


def _rms_bf16(x_ref, g_ref):
    x = x_ref[...]
    ms = jnp.mean(x * x, axis=-1, keepdims=True)
    return (x * lax.rsqrt(ms + EPS) * g_ref[...]).astype(BF16)


def _norm_mm_body(x_ref, g_ref, w_ref, o_ref, *, tn):
    xn = _rms_bf16(x_ref, g_ref)
    for c in range(w_ref.shape[1] // tn):
        o_ref[:, c * tn:(c + 1) * tn] = jnp.dot(xn, w_ref[:, c * tn:(c + 1) * tn], preferred_element_type=F32)


def norm_mm(x, g, w, tm, tn):
    m, d = x.shape
    n = w.shape[1]
    assert n % tn == 0
    return pl.pallas_call(
        functools.partial(_norm_mm_body, tn=tn),
        grid=(m // tm,),
        in_specs=[pl.BlockSpec((tm, d), lambda i: (i, 0)),
                  pl.BlockSpec((1, d), lambda i: (0, 0)),
                  pl.BlockSpec((d, n), lambda i: (0, 0), pipeline_mode=pl.Buffered(1))],
        out_specs=pl.BlockSpec((tm, n), lambda i: (i, 0)),
        out_shape=jax.ShapeDtypeStruct((m, n), F32),
        compiler_params=_params(("parallel",), VMEM_LIMIT_BIG),
        name="norm_mm",
    )(x, g.reshape(1, d), w)


def _swiglu_gu_body(x_ref, g_ref, w_ref, o_ref, *, tn):
    xn = _rms_bf16(x_ref, g_ref)
    dff = o_ref.shape[1]
    for c in range(dff // tn):
        gate = jnp.dot(xn, w_ref[:, c * tn:(c + 1) * tn], preferred_element_type=F32)
        up = jnp.dot(xn, w_ref[:, dff + c * tn:dff + (c + 1) * tn], preferred_element_type=F32)
        o_ref[:, c * tn:(c + 1) * tn] = (_silu(gate) * up).astype(BF16)


def swiglu_gu(x, g, w_gu, tm, tn):
    m, d = x.shape
    dff = w_gu.shape[1] // 2
    assert dff % tn == 0
    return pl.pallas_call(
        functools.partial(_swiglu_gu_body, tn=tn),
        grid=(m // tm,),
        in_specs=[pl.BlockSpec((tm, d), lambda i: (i, 0)),
                  pl.BlockSpec((1, d), lambda i: (0, 0)),
                  pl.BlockSpec((d, 2 * dff), lambda i: (0, 0), pipeline_mode=pl.Buffered(1))],
        out_specs=pl.BlockSpec((tm, dff), lambda i: (i, 0)),
        out_shape=jax.ShapeDtypeStruct((m, dff), BF16),
        compiler_params=_params(("parallel",)),
        name="swiglu_gu",
    )(x, g.reshape(1, d), w_gu)


def _mm_res_body(a_ref, w_ref, r_ref, o_ref):
    o_ref[...] = r_ref[...] + jnp.dot(a_ref[...], w_ref[...], preferred_element_type=F32)


def _mm_res_norm_body(a_ref, w_ref, r_ref, g_ref, o_ref):
    y = r_ref[...] + jnp.dot(a_ref[...], w_ref[...], preferred_element_type=F32)
    ms = jnp.mean(y * y, axis=-1, keepdims=True)
    o_ref[...] = y * lax.rsqrt(ms + EPS) * g_ref[...]


def mm_res(a, w, res, tm, final_g=None):
    m, k = a.shape
    n = w.shape[1]
    in_specs = [pl.BlockSpec((tm, k), lambda i: (i, 0)),
                pl.BlockSpec((k, n), lambda i: (0, 0)),
                pl.BlockSpec((tm, n), lambda i: (i, 0))]
    args = [a, w, res]
    body = _mm_res_body
    if final_g is not None:
        in_specs.append(pl.BlockSpec((1, n), lambda i: (0, 0)))
        args.append(final_g.reshape(1, n))
        body = _mm_res_norm_body
    return pl.pallas_call(
        body,
        grid=(m // tm,),
        in_specs=in_specs,
        out_specs=pl.BlockSpec((tm, n), lambda i: (i, 0)),
        out_shape=jax.ShapeDtypeStruct((m, n), F32),
        compiler_params=_params(("parallel",)),
        name="mm_res_norm" if final_g is not None else "mm_res",
    )(*args)


def _qkv_body(x_ref, g_ref, w_ref, wq_lo_ref, q_ref, k_ref, v_ref, xm_ref):
    d = x_ref.shape[1]
    x = x_ref[...]
    ms = jnp.mean(x * x, axis=-1, keepdims=True)
    xn = x * lax.rsqrt(ms + EPS) * g_ref[...]
    hi = xn.astype(BF16)
    lo = (xn - hi.astype(F32)).astype(BF16)
    wq = w_ref[:, 0:d]
    q_ref[...] = (jnp.dot(hi, wq, preferred_element_type=F32) + jnp.dot(lo, wq, preferred_element_type=F32)
                  + jnp.dot(hi, wq_lo_ref[...], preferred_element_type=F32))
    k_ref[...] = jnp.dot(hi, w_ref[:, d:2 * d], preferred_element_type=F32)
    v_ref[...] = jnp.dot(hi, w_ref[:, 2 * d:3 * d], preferred_element_type=F32)
    for blk in range(xm_ref.shape[0]):
        xm_ref[blk] = jnp.mean(xn[blk * MOBA_BLOCK:(blk + 1) * MOBA_BLOCK], axis=0, keepdims=True)


def qkv_proj(x, g, w, wq_lo, tm):
    m, d = x.shape
    assert tm % MOBA_BLOCK == 0
    row = pl.BlockSpec((tm, d), lambda i: (i, 0))
    out = jax.ShapeDtypeStruct((m, d), F32)
    return pl.pallas_call(
        _qkv_body,
        grid=(m // tm,),
        in_specs=[row, pl.BlockSpec((1, d), lambda i: (0, 0)),
                  pl.BlockSpec((d, 3 * d), lambda i: (0, 0), pipeline_mode=pl.Buffered(1)),
                  pl.BlockSpec((d, d), lambda i: (0, 0), pipeline_mode=pl.Buffered(1))],
        out_specs=[row, row, row, pl.BlockSpec((tm // MOBA_BLOCK, 1, d), lambda i: (i, 0, 0))],
        out_shape=[out, out, out, jax.ShapeDtypeStruct((m // MOBA_BLOCK, 1, d), F32)],
        compiler_params=_params(("parallel",)),
        name="qkv_proj",
    )(x, g.reshape(1, d), w, wq_lo)


def _qkv_prompt_body(x_ref, g_ref, w_ref, wq_lo_ref, wkvt_ref, kx_ref, q_ref, ka_ref, va_ref, kt_ref, vt_ref,
                     xm_ref):
    d = x_ref.shape[1]
    n_heads = ka_ref.shape[1]
    tm = x_ref.shape[0]
    half = AUG // 2
    x = x_ref[...]
    ms = jnp.mean(x * x, axis=-1, keepdims=True)
    xn = x * lax.rsqrt(ms + EPS) * g_ref[...]
    hi = xn.astype(BF16)
    lo = (xn - hi.astype(F32)).astype(BF16)
    wq = w_ref[:, 0:d]
    q_ref[...] = (jnp.dot(hi, wq, preferred_element_type=F32) + jnp.dot(lo, wq, preferred_element_type=F32)
                  + jnp.dot(hi, wq_lo_ref[...], preferred_element_type=F32))
    k = jnp.dot(hi, w_ref[:, d:2 * d], preferred_element_type=F32)
    v = jnp.dot(hi, w_ref[:, 2 * d:3 * d], preferred_element_type=F32)
    lane = lax.broadcasted_iota(jnp.int32, (tm, AUG), 1)
    for pair in range(n_heads // 2):
        kb = k[:, pair * AUG:(pair + 1) * AUG]
        vb = v[:, pair * AUG:(pair + 1) * AUG]
        for hh in range(2):
            h = 2 * pair + hh
            data = (lane < half) if hh == 0 else (lane >= half)
            ones_lane = half if hh == 0 else 0
            ka_ref[0, h] = jnp.where(data, kb, kx_ref[h]).astype(BF16)
            va_ref[0, h] = jnp.where(data, vb, jnp.where(lane == ones_lane, 1.0, 0.0)).astype(BF16)
    kt = lax.dot_general(wkvt_ref[0:d, :], hi, _NT, preferred_element_type=F32)
    vt = lax.dot_general(wkvt_ref[d:2 * d, :], hi, _NT, preferred_element_type=F32)
    kt_ref[0] = kt.reshape(n_heads, HEAD_DIM, tm)
    vt_ref[0] = vt.reshape(n_heads, HEAD_DIM, tm)
    for blk in range(xm_ref.shape[0]):
        xm_ref[blk] = jnp.mean(xn[blk * MOBA_BLOCK:(blk + 1) * MOBA_BLOCK], axis=0, keepdims=True)


def qkv_prompt(x, g, w, wq_lo, wkvt, kx, n_seq, seq, tm):
    m, d = x.shape
    n_heads = d // HEAD_DIM
    assert tm % MOBA_BLOCK == 0 and seq % tm == 0
    steps = seq // tm
    row = pl.BlockSpec((tm, d), lambda i: (i, 0))
    aug_spec = pl.BlockSpec((1, n_heads, tm, AUG), lambda i: (i // steps, 0, i % steps, 0))
    t_spec = pl.BlockSpec((1, n_heads, HEAD_DIM, tm), lambda i: (i // steps, 0, 0, i % steps))
    aug_shape = jax.ShapeDtypeStruct((n_seq, n_heads, seq, AUG), BF16)
    t_shape = jax.ShapeDtypeStruct((n_seq, n_heads, HEAD_DIM, seq), F32)

    def whole(shape):
        return pl.BlockSpec(shape, lambda i: (0,) * len(shape), pipeline_mode=pl.Buffered(1))

    return pl.pallas_call(
        _qkv_prompt_body,
        grid=(m // tm,),
        in_specs=[row, pl.BlockSpec((1, d), lambda i: (0, 0)), whole((d, 3 * d)), whole((d, d)), whole((2 * d, d)),
                  pl.BlockSpec((n_heads, tm, AUG), lambda i: (0, i % steps, 0))],
        out_specs=[row, aug_spec, aug_spec, t_spec, t_spec,
                   pl.BlockSpec((tm // MOBA_BLOCK, 1, d), lambda i: (i, 0, 0))],
        out_shape=[jax.ShapeDtypeStruct((m, d), F32), aug_shape, aug_shape, t_shape, t_shape,
                   jax.ShapeDtypeStruct((m // MOBA_BLOCK, 1, d), F32)],
        compiler_params=_params(("parallel",), VMEM_LIMIT_BIG),
        name="qkv_prompt",
    )(x, g.reshape(1, d), w, wq_lo, wkvt, kx)


def _mm_f32_body(a_ref, w_ref, o_ref):
    o_ref[...] = jnp.dot(a_ref[...], w_ref[...], precision=lax.Precision.HIGHEST, preferred_element_type=F32)


def mm_f32(a, w):
    m, k = a.shape
    n = w.shape[1]
    return pl.pallas_call(
        _mm_f32_body,
        grid=(1,),
        in_specs=[pl.BlockSpec((m, k), lambda i: (0, 0)), pl.BlockSpec((k, n), lambda i: (0, 0))],
        out_specs=pl.BlockSpec((m, n), lambda i: (0, 0)),
        out_shape=jax.ShapeDtypeStruct((m, n), F32),
        compiler_params=_params(("arbitrary",)),
        name="mm_f32",
    )(a, w)


def _topk_select(gate, valid, blk, n_blocks):
    tiles = n_blocks // SUBLANES

    def over_blocks(x, op):
        acc = x[0:SUBLANES]
        for i in range(1, tiles):
            acc = op(acc, x[i * SUBLANES:(i + 1) * SUBLANES])
        shift = SUBLANES // 2
        while shift:
            acc = op(acc, pltpu.roll(acc, shift, axis=0))
            shift //= 2
        return jnp.concatenate([acc] * tiles, axis=0) if tiles > 1 else acc

    g = jnp.where(valid, gate, -jnp.inf)
    taken = jnp.zeros(g.shape, jnp.int32)
    for _ in range(MOBA_TOPK):
        best = over_blocks(g, jnp.maximum)
        first = over_blocks(jnp.where(g == best, blk, n_blocks), jnp.minimum)
        pick = blk == first
        taken = jnp.where(pick, 1, taken)
        g = jnp.where(pick, -jnp.inf, g)
    return jnp.logical_and(valid, taken > 0)


def _bf16_trunc(x):
    bits = lax.bitcast_convert_type(x, jnp.uint32) & jnp.uint32(0xFFFF0000)
    return lax.bitcast_convert_type(bits, F32)


def _bf16_round(x):
    bits = lax.bitcast_convert_type(x, jnp.uint32)
    bits = (bits + jnp.uint32(0x7FFF) + ((bits >> 16) & jnp.uint32(1))) & jnp.uint32(0xFFFF0000)
    return lax.bitcast_convert_type(bits, F32)


def _split3(x):
    hi = _bf16_trunc(x)
    mid = _bf16_trunc(x - hi)
    return hi, mid, x - hi - mid


def _moba_prep_body(q_ref, kmean_ref, qa_ref, *, n_blocks):
    half = AUG // 2
    kmean = kmean_ref[...]
    blk = lax.broadcasted_iota(jnp.int32, (n_blocks, MOBA_BLOCK), 0)
    lane = lax.broadcasted_iota(jnp.int32, (MOBA_BLOCK, AUG), 1)
    arow = lax.broadcasted_iota(jnp.int32, (SUBLANES, MOBA_BLOCK), 0)
    ones_rows = jnp.where(arow < 3, 1.0, 0.0).astype(F32)
    gap = jnp.zeros((half - n_blocks - SUBLANES, MOBA_BLOCK), F32)
    other = jnp.zeros((half, MOBA_BLOCK), F32)
    for j in range(q_ref.shape[0] // MOBA_BLOCK):
        sb = pl.program_id(1) * (q_ref.shape[0] // MOBA_BLOCK) + j
        rows = slice(j * MOBA_BLOCK, (j + 1) * MOBA_BLOCK)
        valid = blk < sb
        q = q_ref[rows, :] * (LOG2E * HEAD_DIM ** -0.5)
        for hh in range(2):
            first = hh == 0
            data = (lane < half) if first else (lane >= half)
            gate = lax.dot_general(kmean[:, hh * half:(hh + 1) * half], q_ref[rows, hh * half:(hh + 1) * half], _NT,
                                   precision=lax.Precision.HIGHEST, preferred_element_type=F32)
            sel = _topk_select(gate, valid, blk, n_blocks)
            bias_t = jnp.where(sel, 0.0, jnp.where(blk == sb, 0.0, NEG_BIAS)).astype(F32)
            extras_t = [bias_t, ones_rows, gap]
            aug = jnp.concatenate([other] + extras_t if first else extras_t + [other], axis=0).T
            qa_ref[0, hh, rows, :] = jnp.where(data, q, aug).astype(BF16)


def _key_extras(slopes, seq, n_blocks):
    shape = (slopes.shape[0], seq, AUG)
    pos = jnp.arange(seq, dtype=F32)
    hi, mid, lo = _split3(pos[None, :] * (slopes * LOG2E)[:, None])
    head = lax.broadcasted_iota(jnp.int32, shape, 0)
    key = lax.broadcasted_iota(jnp.int32, shape, 1)
    lane = lax.broadcasted_iota(jnp.int32, shape, 2)
    e = lane - jnp.where(head % 2 == 0, AUG // 2, 0)
    kx = jnp.where(e == key // MOBA_BLOCK, 1.0, 0.0)
    for i, part in enumerate((hi, mid, lo)):
        kx = jnp.where(e == n_blocks + i, part[:, :, None], kx)
    return kx


def moba_prep(q, kmean, n_seq, seq, n_heads):
    n_blocks = seq // MOBA_BLOCK
    assert n_blocks + SUBLANES <= AUG // 2 and n_blocks % SUBLANES == 0
    hpairs = n_heads // 2
    rows = PREP_BLOCKS * MOBA_BLOCK
    n_steps = seq // rows
    assert seq % rows == 0
    return pl.pallas_call(
        functools.partial(_moba_prep_body, n_blocks=n_blocks),
        grid=(hpairs, n_steps, n_seq),
        in_specs=[pl.BlockSpec((rows, LANES), lambda hp, sb, b: (b * n_steps + sb, hp)),
                  pl.BlockSpec((n_blocks, LANES), lambda hp, sb, b: (b, hp))],
        out_specs=pl.BlockSpec((1, 2, rows, AUG), lambda hp, sb, b: (b, hp, sb, 0)),
        out_shape=jax.ShapeDtypeStruct((n_seq, n_heads, seq, AUG), BF16),
        compiler_params=_params(("parallel", "parallel", "parallel")),
        name="moba_prep",
    )(q, kmean)


def _moba_attn_body(qa_ref, ka_ref, va_ref, o_ref, *, n_blocks):
    t = MOBA_BLOCK
    grp = ATTN_GROUP
    rows = grp * t
    kw = ATTN_KV_BLOCKS
    out_lane = lax.broadcasted_iota(jnp.int32, (rows, AUG), 1)

    def update(q, m, acc, hh, first_blk, n_blk, mask=None):
        width = n_blk * t
        start = pl.multiple_of(first_blk * t, t)
        s = lax.dot_general(q, ka_ref[0, hh, pl.ds(start, width), :], _NT, preferred_element_type=F32)
        if mask is not None:
            s = jnp.where(mask, s, -jnp.inf)
        m_new = jnp.maximum(m, jnp.max(s, axis=-1, keepdims=True))
        p = jnp.exp2(s - m_new)
        acc = jnp.exp2(m - m_new) * acc + jnp.dot(p.astype(BF16), va_ref[0, hh, pl.ds(start, width), :],
                                                  preferred_element_type=F32)
        return m_new, acc

    def group(g, carry):
        base = g * grp
        qs = [qa_ref[0, hh, pl.ds(pl.multiple_of(base * t, t), rows), :] for hh in range(2)]

        def past_tiles(kj, state):
            out = []
            for hh in range(2):
                out.extend(update(qs[hh], state[2 * hh], state[2 * hh + 1], hh, kj * kw, kw))
            return tuple(out)

        init = [jnp.full((rows, 1), -jnp.inf, F32), jnp.zeros((rows, AUG), F32)] * 2
        state = list(lax.fori_loop(0, base // kw, past_tiles, tuple(init)))

        rr = lax.broadcasted_iota(jnp.int32, (t, t), 0)
        cc = lax.broadcasted_iota(jnp.int32, (t, t), 1)
        causal = cc <= rr
        outs = []
        half = grp // 2
        for hh in range(2):
            den_lane = HEAD_DIM if hh == 0 else 0
            m_all, acc_all = state[2 * hh], state[2 * hh + 1]
            m_late, acc_late = update(qs[hh][half * t:], m_all[half * t:], acc_all[half * t:], hh, base, half)
            parts = []
            for a in range(grp):
                sl = slice(a * t, (a + 1) * t)
                if a < half:
                    m, acc, first = m_all[sl], acc_all[sl], 0
                else:
                    late = slice((a - half) * t, (a - half + 1) * t)
                    m, acc, first = m_late[late], acc_late[late], half
                for b in range(first, a + 1):
                    m, acc = update(qs[hh][sl], m, acc, hh, base + b, 1, causal if b == a else None)
                parts.append(acc / acc[:, den_lane:den_lane + 1])
            outs.append(jnp.concatenate(parts, axis=0))
        o_ref[pl.ds(pl.multiple_of(base * t, t), rows), :] = jnp.where(
            out_lane < HEAD_DIM, outs[0], outs[1]).astype(BF16)
        return carry

    lax.fori_loop(0, n_blocks // grp, group, 0)


def moba_attn(qa, ka, va):
    n_seq, n_heads, seq, _ = qa.shape
    n_blocks = seq // MOBA_BLOCK
    assert n_blocks % ATTN_GROUP == 0 and ATTN_GROUP % ATTN_KV_BLOCKS == 0
    spec = pl.BlockSpec((1, 2, seq, AUG), lambda b, hp: (b, hp, 0, 0))
    return pl.pallas_call(
        functools.partial(_moba_attn_body, n_blocks=n_blocks),
        grid=(n_seq, n_heads // 2),
        in_specs=[spec, spec, spec],
        out_specs=pl.BlockSpec((seq, LANES), lambda b, hp: (b, hp)),
        out_shape=jax.ShapeDtypeStruct((n_seq * seq, n_heads * HEAD_DIM), BF16),
        compiler_params=_params(("parallel", "parallel")),
        name="moba_attn",
    )(qa, ka, va)


def _sample_attn_body(pt_ref, rowinfo_ref, q_ref, kn_ref, vn_ref, *rest, n_pages, pps, n_new, n_heads):
    k_refs = rest[:pps]
    v_refs = rest[pps:2 * pps]
    o_ref = rest[2 * pps]
    qbd_ref, qbd32_ref, s_ref, ksum_ref, acc_ref, knt_ref, vnt_ref, bias_ref = rest[2 * pps + 1:]
    t = pl.program_id(1)
    tk = n_pages // pps
    page = k_refs[0].shape[3]
    d = q_ref.shape[1]
    past = n_pages * page
    n_blocks = past // MOBA_BLOCK
    ppb = MOBA_BLOCK // page
    n_rows = n_heads * n_new
    head_of_col = lax.broadcasted_iota(jnp.int32, (n_heads, d), 1) // HEAD_DIM
    head_of_row = lax.broadcasted_iota(jnp.int32, (n_heads, d), 0)
    own_head = head_of_col == head_of_row

    def key_major(rows_val):
        padded = jnp.concatenate([rows_val, jnp.zeros((LANES - rows_val.shape[0], d), F32)], axis=0)
        return padded.T

    @pl.when(t == 0)
    def _():
        q = q_ref[...]
        qbd = jnp.concatenate([jnp.where(own_head, jnp.broadcast_to(q[n:n + 1, :], (n_heads, d)), 0.0)
                               for n in range(n_new)], axis=0)
        qbd32_ref[0:n_rows, :] = qbd
        qbd32_ref[n_rows:, :] = jnp.zeros((LANES - n_rows, d), F32)
        qbd_ref[...] = (qbd * HEAD_DIM ** -0.5).astype(BF16)
        knt_ref[...] = key_major(kn_ref[...])
        vnt_ref[...] = key_major(vn_ref[...])
        acc_ref[...] = jnp.zeros_like(acc_ref)
        ksum_ref[...] = jnp.zeros_like(ksum_ref)

    @pl.when(t < tk)
    def _():
        qbd = qbd_ref[...]
        lane_i = lax.broadcasted_iota(jnp.int32, (1, LANES), 1)
        for bi in range(pps // ppb):
            ksum = jnp.zeros((d, page), F32)
            for pi in range(ppb):
                i = bi * ppb + pi
                kp = k_refs[i][0].reshape(d, page)
                s_ref[t * pps + i] = jnp.dot(qbd, kp.astype(BF16), preferred_element_type=F32)
                ksum = ksum + kp
            onehot = jnp.where(lane_i == t * (pps // ppb) + bi, 1.0, 0.0).astype(F32)
            ksum_ref[...] += jnp.sum(ksum, axis=1, keepdims=True) * onehot

    @pl.when(t == tk - 1)
    def _():
        s_ref[n_pages] = jnp.dot(qbd_ref[...], knt_ref[...].astype(BF16), preferred_element_type=F32)
        gate = jnp.dot(qbd32_ref[...], ksum_ref[...], precision=lax.Precision.HIGHEST,
                       preferred_element_type=F32) * (1.0 / MOBA_BLOCK)
        gate_t = gate.T[:n_blocks]
        blk = lax.broadcasted_iota(jnp.int32, (n_blocks, LANES), 0)
        sel_t = _topk_select(gate_t, blk >= 0, blk, n_blocks)
        sel = jnp.concatenate([jnp.where(sel_t, 1.0, 0.0).astype(F32),
                               jnp.zeros((LANES - n_blocks, LANES), F32)], axis=0).T[:n_rows]
        q_idx = rowinfo_ref[0]
        slope = rowinfo_ref[1]
        lane_f = lax.broadcasted_iota(jnp.int32, (n_rows, LANES), 1).astype(F32)

        for b in range(n_blocks):
            bias_ref[b] = jnp.broadcast_to(jnp.where(sel[:, b:b + 1] > 0.5, 0.0, -jnp.inf), (n_rows, LANES))

        def logits(p):
            dist = (past + q_idx) - (jnp.asarray(p * page, F32) + lane_f)
            return s_ref[p] - slope * dist + bias_ref[p // ppb]

        dist_new = q_idx - lane_f
        s_new = jnp.where((dist_new >= 0) & (lane_f < n_new), s_ref[n_pages] - slope * dist_new, -jnp.inf)
        m = jnp.max(lax.fori_loop(0, n_pages, lambda p, mv: jnp.maximum(mv, logits(p)), s_new, unroll=8),
                    axis=1, keepdims=True)

        def exp_step(p, den):
            e = jnp.exp(logits(p) - m)
            s_ref[p] = e
            return den + e

        e_new = jnp.exp(s_new - m)
        inv = 1.0 / jnp.sum(lax.fori_loop(0, n_pages, exp_step, e_new, unroll=8), axis=1, keepdims=True)
        s_ref[n_pages] = e_new * inv

        def norm_step(p, carry):
            s_ref[p] = s_ref[p] * inv
            return carry

        lax.fori_loop(0, n_pages, norm_step, 0, unroll=8)

    def weighted(p_idx, v_t):
        return lax.dot_general(s_ref[p_idx].astype(BF16), v_t.astype(BF16), _NT, preferred_element_type=F32)

    @pl.when(t >= tk)
    def _():
        acc = acc_ref[...]
        for i in range(pps):
            acc = acc + weighted((t - tk) * pps + i, v_refs[i][0].reshape(d, page))
        acc_ref[...] = acc

    @pl.when(t == 2 * tk - 1)
    def _():
        acc = acc_ref[...] + weighted(n_pages, vnt_ref[...])
        outs = [jnp.sum(jnp.where(own_head, acc[n * n_heads:(n + 1) * n_heads, :], 0.0), axis=0, keepdims=True)
                for n in range(n_new)]
        outs.append(jnp.zeros((SAMPLE_PAD - n_new, d), F32))
        o_ref[...] = jnp.concatenate(outs, axis=0).astype(BF16)


def sample_attn(page_table, rowinfo, q, k_new, v_new, cache_kt, cache_vt, n_new, pps=16):
    n_dec, n_pages = page_table.shape
    _, n_heads, _, page = cache_kt.shape
    d = n_heads * HEAD_DIM
    ppb = MOBA_BLOCK // page
    assert n_pages % pps == 0 and pps % ppb == 0 and page == LANES
    assert n_new <= SAMPLE_PAD and n_heads % SUBLANES == 0
    tk = n_pages // pps
    n_blocks = n_pages // ppb
    n_rows = n_heads * n_new
    assert n_blocks <= LANES and n_rows <= LANES

    def k_spec(i):
        return pl.BlockSpec((1, n_heads, HEAD_DIM, page),
                            lambda b, t, pt: (pt[b, jnp.minimum(t, tk - 1) * pps + i], 0, 0, 0))

    def v_spec(i):
        return pl.BlockSpec((1, n_heads, HEAD_DIM, page),
                            lambda b, t, pt: (pt[b, jnp.maximum(t - tk, 0) * pps + i], 0, 0, 0))

    row_spec = pl.BlockSpec((SAMPLE_PAD, d), lambda b, t, pt: (b, 0))
    grid_spec = pltpu.PrefetchScalarGridSpec(
        num_scalar_prefetch=1,
        grid=(n_dec, 2 * tk),
        in_specs=[pl.BlockSpec((2, n_rows, LANES), lambda b, t, pt: (0, 0, 0)), row_spec, row_spec, row_spec]
                 + [k_spec(i) for i in range(pps)] + [v_spec(i) for i in range(pps)],
        out_specs=row_spec,
        scratch_shapes=[pltpu.VMEM((n_rows, d), BF16),
                        pltpu.VMEM((LANES, d), F32),
                        pltpu.VMEM((n_pages + 1, n_rows, LANES), F32),
                        pltpu.VMEM((d, LANES), F32),
                        pltpu.VMEM((n_rows, d), F32),
                        pltpu.VMEM((d, LANES), F32),
                        pltpu.VMEM((d, LANES), F32),
                        pltpu.VMEM((n_blocks, n_rows, LANES), F32)],
    )
    return pl.pallas_call(
        functools.partial(_sample_attn_body, n_pages=n_pages, pps=pps, n_new=n_new, n_heads=n_heads),
        grid_spec=grid_spec,
        out_shape=jax.ShapeDtypeStruct((n_dec * SAMPLE_PAD, d), BF16),
        compiler_params=_params(("parallel", "arbitrary")),
        name="sample_attn",
    )(page_table, rowinfo, q, k_new, v_new, *([cache_kt] * pps), *([cache_vt] * pps))


def _ssd_body(z_ref, xs_ref, bc_ref, dt_ref, cinit_ref, sinit_ref, cw_ref, cb_ref, dtb_ref, alog_ref,
              dsk_ref, gn_ref, y_ref, cout_ref, sout_ref, ext_ref, xc_ref, h_ref, *, rows, valid, n_chunks):
    cl = SSD_CHUNK
    c = pl.program_id(1)
    d_inner = xs_ref.shape[1]
    n_heads = d_inner // SSM_HEAD_DIM
    hpg = n_heads // N_GROUPS
    gw = hpg * SSM_HEAD_DIM
    tail = SUBLANES

    @pl.when(c == 0)
    def _():
        ext_ref[0:tail, :] = cinit_ref[0]
        h_ref[...] = sinit_ref[0].reshape(h_ref.shape)
        if rows < cl:
            ext_ref[tail + rows:, :] = jnp.zeros((cl - rows, ext_ref.shape[1]), F32)

    @pl.when(c > 0)
    def _():
        ext_ref[0:tail, :] = ext_ref[cl:cl + tail, :]

    ext_ref[tail:tail + rows, 0:d_inner] = xs_ref[...]
    ext_ref[tail:tail + rows, d_inner:] = bc_ref[...]

    cw = cw_ref[...]
    conv = cb_ref[...] + ext_ref[tail:tail + cl, :] * cw[CONV_W - 1:CONV_W, :]
    for back in range(1, CONV_W):
        conv = conv + ext_ref[tail - back:tail - back + cl, :] * cw[CONV_W - 1 - back:CONV_W - back, :]
    xc_ref[...] = _silu(conv)

    def pad(v):
        if rows == cl:
            return v
        return jnp.concatenate([v, jnp.zeros((cl - rows, v.shape[1]), v.dtype)], axis=0)

    trow = lax.broadcasted_iota(jnp.int32, (cl, LANES), 0)
    dt_raw = pad(dt_ref[...]) + dtb_ref[...]
    dt = jnp.maximum(dt_raw, 0.0) + jnp.log1p(jnp.exp(-jnp.abs(dt_raw)))
    dt = jnp.where(trow < valid, dt, 0.0)
    a = -jnp.exp(alog_ref[...])
    tri_r = lax.broadcasted_iota(jnp.int32, (cl, cl), 0)
    tri_c = lax.broadcasted_iota(jnp.int32, (cl, cl), 1)
    causal = tri_c <= tri_r
    a_cs = jnp.dot(jnp.where(causal, 1.0, 0.0).astype(F32), dt * a, precision=lax.Precision.HIGHEST,
                   preferred_element_type=F32)
    a_cs_t = a_cs.T
    dt_t = dt.T
    z = pad(z_ref[...])
    from_start = jnp.exp(a_cs)
    to_end_all = jnp.exp(a_cs_t[:, cl - 1:cl] - a_cs_t) * dt_t

    for g in range(N_GROUPS):
        bg = xc_ref[:, d_inner + g * D_STATE:d_inner + (g + 1) * D_STATE].astype(BF16)
        cg = xc_ref[:, d_inner + (N_GROUPS + g) * D_STATE:d_inner + (N_GROUPS + g + 1) * D_STATE].astype(BF16)
        cb = lax.dot_general(cg, bg, _NT, preferred_element_type=F32)
        xg = xc_ref[:, g * gw:(g + 1) * gw]
        xg_t = xg.T
        h_old = h_ref[g * gw:(g + 1) * gw, :]
        y_off = lax.dot_general(cg, h_old.astype(BF16), _NT, preferred_element_type=F32)
        ys, xw_rows, h_scaled = [], [], []
        for e in range(hpg):
            h = g * hpg + e
            acs_col = a_cs[:, h:h + 1]
            acs_row = a_cs_t[h:h + 1, :]
            dt_row = dt_t[h:h + 1, :]
            decay = jnp.exp(jnp.where(causal, acs_col - acs_row, -jnp.inf))
            w = (cb * decay * dt_row).astype(BF16)
            xh = xg[:, e * SSM_HEAD_DIM:(e + 1) * SSM_HEAD_DIM]
            y_diag = jnp.dot(w, xh.astype(BF16), preferred_element_type=F32)
            ys.append(y_diag + y_off[:, e * SSM_HEAD_DIM:(e + 1) * SSM_HEAD_DIM] * from_start[:, h:h + 1])
            xw_rows.append(xg_t[e * SSM_HEAD_DIM:(e + 1) * SSM_HEAD_DIM, :] * to_end_all[h:h + 1, :])
            h_scaled.append(h_old[e * SSM_HEAD_DIM:(e + 1) * SSM_HEAD_DIM, :] * from_start[cl - 1:cl, h:h + 1])
        states = jnp.dot(jnp.concatenate(xw_rows, axis=0).astype(BF16), bg, preferred_element_type=F32)
        h_ref[g * gw:(g + 1) * gw, :] = jnp.concatenate(h_scaled, axis=0) + states
        y = jnp.concatenate(ys, axis=1) + dsk_ref[:, g * gw:(g + 1) * gw] * xg
        yz = y * _silu(z[:, g * gw:(g + 1) * gw])
        ms = jnp.mean(yz * yz, axis=-1, keepdims=True)
        yn = yz * lax.rsqrt(ms + EPS) * gn_ref[:, g * gw:(g + 1) * gw]
        y_ref[:, g * gw:(g + 1) * gw] = yn[:rows].astype(BF16)

    @pl.when(c == n_chunks - 1)
    def _():
        sout_ref[0] = h_ref[...].reshape(sout_ref.shape[1:])
        first = tail + ((valid - (CONV_W - 1)) // SUBLANES) * SUBLANES
        cout_ref[0] = ext_ref[first:first + SUBLANES, :]


def ssd_mixer(proj, conv_init, ssm_init, cw, cb, dtb, alog, dsk, gnorm, n_seq, rows, valid, d_inner):
    total_rows = proj.shape[0] // n_seq
    n_chunks = total_rows // rows
    assert rows == SSD_CHUNK or n_chunks == 1
    assert CONV_W - 1 <= valid <= rows
    n_heads = d_inner // SSM_HEAD_DIM
    conv_dim = d_inner + 2 * N_GROUPS * D_STATE
    assert conv_dim == 2 * d_inner

    def col(width, idx):
        return pl.BlockSpec((rows, width), lambda b, c: (b * n_chunks + c, idx))

    def const(shape):
        return pl.BlockSpec(shape, lambda b, c: (0,) * len(shape))

    outs = pl.pallas_call(
        functools.partial(_ssd_body, rows=rows, valid=valid, n_chunks=n_chunks),
        grid=(n_seq, n_chunks),
        in_specs=[col(d_inner, 0), col(d_inner, 1), col(d_inner, 2), col(LANES, 3 * d_inner // LANES),
                  pl.BlockSpec((1, SUBLANES, conv_dim), lambda b, c: (b, 0, 0)),
                  pl.BlockSpec((1, n_heads, SSM_HEAD_DIM, D_STATE), lambda b, c: (b, 0, 0, 0)),
                  const((CONV_W, conv_dim)), const((1, conv_dim)), const((1, LANES)), const((1, LANES)),
                  const((1, d_inner)), const((1, d_inner))],
        out_specs=[pl.BlockSpec((rows, d_inner), lambda b, c: (b * n_chunks + c, 0)),
                   pl.BlockSpec((1, SUBLANES, conv_dim), lambda b, c: (b, 0, 0)),
                   pl.BlockSpec((1, n_heads, SSM_HEAD_DIM, D_STATE), lambda b, c: (b, 0, 0, 0))],
        out_shape=[jax.ShapeDtypeStruct((proj.shape[0], d_inner), BF16),
                   jax.ShapeDtypeStruct((n_seq, SUBLANES, conv_dim), F32),
                   jax.ShapeDtypeStruct((n_seq, n_heads, SSM_HEAD_DIM, D_STATE), F32)],
        scratch_shapes=[pltpu.VMEM((SUBLANES + SSD_CHUNK, conv_dim), F32),
                        pltpu.VMEM((SSD_CHUNK, conv_dim), F32),
                        pltpu.VMEM((n_heads * SSM_HEAD_DIM, D_STATE), F32)],
        compiler_params=_params(("parallel", "arbitrary")),
        name="ssd_mixer",
    )(proj, proj, proj, proj, conv_init, ssm_init, cw, cb, dtb, alog, dsk, gnorm)
    return outs


def _pad_cols(w, n):
    return jnp.pad(w, ((0, 0), (0, n - w.shape[1])))


def _pad_lanes(v):
    return jnp.pad(v, (0, LANES - v.shape[0])).reshape(1, LANES)


def kernel(x_prompt, x_sample, cache_k, cache_v, state_conv, state_ssm, page_table, norm_mix, norm_ffn, w_qkv, w_o, w_in_ssm, conv_w, conv_b, dt_bias, a_log, d_skip, norm_ssm, w_out_ssm, w_gate_up, w_down, norm_final):
    n_seq, seq, d_model = x_prompt.shape
    n_dec, n_new, _ = x_sample.shape
    n_heads = d_model // HEAD_DIM
    d_inner = norm_ssm.shape[1]
    ssm_heads = d_inner // SSM_HEAD_DIM
    conv_dim = conv_w.shape[2]
    tm_p = 512
    tm_s = n_dec * SAMPLE_PAD

    hp = x_prompt.reshape(n_seq * seq, d_model)
    hs = jnp.pad(x_sample, ((0, 0), (0, SAMPLE_PAD - n_new), (0, 0))).reshape(tm_s, d_model)

    wqkv = w_qkv[0]
    wq_hi = _bf16_round(wqkv[:, :d_model])
    w_hi = jnp.concatenate([wq_hi, wqkv[:, d_model:]], axis=1).astype(BF16)
    wq_lo = (wqkv[:, :d_model] - wq_hi).astype(BF16)
    wkvt = wqkv[:, d_model:].T.astype(BF16)
    slopes = jnp.exp2(-8.0 * (jnp.arange(n_heads, dtype=F32) + 1.0) / n_heads)
    kx = _key_extras(slopes, seq, seq // MOBA_BLOCK)
    q_p, ka, va, kt_p, vt_p, xm_p = qkv_prompt(hp, norm_mix[0], w_hi, wq_lo, wkvt, kx, n_seq, seq, tm_p)
    q_s, k_s, v_s, _ = qkv_proj(hs, norm_mix[0], w_hi, wq_lo, tm_s)
    kmean = mm_f32(xm_p.reshape(n_seq * seq // MOBA_BLOCK, d_model), wqkv[:, d_model:2 * d_model])
    qa = moba_prep(q_p, kmean, n_seq, seq, n_heads)
    attn_p = moba_attn(qa, ka, va)
    rows = jnp.arange(n_heads * n_new)
    rowinfo = jnp.stack([(rows // n_heads).astype(F32), slopes[rows % n_heads]])
    rowinfo = jnp.broadcast_to(rowinfo[:, :, None], (2, n_heads * n_new, LANES))
    cache_kt = jnp.transpose(cache_k[0], (0, 2, 3, 1))
    cache_vt = jnp.transpose(cache_v[0], (0, 2, 3, 1))
    attn_s = sample_attn(page_table, rowinfo, q_s, k_s, v_s, cache_kt, cache_vt, n_new)
    wo = w_o[0].astype(BF16)
    hp = mm_res(attn_p, wo, hp, tm_p)
    hs = mm_res(attn_s, wo, hs, tm_s)

    k_prompt = jnp.transpose(kt_p, (0, 3, 1, 2))[None]
    v_prompt = jnp.transpose(vt_p, (0, 3, 1, 2))[None]
    k_sample = k_s.reshape(n_dec, SAMPLE_PAD, n_heads, HEAD_DIM)[None, :, :n_new]
    v_sample = v_s.reshape(n_dec, SAMPLE_PAD, n_heads, HEAD_DIM)[None, :, :n_new]

    def ffn(h, layer, tm, final_g=None):
        act = swiglu_gu(h, norm_ffn[layer], w_gate_up[layer].astype(BF16), tm, 256)
        return mm_res(act, w_down[layer].astype(BF16), h, tm, final_g)

    hp = ffn(hp, 0, tm_p)
    hs = ffn(hs, 0, tm_s)

    in_cols = d_inner + conv_dim + ssm_heads
    assert w_in_ssm.shape[2] == in_cols
    proj_w = ((in_cols + 1279) // 1280) * 1280
    w_in = _pad_cols(w_in_ssm[0], proj_w).astype(BF16)
    proj_p = norm_mm(hp, norm_mix[1], w_in, tm_p, 1280)
    proj_s = norm_mm(hs, norm_mix[1], w_in, tm_s, 1280)
    cb = conv_b[0].reshape(1, conv_dim)
    dtb = _pad_lanes(dt_bias[0])
    alog = _pad_lanes(a_log[0])
    dsk = jnp.repeat(d_skip[0], SSM_HEAD_DIM).reshape(1, d_inner)
    gn = norm_ssm[0].reshape(1, d_inner)
    zero_conv = jnp.zeros((n_seq, SUBLANES, conv_dim), F32)
    zero_ssm = jnp.zeros((n_seq, ssm_heads, SSM_HEAD_DIM, D_STATE), F32)
    y_p, conv_p, ssm_p = ssd_mixer(proj_p, zero_conv, zero_ssm, conv_w[0], cb, dtb, alog, dsk, gn,
                                   n_seq, SSD_CHUNK, SSD_CHUNK, d_inner)
    conv_init_s = jnp.pad(state_conv[0], ((0, 0), (SUBLANES - (CONV_W - 1), 0), (0, 0)))
    y_s, conv_s, ssm_s = ssd_mixer(proj_s, conv_init_s, state_ssm[0], conv_w[0], cb, dtb, alog, dsk, gn,
                                   n_dec, SAMPLE_PAD, n_new, d_inner)
    w_out = w_out_ssm[0].astype(BF16)
    hp = mm_res(y_p, w_out, hp, tm_p)
    hs = mm_res(y_s, w_out, hs, tm_s)
    y_prompt = ffn(hp, 1, tm_p, norm_final).reshape(n_seq, seq, d_model)
    y_sample = ffn(hs, 1, tm_s, norm_final).reshape(n_dec, SAMPLE_PAD, d_model)[:, :n_new]

    lo_p = SUBLANES - (CONV_W - 1)
    lo_s = (n_new - (CONV_W - 1)) % SUBLANES
    return (y_prompt, y_sample, k_prompt, v_prompt, k_sample, v_sample,
            conv_p[None, :, lo_p:lo_p + CONV_W - 1], ssm_p[None],
            conv_s[None, :, lo_s:lo_s + CONV_W - 1], ssm_s[None])
```

```python
import functools
import math

import jax
import jax.numpy as jnp
from jax import lax
from jax.experimental import pallas as pl
from jax.experimental.pallas import tpu as pltpu

F32 = jnp.float32
BF16 = jnp.bfloat16

EPS = 1e-5
HEAD_DIM = 64
MOBA_BLOCK = 256
MOBA_TOPK = 3
SSM_HEAD_DIM = 64
N_GROUPS = 8
D_STATE = 128
CONV_W = 4
SSD_CHUNK = 128
SSD_SEQS_PER_STEP = 2
PAGE_SLOTS = 4
SAMPLE_PAD = 16
AUG = 128
NEG_BIAS = -1e30
LOG2E = math.log2(math.e)
ATTN_GROUP = 4
ATTN_KV_BLOCKS = 4
PREP_BLOCKS = 8
LANES = 128
SUBLANES = 8
VMEM_LIMIT = 48 * 1024 * 1024
VMEM_LIMIT_BIG = 56 * 1024 * 1024

_NT = (((1,), (1,)), ((), ()))


def _silu(x):
    return x / (1.0 + jnp.exp(-x))


def _params(sem, vmem=VMEM_LIMIT):
    return pltpu.CompilerParams(dimension_semantics=sem, vmem_limit_bytes=vmem)


def _rms_bf16(x_ref, g_ref):
    x = x_ref[...]
    ms = jnp.mean(x * x, axis=-1, keepdims=True)
    return (x * lax.rsqrt(ms + EPS) * g_ref[...]).astype(BF16)


def _norm_mm_body(x_ref, g_ref, w_ref, o_ref, *, tn):
    xn = _rms_bf16(x_ref, g_ref)
    for c in range(w_ref.shape[1] // tn):
        o_ref[:, c * tn:(c + 1) * tn] = jnp.dot(xn, w_ref[:, c * tn:(c + 1) * tn], preferred_element_type=F32)


def norm_mm(x, g, w, tm, tn):
    m, d = x.shape
    n = w.shape[1]
    assert n % tn == 0
    return pl.pallas_call(
        functools.partial(_norm_mm_body, tn=tn),
        grid=(m // tm,),
        in_specs=[pl.BlockSpec((tm, d), lambda i: (i, 0)),
                  pl.BlockSpec((1, d), lambda i: (0, 0)),
                  pl.BlockSpec((d, n), lambda i: (0, 0), pipeline_mode=pl.Buffered(1))],
        out_specs=pl.BlockSpec((tm, n), lambda i: (i, 0)),
        out_shape=jax.ShapeDtypeStruct((m, n), F32),
        compiler_params=_params(("parallel",), VMEM_LIMIT_BIG),
        name="norm_mm",
    )(x, g.reshape(1, d), w)


def _swiglu_gu_body(x_ref, g_ref, w_ref, o_ref, *, tn):
    xn = _rms_bf16(x_ref, g_ref)
    dff = o_ref.shape[1]
    for c in range(dff // tn):
        gate = jnp.dot(xn, w_ref[:, c * tn:(c + 1) * tn], preferred_element_type=F32)
        up = jnp.dot(xn, w_ref[:, dff + c * tn:dff + (c + 1) * tn], preferred_element_type=F32)
        o_ref[:, c * tn:(c + 1) * tn] = (_silu(gate) * up).astype(BF16)


def swiglu_gu(x, g, w_gu, tm, tn):
    m, d = x.shape
    dff = w_gu.shape[1] // 2
    assert dff % tn == 0
    return pl.pallas_call(
        functools.partial(_swiglu_gu_body, tn=tn),
        grid=(m // tm,),
        in_specs=[pl.BlockSpec((tm, d), lambda i: (i, 0)),
                  pl.BlockSpec((1, d), lambda i: (0, 0)),
                  pl.BlockSpec((d, 2 * dff), lambda i: (0, 0), pipeline_mode=pl.Buffered(1))],
        out_specs=pl.BlockSpec((tm, dff), lambda i: (i, 0)),
        out_shape=jax.ShapeDtypeStruct((m, dff), BF16),
        compiler_params=_params(("parallel",)),
        name="swiglu_gu",
    )(x, g.reshape(1, d), w_gu)


def _mm_res_body(a_ref, w_ref, r_ref, o_ref):
    o_ref[...] = r_ref[...] + jnp.dot(a_ref[...], w_ref[...], preferred_element_type=F32)


def _mm_res_norm_body(a_ref, w_ref, r_ref, g_ref, o_ref):
    y = r_ref[...] + jnp.dot(a_ref[...], w_ref[...], preferred_element_type=F32)
    ms = jnp.mean(y * y, axis=-1, keepdims=True)
    o_ref[...] = y * lax.rsqrt(ms + EPS) * g_ref[...]


def mm_res(a, w, res, tm, final_g=None):
    m, k = a.shape
    n = w.shape[1]
    in_specs = [pl.BlockSpec((tm, k), lambda i: (i, 0)),
                pl.BlockSpec((k, n), lambda i: (0, 0)),
                pl.BlockSpec((tm, n), lambda i: (i, 0))]
    args = [a, w, res]
    body = _mm_res_body
    if final_g is not None:
        in_specs.append(pl.BlockSpec((1, n), lambda i: (0, 0)))
        args.append(final_g.reshape(1, n))
        body = _mm_res_norm_body
    return pl.pallas_call(
        body,
        grid=(m // tm,),
        in_specs=in_specs,
        out_specs=pl.BlockSpec((tm, n), lambda i: (i, 0)),
        out_shape=jax.ShapeDtypeStruct((m, n), F32),
        compiler_params=_params(("parallel",)),
        name="mm_res_norm" if final_g is not None else "mm_res",
    )(*args)


def _qkv_body(x_ref, g_ref, w_ref, wq_lo_ref, q_ref, k_ref, v_ref, xm_ref):
    d = x_ref.shape[1]
    x = x_ref[...]
    ms = jnp.mean(x * x, axis=-1, keepdims=True)
    xn = x * lax.rsqrt(ms + EPS) * g_ref[...]
    hi = xn.astype(BF16)
    lo = (xn - hi.astype(F32)).astype(BF16)
    wq = w_ref[:, 0:d]
    q_ref[...] = (jnp.dot(hi, wq, preferred_element_type=F32) + jnp.dot(lo, wq, preferred_element_type=F32)
                  + jnp.dot(hi, wq_lo_ref[...], preferred_element_type=F32))
    k_ref[...] = jnp.dot(hi, w_ref[:, d:2 * d], preferred_element_type=F32)
    v_ref[...] = jnp.dot(hi, w_ref[:, 2 * d:3 * d], preferred_element_type=F32)
    for blk in range(xm_ref.shape[0]):
        xm_ref[blk] = jnp.mean(xn[blk * MOBA_BLOCK:(blk + 1) * MOBA_BLOCK], axis=0, keepdims=True)


def qkv_proj(x, g, w, wq_lo, tm):
    m, d = x.shape
    assert tm % MOBA_BLOCK == 0
    row = pl.BlockSpec((tm, d), lambda i: (i, 0))
    out = jax.ShapeDtypeStruct((m, d), F32)
    return pl.pallas_call(
        _qkv_body,
        grid=(m // tm,),
        in_specs=[row, pl.BlockSpec((1, d), lambda i: (0, 0)),
                  pl.BlockSpec((d, 3 * d), lambda i: (0, 0), pipeline_mode=pl.Buffered(1)),
                  pl.BlockSpec((d, d), lambda i: (0, 0), pipeline_mode=pl.Buffered(1))],
        out_specs=[row, row, row, pl.BlockSpec((tm // MOBA_BLOCK, 1, d), lambda i: (i, 0, 0))],
        out_shape=[out, out, out, jax.ShapeDtypeStruct((m // MOBA_BLOCK, 1, d), F32)],
        compiler_params=_params(("parallel",)),
        name="qkv_proj",
    )(x, g.reshape(1, d), w, wq_lo)


def _qkv_prompt_body(x_ref, g_ref, w_ref, wq_lo_ref, wkvt_ref, kx_ref, q_ref, ka_ref, va_ref, kt_ref, vt_ref,
                     xm_ref):
    d = x_ref.shape[1]
    n_heads = ka_ref.shape[1]
    tm = x_ref.shape[0]
    half = AUG // 2
    x = x_ref[...]
    ms = jnp.mean(x * x, axis=-1, keepdims=True)
    xn = x * lax.rsqrt(ms + EPS) * g_ref[...]
    hi = xn.astype(BF16)
    lo = (xn - hi.astype(F32)).astype(BF16)
    wq = w_ref[:, 0:d]
    q_ref[...] = (jnp.dot(hi, wq, preferred_element_type=F32) + jnp.dot(lo, wq, preferred_element_type=F32)
                  + jnp.dot(hi, wq_lo_ref[...], preferred_element_type=F32))
    k = jnp.dot(hi, w_ref[:, d:2 * d], preferred_element_type=F32)
    v = jnp.dot(hi, w_ref[:, 2 * d:3 * d], preferred_element_type=F32)
    lane = lax.broadcasted_iota(jnp.int32, (tm, AUG), 1)
    for pair in range(n_heads // 2):
        kb = k[:, pair * AUG:(pair + 1) * AUG]
        vb = v[:, pair * AUG:(pair + 1) * AUG]
        for hh in range(2):
            h = 2 * pair + hh
            data = (lane < half) if hh == 0 else (lane >= half)
            ones_lane = half if hh == 0 else 0
            ka_ref[0, h] = jnp.where(data, kb, kx_ref[h]).astype(BF16)
            va_ref[0, h] = jnp.where(data, vb, jnp.where(lane == ones_lane, 1.0, 0.0)).astype(BF16)
    kt = lax.dot_general(wkvt_ref[0:d, :], hi, _NT, preferred_element_type=F32)
    vt = lax.dot_general(wkvt_ref[d:2 * d, :], hi, _NT, preferred_element_type=F32)
    kt_ref[0] = kt.reshape(n_heads, HEAD_DIM, tm)
    vt_ref[0] = vt.reshape(n_heads, HEAD_DIM, tm)
    for blk in range(xm_ref.shape[0]):
        xm_ref[blk] = jnp.mean(xn[blk * MOBA_BLOCK:(blk + 1) * MOBA_BLOCK], axis=0, keepdims=True)


def qkv_prompt(x, g, w, wq_lo, wkvt, kx, n_seq, seq, tm):
    m, d = x.shape
    n_heads = d // HEAD_DIM
    assert tm % MOBA_BLOCK == 0 and seq % tm == 0
    steps = seq // tm
    row = pl.BlockSpec((tm, d), lambda i: (i, 0))
    aug_spec = pl.BlockSpec((1, n_heads, tm, AUG), lambda i: (i // steps, 0, i % steps, 0))
    t_spec = pl.BlockSpec((1, n_heads, HEAD_DIM, tm), lambda i: (i // steps, 0, 0, i % steps))
    aug_shape = jax.ShapeDtypeStruct((n_seq, n_heads, seq, AUG), BF16)
    t_shape = jax.ShapeDtypeStruct((n_seq, n_heads, HEAD_DIM, seq), F32)

    def whole(shape):
        return pl.BlockSpec(shape, lambda i: (0,) * len(shape), pipeline_mode=pl.Buffered(1))

    return pl.pallas_call(
        _qkv_prompt_body,
        grid=(m // tm,),
        in_specs=[row, pl.BlockSpec((1, d), lambda i: (0, 0)), whole((d, 3 * d)), whole((d, d)), whole((2 * d, d)),
                  pl.BlockSpec((n_heads, tm, AUG), lambda i: (0, i % steps, 0))],
        out_specs=[row, aug_spec, aug_spec, t_spec, t_spec,
                   pl.BlockSpec((tm // MOBA_BLOCK, 1, d), lambda i: (i, 0, 0))],
        out_shape=[jax.ShapeDtypeStruct((m, d), F32), aug_shape, aug_shape, t_shape, t_shape,
                   jax.ShapeDtypeStruct((m // MOBA_BLOCK, 1, d), F32)],
        compiler_params=_params(("parallel",), VMEM_LIMIT_BIG),
        name="qkv_prompt",
    )(x, g.reshape(1, d), w, wq_lo, wkvt, kx)


def _mm_f32_body(a_ref, w_ref, o_ref):
    o_ref[...] = jnp.dot(a_ref[...], w_ref[...], precision=lax.Precision.HIGHEST, preferred_element_type=F32)


def mm_f32(a, w):
    m, k = a.shape
    n = w.shape[1]
    return pl.pallas_call(
        _mm_f32_body,
        grid=(1,),
        in_specs=[pl.BlockSpec((m, k), lambda i: (0, 0)), pl.BlockSpec((k, n), lambda i: (0, 0))],
        out_specs=pl.BlockSpec((m, n), lambda i: (0, 0)),
        out_shape=jax.ShapeDtypeStruct((m, n), F32),
        compiler_params=_params(("arbitrary",)),
        name="mm_f32",
    )(a, w)


def _topk_select(gate, valid, blk, n_blocks):
    tiles = n_blocks // SUBLANES

    def over_blocks(x, op):
        acc = x[0:SUBLANES]
        for i in range(1, tiles):
            acc = op(acc, x[i * SUBLANES:(i + 1) * SUBLANES])
        shift = SUBLANES // 2
        while shift:
            acc = op(acc, pltpu.roll(acc, shift, axis=0))
            shift //= 2
        return jnp.concatenate([acc] * tiles, axis=0) if tiles > 1 else acc

    g = jnp.where(valid, gate, -jnp.inf)
    taken = jnp.zeros(g.shape, jnp.int32)
    for _ in range(MOBA_TOPK):
        best = over_blocks(g, jnp.maximum)
        first = over_blocks(jnp.where(g == best, blk, n_blocks), jnp.minimum)
        pick = blk == first
        taken = jnp.where(pick, 1, taken)
        g = jnp.where(pick, -jnp.inf, g)
    return jnp.logical_and(valid, taken > 0)


def _bf16_trunc(x):
    bits = lax.bitcast_convert_type(x, jnp.uint32) & jnp.uint32(0xFFFF0000)
    return lax.bitcast_convert_type(bits, F32)


def _bf16_round(x):
    bits = lax.bitcast_convert_type(x, jnp.uint32)
    bits = (bits + jnp.uint32(0x7FFF) + ((bits >> 16) & jnp.uint32(1))) & jnp.uint32(0xFFFF0000)
    return lax.bitcast_convert_type(bits, F32)


def _split3(x):
    hi = _bf16_trunc(x)
    mid = _bf16_trunc(x - hi)
    return hi, mid, x - hi - mid


def _moba_prep_body(q_ref, kmean_ref, qa_ref, *, n_blocks):
    half = AUG // 2
    kmean = kmean_ref[...]
    blk = lax.broadcasted_iota(jnp.int32, (n_blocks, MOBA_BLOCK), 0)
    lane = lax.broadcasted_iota(jnp.int32, (MOBA_BLOCK, AUG), 1)
    arow = lax.broadcasted_iota(jnp.int32, (SUBLANES, MOBA_BLOCK), 0)
    ones_rows = jnp.where(arow < 3, 1.0, 0.0).astype(F32)
    gap = jnp.zeros((half - n_blocks - SUBLANES, MOBA_BLOCK), F32)
    other = jnp.zeros((half, MOBA_BLOCK), F32)
    for j in range(q_ref.shape[0] // MOBA_BLOCK):
        sb = pl.program_id(1) * (q_ref.shape[0] // MOBA_BLOCK) + j
        rows = slice(j * MOBA_BLOCK, (j + 1) * MOBA_BLOCK)
        valid = blk < sb
        q = q_ref[rows, :] * (LOG2E * HEAD_DIM ** -0.5)
        for hh in range(2):
            first = hh == 0
            data = (lane < half) if first else (lane >= half)
            gate = lax.dot_general(kmean[:, hh * half:(hh + 1) * half], q_ref[rows, hh * half:(hh + 1) * half], _NT,
                                   precision=lax.Precision.HIGHEST, preferred_element_type=F32)
            sel = _topk_select(gate, valid, blk, n_blocks)
            bias_t = jnp.where(sel, 0.0, jnp.where(blk == sb, 0.0, NEG_BIAS)).astype(F32)
            extras_t = [bias_t, ones_rows, gap]
            aug = jnp.concatenate([other] + extras_t if first else extras_t + [other], axis=0).T
            qa_ref[0, hh, rows, :] = jnp.where(data, q, aug).astype(BF16)


def _key_extras(slopes, seq, n_blocks):
    shape = (slopes.shape[0], seq, AUG)
    pos = jnp.arange(seq, dtype=F32)
    hi, mid, lo = _split3(pos[None, :] * (slopes * LOG2E)[:, None])
    head = lax.broadcasted_iota(jnp.int32, shape, 0)
    key = lax.broadcasted_iota(jnp.int32, shape, 1)
    lane = lax.broadcasted_iota(jnp.int32, shape, 2)
    e = lane - jnp.where(head % 2 == 0, AUG // 2, 0)
    kx = jnp.where(e == key // MOBA_BLOCK, 1.0, 0.0)
    for i, part in enumerate((hi, mid, lo)):
        kx = jnp.where(e == n_blocks + i, part[:, :, None], kx)
    return kx


def moba_prep(q, kmean, n_seq, seq, n_heads):
    n_blocks = seq // MOBA_BLOCK
    assert n_blocks + SUBLANES <= AUG // 2 and n_blocks % SUBLANES == 0
    hpairs = n_heads // 2
    rows = PREP_BLOCKS * MOBA_BLOCK
    n_steps = seq // rows
    assert seq % rows == 0
    return pl.pallas_call(
        functools.partial(_moba_prep_body, n_blocks=n_blocks),
        grid=(hpairs, n_steps, n_seq),
        in_specs=[pl.BlockSpec((rows, LANES), lambda hp, sb, b: (b * n_steps + sb, hp)),
                  pl.BlockSpec((n_blocks, LANES), lambda hp, sb, b: (b, hp))],
        out_specs=pl.BlockSpec((1, 2, rows, AUG), lambda hp, sb, b: (b, hp, sb, 0)),
        out_shape=jax.ShapeDtypeStruct((n_seq, n_heads, seq, AUG), BF16),
        compiler_params=_params(("parallel", "parallel", "parallel")),
        name="moba_prep",
    )(q, kmean)


def _moba_attn_body(qa_ref, ka_ref, va_ref, o_ref, *, n_blocks):
    t = MOBA_BLOCK
    grp = ATTN_GROUP
    rows = grp * t
    kw = ATTN_KV_BLOCKS
    out_lane = lax.broadcasted_iota(jnp.int32, (rows, AUG), 1)

    def update(q, m, acc, hh, first_blk, n_blk, mask=None):
        width = n_blk * t
        start = pl.multiple_of(first_blk * t, t)
        s = lax.dot_general(q, ka_ref[0, hh, pl.ds(start, width), :], _NT, preferred_element_type=F32)
        if mask is not None:
            s = jnp.where(mask, s, -jnp.inf)
        m_new = jnp.maximum(m, jnp.max(s, axis=-1, keepdims=True))
        p = jnp.exp2(s - m_new)
        acc = jnp.exp2(m - m_new) * acc + jnp.dot(p.astype(BF16), va_ref[0, hh, pl.ds(start, width), :],
                                                  preferred_element_type=F32)
        return m_new, acc

    def group(g, carry):
        base = g * grp
        qs = [qa_ref[0, hh, pl.ds(pl.multiple_of(base * t, t), rows), :] for hh in range(2)]

        def past_tiles(kj, state):
            out = []
            for hh in range(2):
                out.extend(update(qs[hh], state[2 * hh], state[2 * hh + 1], hh, kj * kw, kw))
            return tuple(out)

        init = [jnp.full((rows, 1), -jnp.inf, F32), jnp.zeros((rows, AUG), F32)] * 2
        state = list(lax.fori_loop(0, base // kw, past_tiles, tuple(init)))

        rr = lax.broadcasted_iota(jnp.int32, (t, t), 0)
        cc = lax.broadcasted_iota(jnp.int32, (t, t), 1)
        causal = cc <= rr

        def diagonal(q, m, acc, hh, blk0, n):
            if n == 1:
                return [update(q, m, acc, hh, blk0, 1, causal)]
            half = n // 2
            cut = half * t
            early = diagonal(q[:cut], m[:cut], acc[:cut], hh, blk0, half)
            m_late, acc_late = update(q[cut:], m[cut:], acc[cut:], hh, blk0, half)
            return early + diagonal(q[cut:], m_late, acc_late, hh, blk0 + half, n - half)

        outs = []
        for hh in range(2):
            den_lane = HEAD_DIM if hh == 0 else 0
            blocks = diagonal(qs[hh], state[2 * hh], state[2 * hh + 1], hh, base, grp)
            outs.append(jnp.concatenate([acc / acc[:, den_lane:den_lane + 1] for _, acc in blocks], axis=0))
        o_ref[pl.ds(pl.multiple_of(base * t, t), rows), :] = jnp.where(
            out_lane < HEAD_DIM, outs[0], outs[1]).astype(BF16)
        return carry

    lax.fori_loop(0, n_blocks // grp, group, 0)


def moba_attn(qa, ka, va):
    n_seq, n_heads, seq, _ = qa.shape
    n_blocks = seq // MOBA_BLOCK
    assert n_blocks % ATTN_GROUP == 0 and ATTN_GROUP % ATTN_KV_BLOCKS == 0
    spec = pl.BlockSpec((1, 2, seq, AUG), lambda b, hp: (b, hp, 0, 0))
    return pl.pallas_call(
        functools.partial(_moba_attn_body, n_blocks=n_blocks),
        grid=(n_seq, n_heads // 2),
        in_specs=[spec, spec, spec],
        out_specs=pl.BlockSpec((seq, LANES), lambda b, hp: (b, hp)),
        out_shape=jax.ShapeDtypeStruct((n_seq * seq, n_heads * HEAD_DIM), BF16),
        compiler_params=_params(("parallel", "parallel")),
        name="moba_attn",
    )(qa, ka, va)


def _sample_attn_body(pt_ref, rowinfo_ref, q_ref, kn_ref, vn_ref, ck_ref, cv_ref, o_ref, qbd_ref, qbd32_ref, s_ref,
                      ksum_ref, acc_ref, knt_ref, vnt_ref, bias_ref, ring_ref, sem_ref, *, n_pages, pps, n_new,
                      n_heads):
    b = pl.program_id(0)
    t = pl.program_id(1)
    tk = n_pages // pps
    page = ck_ref.shape[3]
    d = q_ref.shape[1]
    past = n_pages * page
    n_blocks = past // MOBA_BLOCK
    ppb = MOBA_BLOCK // page
    n_rows = n_heads * n_new
    steps = 2 * tk
    total = pl.num_programs(0) * steps
    step = b * steps + t
    slot = step % PAGE_SLOTS

    def page_copy(src_ref, pool_page, to_slot, i):
        return pltpu.make_async_copy(src_ref.at[pool_page], ring_ref.at[to_slot, i], sem_ref.at[to_slot])

    def fetch(at):
        seq, tt, to_slot = at // steps, at % steps, at % PAGE_SLOTS

        @pl.when(tt < tk)
        def _():
            for i in range(pps):
                page_copy(ck_ref, pt_ref[seq, tt * pps + i], to_slot, i).start()

        @pl.when(tt >= tk)
        def _():
            for i in range(pps):
                page_copy(cv_ref, pt_ref[seq, (tt - tk) * pps + i], to_slot, i).start()

    @pl.when(step == 0)
    def _():
        for ahead in range(PAGE_SLOTS):
            fetch(ahead)

    @pl.when(jnp.logical_and(step > 0, step + PAGE_SLOTS - 1 < total))
    def _():
        fetch(step + PAGE_SLOTS - 1)

    for i in range(pps):
        page_copy(ck_ref, 0, slot, i).wait()

    head_of_col = lax.broadcasted_iota(jnp.int32, (n_heads, d), 1) // HEAD_DIM
    head_of_row = lax.broadcasted_iota(jnp.int32, (n_heads, d), 0)
    own_head = head_of_col == head_of_row

    def key_major(rows_val):
        padded = jnp.concatenate([rows_val, jnp.zeros((LANES - rows_val.shape[0], d), F32)], axis=0)
        return padded.T

    @pl.when(t == 0)
    def _():
        q = q_ref[...]
        qbd = jnp.concatenate([jnp.where(own_head, jnp.broadcast_to(q[n:n + 1, :], (n_heads, d)), 0.0)
                               for n in range(n_new)], axis=0)
        qbd32_ref[0:n_rows, :] = qbd
        qbd32_ref[n_rows:, :] = jnp.zeros((LANES - n_rows, d), F32)
        qbd_ref[...] = (qbd * HEAD_DIM ** -0.5).astype(BF16)
        knt_ref[...] = key_major(kn_ref[...])
        vnt_ref[...] = key_major(vn_ref[...])
        acc_ref[...] = jnp.zeros_like(acc_ref)
        ksum_ref[...] = jnp.zeros_like(ksum_ref)

    @pl.when(t < tk)
    def _():
        qbd = qbd_ref[...]
        lane_i = lax.broadcasted_iota(jnp.int32, (1, LANES), 1)
        for bi in range(pps // ppb):
            ksum = jnp.zeros((d, page), F32)
            for pi in range(ppb):
                i = bi * ppb + pi
                kp = ring_ref[slot, i].reshape(d, page)
                s_ref[t * pps + i] = jnp.dot(qbd, kp.astype(BF16), preferred_element_type=F32)
                ksum = ksum + kp
            onehot = jnp.where(lane_i == t * (pps // ppb) + bi, 1.0, 0.0).astype(F32)
            ksum_ref[...] += jnp.sum(ksum, axis=1, keepdims=True) * onehot

    @pl.when(t == tk - 1)
    def _():
        s_ref[n_pages] = jnp.dot(qbd_ref[...], knt_ref[...].astype(BF16), preferred_element_type=F32)
        gate = jnp.dot(qbd32_ref[...], ksum_ref[...], precision=lax.Precision.HIGHEST,
                       preferred_element_type=F32) * (1.0 / MOBA_BLOCK)
        gate_t = gate.T[:n_blocks]
        blk = lax.broadcasted_iota(jnp.int32, (n_blocks, LANES), 0)
        sel_t = _topk_select(gate_t, blk >= 0, blk, n_blocks)
        sel = jnp.concatenate([jnp.where(sel_t, 1.0, 0.0).astype(F32),
                               jnp.zeros((LANES - n_blocks, LANES), F32)], axis=0).T[:n_rows]
        q_idx = rowinfo_ref[0]
        slope = rowinfo_ref[1]
        lane_f = lax.broadcasted_iota(jnp.int32, (n_rows, LANES), 1).astype(F32)

        for b in range(n_blocks):
            bias_ref[b] = jnp.broadcast_to(jnp.where(sel[:, b:b + 1] > 0.5, 0.0, -jnp.inf), (n_rows, LANES))

        def logits(p):
            dist = (past + q_idx) - (jnp.asarray(p * page, F32) + lane_f)
            return s_ref[p] - slope * dist + bias_ref[p // ppb]

        dist_new = q_idx - lane_f
        s_new = jnp.where((dist_new >= 0) & (lane_f < n_new), s_ref[n_pages] - slope * dist_new, -jnp.inf)
        m = jnp.max(lax.fori_loop(0, n_pages, lambda p, mv: jnp.maximum(mv, logits(p)), s_new, unroll=8),
                    axis=1, keepdims=True)

        def exp_step(p, den):
            e = jnp.exp(logits(p) - m)
            s_ref[p] = e
            return den + e

        e_new = jnp.exp(s_new - m)
        inv = 1.0 / jnp.sum(lax.fori_loop(0, n_pages, exp_step, e_new, unroll=8), axis=1, keepdims=True)
        s_ref[n_pages] = e_new * inv

        def norm_step(p, carry):
            s_ref[p] = s_ref[p] * inv
            return carry

        lax.fori_loop(0, n_pages, norm_step, 0, unroll=8)

    def weighted(p_idx, v_t):
        return lax.dot_general(s_ref[p_idx].astype(BF16), v_t.astype(BF16), _NT, preferred_element_type=F32)

    @pl.when(t >= tk)
    def _():
        acc = acc_ref[...]
        for i in range(pps):
            acc = acc + weighted((t - tk) * pps + i, ring_ref[slot, i].reshape(d, page))
        acc_ref[...] = acc

    @pl.when(t == 2 * tk - 1)
    def _():
        acc = acc_ref[...] + weighted(n_pages, vnt_ref[...])
        outs = [jnp.sum(jnp.where(own_head, acc[n * n_heads:(n + 1) * n_heads, :], 0.0), axis=0, keepdims=True)
                for n in range(n_new)]
        outs.append(jnp.zeros((SAMPLE_PAD - n_new, d), F32))
        o_ref[...] = jnp.concatenate(outs, axis=0).astype(BF16)


def sample_attn(page_table, rowinfo, q, k_new, v_new, cache_kt, cache_vt, n_new, pps=8):
    n_dec, n_pages = page_table.shape
    _, n_heads, _, page = cache_kt.shape
    d = n_heads * HEAD_DIM
    ppb = MOBA_BLOCK // page
    assert n_pages % pps == 0 and pps % ppb == 0 and page == LANES
    assert n_new <= SAMPLE_PAD and n_heads % SUBLANES == 0
    tk = n_pages // pps
    n_blocks = n_pages // ppb
    n_rows = n_heads * n_new
    assert n_blocks <= LANES and n_rows <= LANES and n_dec * 2 * tk >= PAGE_SLOTS

    row_spec = pl.BlockSpec((SAMPLE_PAD, d), lambda b, t, pt: (b, 0))
    pool_spec = pl.BlockSpec(memory_space=pl.ANY)
    grid_spec = pltpu.PrefetchScalarGridSpec(
        num_scalar_prefetch=1,
        grid=(n_dec, 2 * tk),
        in_specs=[pl.BlockSpec((2, n_rows, LANES), lambda b, t, pt: (0, 0, 0)), row_spec, row_spec, row_spec,
                  pool_spec, pool_spec],
        out_specs=row_spec,
        scratch_shapes=[pltpu.VMEM((n_rows, d), BF16),
                        pltpu.VMEM((LANES, d), F32),
                        pltpu.VMEM((n_pages + 1, n_rows, LANES), F32),
                        pltpu.VMEM((d, LANES), F32),
                        pltpu.VMEM((n_rows, d), F32),
                        pltpu.VMEM((d, LANES), F32),
                        pltpu.VMEM((d, LANES), F32),
                        pltpu.VMEM((n_blocks, n_rows, LANES), F32),
                        pltpu.VMEM((PAGE_SLOTS, pps, n_heads, HEAD_DIM, page), F32),
                        pltpu.SemaphoreType.DMA((PAGE_SLOTS,))],
    )
    return pl.pallas_call(
        functools.partial(_sample_attn_body, n_pages=n_pages, pps=pps, n_new=n_new, n_heads=n_heads),
        grid_spec=grid_spec,
        out_shape=jax.ShapeDtypeStruct((n_dec * SAMPLE_PAD, d), BF16),
        compiler_params=_params(("arbitrary", "arbitrary")),
        name="sample_attn",
    )(page_table, rowinfo, q, k_new, v_new, cache_kt, cache_vt)


def _ssd_body(z_ref, xs_ref, bc_ref, dt_ref, cinit_ref, sinit_ref, cw_ref, cb_ref, dtb_ref, alog_ref,
              dsk_ref, gn_ref, y_ref, cout_ref, sout_ref, ext_ref, xc_ref, h_ref, **static):
    for phase in ("start", "main", "end"):
        for p in range(z_ref.shape[0]):
            _ssd_chunk(z_ref.at[p], xs_ref.at[p], bc_ref.at[p], dt_ref.at[p], cinit_ref.at[p], sinit_ref.at[p],
                       cw_ref, cb_ref, dtb_ref, alog_ref, dsk_ref, gn_ref, y_ref.at[p], cout_ref.at[p],
                       sout_ref.at[p], ext_ref.at[p], xc_ref.at[p], h_ref.at[p], phase=phase, **static)


def _ssd_chunk(z_ref, xs_ref, bc_ref, dt_ref, cinit_ref, sinit_ref, cw_ref, cb_ref, dtb_ref, alog_ref,
               dsk_ref, gn_ref, y_ref, cout_ref, sout_ref, ext_ref, xc_ref, h_ref, *, rows, valid, n_chunks,
               phase):
    cl = SSD_CHUNK
    c = pl.program_id(1)
    d_inner = xs_ref.shape[1]
    n_heads = d_inner // SSM_HEAD_DIM
    hpg = n_heads // N_GROUPS
    gw = hpg * SSM_HEAD_DIM
    tail = SUBLANES

    if phase == "start":
        @pl.when(c == 0)
        def _():
            ext_ref[0:tail, :] = cinit_ref[...]
            h_ref[...] = sinit_ref[...].reshape(h_ref.shape)
            if rows < cl:
                ext_ref[tail + rows:, :] = jnp.zeros((cl - rows, ext_ref.shape[1]), F32)

        @pl.when(c > 0)
        def _():
            ext_ref[0:tail, :] = ext_ref[cl:cl + tail, :]

        return

    if phase == "end":
        @pl.when(c == n_chunks - 1)
        def _():
            sout_ref[...] = h_ref[...].reshape(sout_ref.shape)
            first = tail + ((valid - (CONV_W - 1)) // SUBLANES) * SUBLANES
            cout_ref[...] = ext_ref[first:first + SUBLANES, :]

        return

    ext_ref[tail:tail + rows, 0:d_inner] = xs_ref[...]
    ext_ref[tail:tail + rows, d_inner:] = bc_ref[...]

    cw = cw_ref[...]
    conv = cb_ref[...] + ext_ref[tail:tail + cl, :] * cw[CONV_W - 1:CONV_W, :]
    for back in range(1, CONV_W):
        conv = conv + ext_ref[tail - back:tail - back + cl, :] * cw[CONV_W - 1 - back:CONV_W - back, :]
    xc_ref[...] = _silu(conv)

    def pad(v):
        if rows == cl:
            return v
        return jnp.concatenate([v, jnp.zeros((cl - rows, v.shape[1]), v.dtype)], axis=0)

    trow = lax.broadcasted_iota(jnp.int32, (cl, LANES), 0)
    dt_raw = pad(dt_ref[...]) + dtb_ref[...]
    dt = jnp.maximum(dt_raw, 0.0) + jnp.log1p(jnp.exp(-jnp.abs(dt_raw)))
    dt = jnp.where(trow < valid, dt, 0.0)
    a = -jnp.exp(alog_ref[...])
    tri_r = lax.broadcasted_iota(jnp.int32, (cl, cl), 0)
    tri_c = lax.broadcasted_iota(jnp.int32, (cl, cl), 1)
    causal = tri_c <= tri_r
    a_cs = jnp.dot(jnp.where(causal, 1.0, 0.0).astype(F32), dt * a, precision=lax.Precision.HIGHEST,
                   preferred_element_type=F32)
    a_cs_t = a_cs.T
    dt_t = dt.T
    z = pad(z_ref[...])
    from_start = jnp.exp(a_cs)
    to_end_all = jnp.exp(a_cs_t[:, cl - 1:cl] - a_cs_t) * dt_t

    for g in range(N_GROUPS):
        bg = xc_ref[:, d_inner + g * D_STATE:d_inner + (g + 1) * D_STATE].astype(BF16)
        cg = xc_ref[:, d_inner + (N_GROUPS + g) * D_STATE:d_inner + (N_GROUPS + g + 1) * D_STATE].astype(BF16)
        cb = lax.dot_general(cg, bg, _NT, preferred_element_type=F32)
        xg = xc_ref[:, g * gw:(g + 1) * gw]
        xg_t = xg.T
        h_old = h_ref[g * gw:(g + 1) * gw, :]
        y_off = lax.dot_general(cg, h_old.astype(BF16), _NT, preferred_element_type=F32)
        ys, xw_rows, h_scaled = [], [], []
        for e in range(hpg):
            h = g * hpg + e
            acs_col = a_cs[:, h:h + 1]
            acs_row = a_cs_t[h:h + 1, :]
            dt_row = dt_t[h:h + 1, :]
            decay = jnp.exp(jnp.where(causal, acs_col - acs_row, -jnp.inf))
            w = (cb * decay * dt_row).astype(BF16)
            xh = xg[:, e * SSM_HEAD_DIM:(e + 1) * SSM_HEAD_DIM]
            y_diag = jnp.dot(w, xh.astype(BF16), preferred_element_type=F32)
            ys.append(y_diag + y_off[:, e * SSM_HEAD_DIM:(e + 1) * SSM_HEAD_DIM] * from_start[:, h:h + 1])
            xw_rows.append(xg_t[e * SSM_HEAD_DIM:(e + 1) * SSM_HEAD_DIM, :] * to_end_all[h:h + 1, :])
            h_scaled.append(h_old[e * SSM_HEAD_DIM:(e + 1) * SSM_HEAD_DIM, :] * from_start[cl - 1:cl, h:h + 1])
        states = jnp.dot(jnp.concatenate(xw_rows, axis=0).astype(BF16), bg, preferred_element_type=F32)
        h_ref[g * gw:(g + 1) * gw, :] = jnp.concatenate(h_scaled, axis=0) + states
        y = jnp.concatenate(ys, axis=1) + dsk_ref[:, g * gw:(g + 1) * gw] * xg
        yz = y * _silu(z[:, g * gw:(g + 1) * gw])
        ms = jnp.mean(yz * yz, axis=-1, keepdims=True)
        yn = yz * lax.rsqrt(ms + EPS) * gn_ref[:, g * gw:(g + 1) * gw]
        y_ref[:, g * gw:(g + 1) * gw] = yn[:rows].astype(BF16)


def ssd_mixer(proj, conv_init, ssm_init, cw, cb, dtb, alog, dsk, gnorm, n_seq, rows, valid, d_inner):
    total_rows = proj.shape[0] // n_seq
    n_chunks = total_rows // rows
    assert rows == SSD_CHUNK or n_chunks == 1
    assert CONV_W - 1 <= valid <= rows
    n_heads = d_inner // SSM_HEAD_DIM
    conv_dim = d_inner + 2 * N_GROUPS * D_STATE
    assert conv_dim == 2 * d_inner

    par = SSD_SEQS_PER_STEP
    assert n_seq % par == 0
    proj3 = proj.reshape(n_seq, total_rows, proj.shape[1])

    def col(width, idx):
        return pl.BlockSpec((par, rows, width), lambda b, c: (b, c, idx))

    def const(shape):
        return pl.BlockSpec(shape, lambda b, c: (0,) * len(shape))

    def per_seq(shape):
        return pl.BlockSpec((par,) + shape, lambda b, c: (b,) + (0,) * len(shape))

    y, conv_out, ssm_out = pl.pallas_call(
        functools.partial(_ssd_body, rows=rows, valid=valid, n_chunks=n_chunks),
        grid=(n_seq // par, n_chunks),
        in_specs=[col(d_inner, 0), col(d_inner, 1), col(d_inner, 2), col(LANES, 3 * d_inner // LANES),
                  per_seq((SUBLANES, conv_dim)), per_seq((n_heads, SSM_HEAD_DIM, D_STATE)),
                  const((CONV_W, conv_dim)), const((1, conv_dim)), const((1, LANES)), const((1, LANES)),
                  const((1, d_inner)), const((1, d_inner))],
        out_specs=[col(d_inner, 0), per_seq((SUBLANES, conv_dim)), per_seq((n_heads, SSM_HEAD_DIM, D_STATE))],
        out_shape=[jax.ShapeDtypeStruct((n_seq, total_rows, d_inner), BF16),
                   jax.ShapeDtypeStruct((n_seq, SUBLANES, conv_dim), F32),
                   jax.ShapeDtypeStruct((n_seq, n_heads, SSM_HEAD_DIM, D_STATE), F32)],
        scratch_shapes=[pltpu.VMEM((par, SUBLANES + SSD_CHUNK, conv_dim), F32),
                        pltpu.VMEM((par, SSD_CHUNK, conv_dim), F32),
                        pltpu.VMEM((par, n_heads * SSM_HEAD_DIM, D_STATE), F32)],
        compiler_params=_params(("parallel", "arbitrary")),
        name="ssd_mixer",
    )(proj3, proj3, proj3, proj3, conv_init, ssm_init, cw, cb, dtb, alog, dsk, gnorm)
    return y.reshape(n_seq * total_rows, d_inner), conv_out, ssm_out


def _pad_cols(w, n):
    return jnp.pad(w, ((0, 0), (0, n - w.shape[1])))


def _pad_lanes(v):
    return jnp.pad(v, (0, LANES - v.shape[0])).reshape(1, LANES)


def kernel(x_prompt, x_sample, cache_k, cache_v, state_conv, state_ssm, page_table, norm_mix, norm_ffn, w_qkv, w_o, w_in_ssm, conv_w, conv_b, dt_bias, a_log, d_skip, norm_ssm, w_out_ssm, w_gate_up, w_down, norm_final):
    n_seq, seq, d_model = x_prompt.shape
    n_dec, n_new, _ = x_sample.shape
    n_heads = d_model // HEAD_DIM
    d_inner = norm_ssm.shape[1]
    ssm_heads = d_inner // SSM_HEAD_DIM
    conv_dim = conv_w.shape[2]
    tm_p = 512
    tm_s = n_dec * SAMPLE_PAD

    hp = x_prompt.reshape(n_seq * seq, d_model)
    hs = jnp.pad(x_sample, ((0, 0), (0, SAMPLE_PAD - n_new), (0, 0))).reshape(tm_s, d_model)

    wqkv = w_qkv[0]
    wq_hi = _bf16_round(wqkv[:, :d_model])
    w_hi = jnp.concatenate([wq_hi, wqkv[:, d_model:]], axis=1).astype(BF16)
    wq_lo = (wqkv[:, :d_model] - wq_hi).astype(BF16)
    wkvt = wqkv[:, d_model:].T.astype(BF16)
    slopes = jnp.exp2(-8.0 * (jnp.arange(n_heads, dtype=F32) + 1.0) / n_heads)
    kx = _key_extras(slopes, seq, seq // MOBA_BLOCK)
    q_p, ka, va, kt_p, vt_p, xm_p = qkv_prompt(hp, norm_mix[0], w_hi, wq_lo, wkvt, kx, n_seq, seq, tm_p)
    q_s, k_s, v_s, _ = qkv_proj(hs, norm_mix[0], w_hi, wq_lo, tm_s)
    kmean = mm_f32(xm_p.reshape(n_seq * seq // MOBA_BLOCK, d_model), wqkv[:, d_model:2 * d_model])
    qa = moba_prep(q_p, kmean, n_seq, seq, n_heads)
    attn_p = moba_attn(qa, ka, va)
    rows = jnp.arange(n_heads * n_new)
    rowinfo = jnp.stack([(rows // n_heads).astype(F32), slopes[rows % n_heads]])
    rowinfo = jnp.broadcast_to(rowinfo[:, :, None], (2, n_heads * n_new, LANES))
    cache_kt = jnp.transpose(cache_k[0], (0, 2, 3, 1))
    cache_vt = jnp.transpose(cache_v[0], (0, 2, 3, 1))
    attn_s = sample_attn(page_table, rowinfo, q_s, k_s, v_s, cache_kt, cache_vt, n_new)
    wo = w_o[0].astype(BF16)
    hp = mm_res(attn_p, wo, hp, tm_p)
    hs = mm_res(attn_s, wo, hs, tm_s)

    k_prompt = jnp.transpose(kt_p, (0, 3, 1, 2))[None]
    v_prompt = jnp.transpose(vt_p, (0, 3, 1, 2))[None]
    k_sample = k_s.reshape(n_dec, SAMPLE_PAD, n_heads, HEAD_DIM)[None, :, :n_new]
    v_sample = v_s.reshape(n_dec, SAMPLE_PAD, n_heads, HEAD_DIM)[None, :, :n_new]

    def ffn(h, layer, tm, final_g=None):
        act = swiglu_gu(h, norm_ffn[layer], w_gate_up[layer].astype(BF16), tm, 256)
        return mm_res(act, w_down[layer].astype(BF16), h, tm, final_g)

    hp = ffn(hp, 0, tm_p)
    hs = ffn(hs, 0, tm_s)

    in_cols = d_inner + conv_dim + ssm_heads
    assert w_in_ssm.shape[2] == in_cols
    proj_w = ((in_cols + 1279) // 1280) * 1280
    w_in = _pad_cols(w_in_ssm[0], proj_w).astype(BF16)
    proj_p = norm_mm(hp, norm_mix[1], w_in, tm_p, 1280)
    proj_s = norm_mm(hs, norm_mix[1], w_in, tm_s, 1280)
    cb = conv_b[0].reshape(1, conv_dim)
    dtb = _pad_lanes(dt_bias[0])
    alog = _pad_lanes(a_log[0])
    dsk = jnp.repeat(d_skip[0], SSM_HEAD_DIM).reshape(1, d_inner)
    gn = norm_ssm[0].reshape(1, d_inner)
    zero_conv = jnp.zeros((n_seq, SUBLANES, conv_dim), F32)
    zero_ssm = jnp.zeros((n_seq, ssm_heads, SSM_HEAD_DIM, D_STATE), F32)
    y_p, conv_p, ssm_p = ssd_mixer(proj_p, zero_conv, zero_ssm, conv_w[0], cb, dtb, alog, dsk, gn,
                                   n_seq, SSD_CHUNK, SSD_CHUNK, d_inner)
    conv_init_s = jnp.pad(state_conv[0], ((0, 0), (SUBLANES - (CONV_W - 1), 0), (0, 0)))
    y_s, conv_s, ssm_s = ssd_mixer(proj_s, conv_init_s, state_ssm[0], conv_w[0], cb, dtb, alog, dsk, gn,
                                   n_dec, SAMPLE_PAD, n_new, d_inner)
    w_out = w_out_ssm[0].astype(BF16)
    hp = mm_res(y_p, w_out, hp, tm_p)
    hs = mm_res(y_s, w_out, hs, tm_s)
    y_prompt = ffn(hp, 1, tm_p, norm_final).reshape(n_seq, seq, d_model)
    y_sample = ffn(hs, 1, tm_s, norm_final).reshape(n_dec, SAMPLE_PAD, d_model)[:, :n_new]

    lo_p = SUBLANES - (CONV_W - 1)
    lo_s = (n_new - (CONV_W - 1)) % SUBLANES
    return (y_prompt, y_sample, k_prompt, v_prompt, k_sample, v_sample,
            conv_p[None, :, lo_p:lo_p + CONV_W - 1], ssm_p[None],
            conv_s[None, :, lo_s:lo_s + CONV_W - 1], ssm_s[None])
```

```python
import functools
import math

import jax
import jax.numpy as jnp
from jax import lax
from jax.experimental import pallas as pl
from jax.experimental.pallas import tpu as pltpu

F32 = jnp.float32
BF16 = jnp.bfloat16

EPS = 1e-5
HEAD_DIM = 64
MOBA_BLOCK = 256
MOBA_TOPK = 3
SSM_HEAD_DIM = 64
N_GROUPS = 8
D_STATE = 128
CONV_W = 4
SSD_CHUNK = 128
SSD_SEQS_PER_STEP = 2
PAGE_SLOTS = 4
SAMPLE_PAD = 16
AUG = 128
NEG_BIAS = -1e30
LOG2E = math.log2(math.e)
ATTN_GROUP = 4
ATTN_KV_BLOCKS = 4
PREP_BLOCKS = 8
LANES = 128
SUBLANES = 8
VMEM_LIMIT = 48 * 1024 * 1024
VMEM_LIMIT_BIG = 56 * 1024 * 1024

_NT = (((1,), (1,)), ((), ()))


def _silu(x):
    return x / (1.0 + jnp.exp(-x))


def _params(sem, vmem=VMEM_LIMIT):
    return pltpu.CompilerParams(dimension_semantics=sem, vmem_limit_bytes=vmem)


def _rms_bf16(x_ref, g_ref):
    x = x_ref[...]
    ms = jnp.mean(x * x, axis=-1, keepdims=True)
    return (x * lax.rsqrt(ms + EPS) * g_ref[...]).astype(BF16)


def _norm_mm_body(x_ref, g_ref, w_ref, o_ref, *, tn):
    xn = _rms_bf16(x_ref, g_ref)
    for c in range(w_ref.shape[1] // tn):
        o_ref[:, c * tn:(c + 1) * tn] = jnp.dot(xn, w_ref[:, c * tn:(c + 1) * tn], preferred_element_type=F32)


def norm_mm(x, g, w, tm, tn):
    m, d = x.shape
    n = w.shape[1]
    assert n % tn == 0
    return pl.pallas_call(
        functools.partial(_norm_mm_body, tn=tn),
        grid=(m // tm,),
        in_specs=[pl.BlockSpec((tm, d), lambda i: (i, 0)),
                  pl.BlockSpec((1, d), lambda i: (0, 0)),
                  pl.BlockSpec((d, n), lambda i: (0, 0), pipeline_mode=pl.Buffered(1))],
        out_specs=pl.BlockSpec((tm, n), lambda i: (i, 0)),
        out_shape=jax.ShapeDtypeStruct((m, n), F32),
        compiler_params=_params(("parallel",), VMEM_LIMIT_BIG),
        name="norm_mm",
    )(x, g.reshape(1, d), w)


def _swiglu_gu_body(x_ref, g_ref, w_ref, o_ref, *, tn):
    xn = _rms_bf16(x_ref, g_ref)
    dff = o_ref.shape[1]
    for c in range(dff // tn):
        gate = jnp.dot(xn, w_ref[:, c * tn:(c + 1) * tn], preferred_element_type=F32)
        up = jnp.dot(xn, w_ref[:, dff + c * tn:dff + (c + 1) * tn], preferred_element_type=F32)
        o_ref[:, c * tn:(c + 1) * tn] = (_silu(gate) * up).astype(BF16)


def swiglu_gu(x, g, w_gu, tm, tn):
    m, d = x.shape
    dff = w_gu.shape[1] // 2
    assert dff % tn == 0
    return pl.pallas_call(
        functools.partial(_swiglu_gu_body, tn=tn),
        grid=(m // tm,),
        in_specs=[pl.BlockSpec((tm, d), lambda i: (i, 0)),
                  pl.BlockSpec((1, d), lambda i: (0, 0)),
                  pl.BlockSpec((d, 2 * dff), lambda i: (0, 0), pipeline_mode=pl.Buffered(1))],
        out_specs=pl.BlockSpec((tm, dff), lambda i: (i, 0)),
        out_shape=jax.ShapeDtypeStruct((m, dff), BF16),
        compiler_params=_params(("parallel",)),
        name="swiglu_gu",
    )(x, g.reshape(1, d), w_gu)


def _mm_res_body(a_ref, w_ref, r_ref, o_ref):
    o_ref[...] = r_ref[...] + jnp.dot(a_ref[...], w_ref[...], preferred_element_type=F32)


def _mm_res_norm_body(a_ref, w_ref, r_ref, g_ref, o_ref):
    y = r_ref[...] + jnp.dot(a_ref[...], w_ref[...], preferred_element_type=F32)
    ms = jnp.mean(y * y, axis=-1, keepdims=True)
    o_ref[...] = y * lax.rsqrt(ms + EPS) * g_ref[...]


def mm_res(a, w, res, tm, final_g=None):
    m, k = a.shape
    n = w.shape[1]
    in_specs = [pl.BlockSpec((tm, k), lambda i: (i, 0)),
                pl.BlockSpec((k, n), lambda i: (0, 0)),
                pl.BlockSpec((tm, n), lambda i: (i, 0))]
    args = [a, w, res]
    body = _mm_res_body
    if final_g is not None:
        in_specs.append(pl.BlockSpec((1, n), lambda i: (0, 0)))
        args.append(final_g.reshape(1, n))
        body = _mm_res_norm_body
    return pl.pallas_call(
        body,
        grid=(m // tm,),
        in_specs=in_specs,
        out_specs=pl.BlockSpec((tm, n), lambda i: (i, 0)),
        out_shape=jax.ShapeDtypeStruct((m, n), F32),
        compiler_params=_params(("parallel",)),
        name="mm_res_norm" if final_g is not None else "mm_res",
    )(*args)


def _qkv_body(x_ref, g_ref, w_ref, wq_lo_ref, q_ref, k_ref, v_ref, xm_ref):
    d = x_ref.shape[1]
    x = x_ref[...]
    ms = jnp.mean(x * x, axis=-1, keepdims=True)
    xn = x * lax.rsqrt(ms + EPS) * g_ref[...]
    hi = xn.astype(BF16)
    lo = (xn - hi.astype(F32)).astype(BF16)
    wq = w_ref[:, 0:d]
    q_ref[...] = (jnp.dot(hi, wq, preferred_element_type=F32) + jnp.dot(lo, wq, preferred_element_type=F32)
                  + jnp.dot(hi, wq_lo_ref[...], preferred_element_type=F32))
    k_ref[...] = jnp.dot(hi, w_ref[:, d:2 * d], preferred_element_type=F32)
    v_ref[...] = jnp.dot(hi, w_ref[:, 2 * d:3 * d], preferred_element_type=F32)
    for blk in range(xm_ref.shape[0]):
        xm_ref[blk] = jnp.mean(xn[blk * MOBA_BLOCK:(blk + 1) * MOBA_BLOCK], axis=0, keepdims=True)


def qkv_proj(x, g, w, wq_lo, tm):
    m, d = x.shape
    assert tm % MOBA_BLOCK == 0
    row = pl.BlockSpec((tm, d), lambda i: (i, 0))
    out = jax.ShapeDtypeStruct((m, d), F32)
    return pl.pallas_call(
        _qkv_body,
        grid=(m // tm,),
        in_specs=[row, pl.BlockSpec((1, d), lambda i: (0, 0)),
                  pl.BlockSpec((d, 3 * d), lambda i: (0, 0), pipeline_mode=pl.Buffered(1)),
                  pl.BlockSpec((d, d), lambda i: (0, 0), pipeline_mode=pl.Buffered(1))],
        out_specs=[row, row, row, pl.BlockSpec((tm // MOBA_BLOCK, 1, d), lambda i: (i, 0, 0))],
        out_shape=[out, out, out, jax.ShapeDtypeStruct((m // MOBA_BLOCK, 1, d), F32)],
        compiler_params=_params(("parallel",)),
        name="qkv_proj",
    )(x, g.reshape(1, d), w, wq_lo)


def _qkv_prompt_body(x_ref, g_ref, w_ref, wq_lo_ref, wkvt_ref, kx_ref, q_ref, ka_ref, va_ref, kt_ref, vt_ref,
                     xm_ref):
    d = x_ref.shape[1]
    n_heads = ka_ref.shape[1]
    tm = x_ref.shape[0]
    half = AUG // 2
    x = x_ref[...]
    ms = jnp.mean(x * x, axis=-1, keepdims=True)
    xn = x * lax.rsqrt(ms + EPS) * g_ref[...]
    hi = xn.astype(BF16)
    lo = (xn - hi.astype(F32)).astype(BF16)
    wq = w_ref[:, 0:d]
    q_ref[...] = (jnp.dot(hi, wq, preferred_element_type=F32) + jnp.dot(lo, wq, preferred_element_type=F32)
                  + jnp.dot(hi, wq_lo_ref[...], preferred_element_type=F32))
    k = jnp.dot(hi, w_ref[:, d:2 * d], preferred_element_type=F32)
    v = jnp.dot(hi, w_ref[:, 2 * d:3 * d], preferred_element_type=F32)
    lane = lax.broadcasted_iota(jnp.int32, (tm, AUG), 1)
    for pair in range(n_heads // 2):
        kb = k[:, pair * AUG:(pair + 1) * AUG]
        vb = v[:, pair * AUG:(pair + 1) * AUG]
        for hh in range(2):
            h = 2 * pair + hh
            data = (lane < half) if hh == 0 else (lane >= half)
            ones_lane = half if hh == 0 else 0
            ka_ref[0, h] = jnp.where(data, kb, kx_ref[h].astype(F32)).astype(BF16)
            va_ref[0, h] = jnp.where(data, vb, jnp.where(lane == ones_lane, 1.0, 0.0)).astype(BF16)
    kt = lax.dot_general(wkvt_ref[0:d, :], hi, _NT, preferred_element_type=F32)
    vt = lax.dot_general(wkvt_ref[d:2 * d, :], hi, _NT, preferred_element_type=F32)
    kt_ref[0] = kt.reshape(n_heads, HEAD_DIM, tm)
    vt_ref[0] = vt.reshape(n_heads, HEAD_DIM, tm)
    for blk in range(xm_ref.shape[0]):
        xm_ref[blk] = jnp.mean(xn[blk * MOBA_BLOCK:(blk + 1) * MOBA_BLOCK], axis=0, keepdims=True)


def qkv_prompt(x, g, w, wq_lo, wkvt, kx, n_seq, seq, tm):
    m, d = x.shape
    n_heads = d // HEAD_DIM
    assert tm % MOBA_BLOCK == 0 and seq % tm == 0
    steps = seq // tm
    row = pl.BlockSpec((tm, d), lambda i: (i, 0))
    aug_spec = pl.BlockSpec((1, n_heads, tm, AUG), lambda i: (i // steps, 0, i % steps, 0))
    t_spec = pl.BlockSpec((1, n_heads, HEAD_DIM, tm), lambda i: (i // steps, 0, 0, i % steps))
    aug_shape = jax.ShapeDtypeStruct((n_seq, n_heads, seq, AUG), BF16)
    t_shape = jax.ShapeDtypeStruct((n_seq, n_heads, HEAD_DIM, seq), F32)

    def whole(shape):
        return pl.BlockSpec(shape, lambda i: (0,) * len(shape), pipeline_mode=pl.Buffered(1))

    return pl.pallas_call(
        _qkv_prompt_body,
        grid=(m // tm,),
        in_specs=[row, pl.BlockSpec((1, d), lambda i: (0, 0)), whole((d, 3 * d)), whole((d, d)), whole((2 * d, d)),
                  pl.BlockSpec((n_heads, tm, AUG), lambda i: (0, i % steps, 0))],
        out_specs=[row, aug_spec, aug_spec, t_spec, t_spec,
                   pl.BlockSpec((tm // MOBA_BLOCK, 1, d), lambda i: (i, 0, 0))],
        out_shape=[jax.ShapeDtypeStruct((m, d), F32), aug_shape, aug_shape, t_shape, t_shape,
                   jax.ShapeDtypeStruct((m // MOBA_BLOCK, 1, d), F32)],
        compiler_params=_params(("parallel",), VMEM_LIMIT_BIG),
        name="qkv_prompt",
    )(x, g.reshape(1, d), w, wq_lo, wkvt, kx)


def _mm_f32_body(a_ref, w_ref, o_ref):
    o_ref[...] = jnp.dot(a_ref[...], w_ref[...], precision=lax.Precision.HIGHEST, preferred_element_type=F32)


def mm_f32(a, w):
    m, k = a.shape
    n = w.shape[1]
    return pl.pallas_call(
        _mm_f32_body,
        grid=(1,),
        in_specs=[pl.BlockSpec((m, k), lambda i: (0, 0)), pl.BlockSpec((k, n), lambda i: (0, 0))],
        out_specs=pl.BlockSpec((m, n), lambda i: (0, 0)),
        out_shape=jax.ShapeDtypeStruct((m, n), F32),
        compiler_params=_params(("arbitrary",)),
        name="mm_f32",
    )(a, w)


def _topk_select(gate, valid, blk, n_blocks):
    tiles = n_blocks // SUBLANES

    def over_blocks(x, op):
        acc = x[0:SUBLANES]
        for i in range(1, tiles):
            acc = op(acc, x[i * SUBLANES:(i + 1) * SUBLANES])
        shift = SUBLANES // 2
        while shift:
            acc = op(acc, pltpu.roll(acc, shift, axis=0))
            shift //= 2
        return jnp.concatenate([acc] * tiles, axis=0) if tiles > 1 else acc

    g = jnp.where(valid, gate, -jnp.inf)
    taken = jnp.zeros(g.shape, jnp.int32)
    for _ in range(MOBA_TOPK):
        best = over_blocks(g, jnp.maximum)
        first = over_blocks(jnp.where(g == best, blk, n_blocks), jnp.minimum)
        pick = blk == first
        taken = jnp.where(pick, 1, taken)
        g = jnp.where(pick, -jnp.inf, g)
    return jnp.logical_and(valid, taken > 0)


def _bf16_trunc(x):
    bits = lax.bitcast_convert_type(x, jnp.uint32) & jnp.uint32(0xFFFF0000)
    return lax.bitcast_convert_type(bits, F32)


def _bf16_round(x):
    bits = lax.bitcast_convert_type(x, jnp.uint32)
    bits = (bits + jnp.uint32(0x7FFF) + ((bits >> 16) & jnp.uint32(1))) & jnp.uint32(0xFFFF0000)
    return lax.bitcast_convert_type(bits, F32)


def _split3(x):
    hi = _bf16_trunc(x)
    mid = _bf16_trunc(x - hi)
    return hi, mid, x - hi - mid


def _moba_prep_body(q_ref, kmean_ref, qa_ref, *, n_blocks):
    half = AUG // 2
    kmean = kmean_ref[...]
    blk = lax.broadcasted_iota(jnp.int32, (n_blocks, MOBA_BLOCK), 0)
    lane = lax.broadcasted_iota(jnp.int32, (MOBA_BLOCK, AUG), 1)
    arow = lax.broadcasted_iota(jnp.int32, (SUBLANES, MOBA_BLOCK), 0)
    ones_rows = jnp.where(arow < 3, 1.0, 0.0).astype(F32)
    gap = jnp.zeros((half - n_blocks - SUBLANES, MOBA_BLOCK), F32)
    other = jnp.zeros((half, MOBA_BLOCK), F32)
    for j in range(q_ref.shape[0] // MOBA_BLOCK):
        sb = pl.program_id(1) * (q_ref.shape[0] // MOBA_BLOCK) + j
        rows = slice(j * MOBA_BLOCK, (j + 1) * MOBA_BLOCK)
        valid = blk < sb
        q = q_ref[rows, :] * (LOG2E * HEAD_DIM ** -0.5)
        for hh in range(2):
            first = hh == 0
            data = (lane < half) if first else (lane >= half)
            gate = lax.dot_general(kmean[:, hh * half:(hh + 1) * half], q_ref[rows, hh * half:(hh + 1) * half], _NT,
                                   precision=lax.Precision.HIGHEST, preferred_element_type=F32)
            sel = _topk_select(gate, valid, blk, n_blocks)
            bias_t = jnp.where(sel, 0.0, jnp.where(blk == sb, 0.0, NEG_BIAS)).astype(F32)
            extras_t = [bias_t, ones_rows, gap]
            aug = jnp.concatenate([other] + extras_t if first else extras_t + [other], axis=0).T
            qa_ref[0, hh, rows, :] = jnp.where(data, q, aug).astype(BF16)


def _key_extras(slopes, seq, n_blocks):
    shape = (slopes.shape[0], seq, AUG)
    pos = jnp.arange(seq, dtype=F32)
    hi, mid, lo = _split3(pos[None, :] * (slopes * LOG2E)[:, None])
    head = lax.broadcasted_iota(jnp.int32, shape, 0)
    key = lax.broadcasted_iota(jnp.int32, shape, 1)
    lane = lax.broadcasted_iota(jnp.int32, shape, 2)
    e = lane - jnp.where(head % 2 == 0, AUG // 2, 0)
    kx = jnp.where(e == key // MOBA_BLOCK, 1.0, 0.0)
    for i, part in enumerate((hi, mid, lo)):
        kx = jnp.where(e == n_blocks + i, part[:, :, None], kx)
    return kx.astype(BF16)


def moba_prep(q, kmean, n_seq, seq, n_heads):
    n_blocks = seq // MOBA_BLOCK
    assert n_blocks + SUBLANES <= AUG // 2 and n_blocks % SUBLANES == 0
    hpairs = n_heads // 2
    rows = PREP_BLOCKS * MOBA_BLOCK
    n_steps = seq // rows
    assert seq % rows == 0
    return pl.pallas_call(
        functools.partial(_moba_prep_body, n_blocks=n_blocks),
        grid=(hpairs, n_steps, n_seq),
        in_specs=[pl.BlockSpec((rows, LANES), lambda hp, sb, b: (b * n_steps + sb, hp)),
                  pl.BlockSpec((n_blocks, LANES), lambda hp, sb, b: (b, hp))],
        out_specs=pl.BlockSpec((1, 2, rows, AUG), lambda hp, sb, b: (b, hp, sb, 0)),
        out_shape=jax.ShapeDtypeStruct((n_seq, n_heads, seq, AUG), BF16),
        compiler_params=_params(("parallel", "parallel", "parallel")),
        name="moba_prep",
    )(q, kmean)


def _moba_attn_body(qa_ref, ka_ref, va_ref, o_ref, *, n_blocks):
    t = MOBA_BLOCK
    grp = ATTN_GROUP
    rows = grp * t
    kw = ATTN_KV_BLOCKS
    out_lane = lax.broadcasted_iota(jnp.int32, (rows, AUG), 1)

    def update(q, m, acc, hh, first_blk, n_blk, mask=None):
        width = n_blk * t
        start = pl.multiple_of(first_blk * t, t)
        s = lax.dot_general(q, ka_ref[0, hh, pl.ds(start, width), :], _NT, preferred_element_type=F32)
        if mask is not None:
            s = jnp.where(mask, s, -jnp.inf)
        m_new = jnp.maximum(m, jnp.max(s, axis=-1, keepdims=True))
        p = jnp.exp2(s - m_new)
        acc = jnp.exp2(m - m_new) * acc + jnp.dot(p.astype(BF16), va_ref[0, hh, pl.ds(start, width), :],
                                                  preferred_element_type=F32)
        return m_new, acc

    def group(g, carry):
        base = g * grp
        qs = [qa_ref[0, hh, pl.ds(pl.multiple_of(base * t, t), rows), :] for hh in range(2)]

        def past_tiles(kj, state):
            out = []
            for hh in range(2):
                out.extend(update(qs[hh], state[2 * hh], state[2 * hh + 1], hh, kj * kw, kw))
            return tuple(out)

        init = [jnp.full((rows, 1), -jnp.inf, F32), jnp.zeros((rows, AUG), F32)] * 2
        state = list(lax.fori_loop(0, base // kw, past_tiles, tuple(init)))

        rr = lax.broadcasted_iota(jnp.int32, (t, t), 0)
        cc = lax.broadcasted_iota(jnp.int32, (t, t), 1)
        causal = cc <= rr
        outs = []
        half = grp // 2
        for hh in range(2):
            den_lane = HEAD_DIM if hh == 0 else 0
            m_all, acc_all = state[2 * hh], state[2 * hh + 1]
            m_late, acc_late = update(qs[hh][half * t:], m_all[half * t:], acc_all[half * t:], hh, base, half)
            parts = []
            for a in range(grp):
                sl = slice(a * t, (a + 1) * t)
                if a < half:
                    m, acc, first = m_all[sl], acc_all[sl], 0
                else:
                    late = slice((a - half) * t, (a - half + 1) * t)
                    m, acc, first = m_late[late], acc_late[late], half
                for b in range(first, a + 1):
                    m, acc = update(qs[hh][sl], m, acc, hh, base + b, 1, causal if b == a else None)
                parts.append(acc / acc[:, den_lane:den_lane + 1])
            outs.append(jnp.concatenate(parts, axis=0))
        o_ref[pl.ds(pl.multiple_of(base * t, t), rows), :] = jnp.where(
            out_lane < HEAD_DIM, outs[0], outs[1]).astype(BF16)
        return carry

    lax.fori_loop(0, n_blocks // grp, group, 0)


def moba_attn(qa, ka, va):
    n_seq, n_heads, seq, _ = qa.shape
    n_blocks = seq // MOBA_BLOCK
    assert n_blocks % ATTN_GROUP == 0 and ATTN_GROUP % ATTN_KV_BLOCKS == 0
    spec = pl.BlockSpec((1, 2, seq, AUG), lambda b, hp: (b, hp, 0, 0))
    return pl.pallas_call(
        functools.partial(_moba_attn_body, n_blocks=n_blocks),
        grid=(n_seq, n_heads // 2),
        in_specs=[spec, spec, spec],
        out_specs=pl.BlockSpec((seq, LANES), lambda b, hp: (b, hp)),
        out_shape=jax.ShapeDtypeStruct((n_seq * seq, n_heads * HEAD_DIM), BF16),
        compiler_params=_params(("parallel", "parallel")),
        name="moba_attn",
    )(qa, ka, va)


def _sample_attn_body(pt_ref, rowinfo_ref, q_ref, kn_ref, vn_ref, ck_ref, cv_ref, o_ref, qbd_ref, qbd32_ref, s_ref,
                      ksum_ref, acc_ref, knt_ref, vnt_ref, bias_ref, ring_ref, sem_ref, *, n_pages, pps, n_new,
                      n_heads):
    b = pl.program_id(0)
    t = pl.program_id(1)
    tk = n_pages // pps
    page = ck_ref.shape[3]
    d = q_ref.shape[1]
    past = n_pages * page
    n_blocks = past // MOBA_BLOCK
    ppb = MOBA_BLOCK // page
    n_rows = n_heads * n_new
    steps = 2 * tk
    total = pl.num_programs(0) * steps
    step = b * steps + t
    slot = step % PAGE_SLOTS

    def page_copy(src_ref, pool_page, to_slot, i):
        return pltpu.make_async_copy(src_ref.at[pool_page], ring_ref.at[to_slot, i], sem_ref.at[to_slot])

    def fetch(at):
        seq, tt, to_slot = at // steps, at % steps, at % PAGE_SLOTS

        @pl.when(tt < tk)
        def _():
            for i in range(pps):
                page_copy(ck_ref, pt_ref[seq, tt * pps + i], to_slot, i).start()

        @pl.when(tt >= tk)
        def _():
            for i in range(pps):
                page_copy(cv_ref, pt_ref[seq, (tt - tk) * pps + i], to_slot, i).start()

    @pl.when(step == 0)
    def _():
        for ahead in range(PAGE_SLOTS):
            fetch(ahead)

    @pl.when(jnp.logical_and(step > 0, step + PAGE_SLOTS - 1 < total))
    def _():
        fetch(step + PAGE_SLOTS - 1)

    for i in range(pps):
        page_copy(ck_ref, 0, slot, i).wait()

    head_of_col = lax.broadcasted_iota(jnp.int32, (n_heads, d), 1) // HEAD_DIM
    head_of_row = lax.broadcasted_iota(jnp.int32, (n_heads, d), 0)
    own_head = head_of_col == head_of_row

    def key_major(rows_val):
        padded = jnp.concatenate([rows_val, jnp.zeros((LANES - rows_val.shape[0], d), F32)], axis=0)
        return padded.T

    @pl.when(t == 0)
    def _():
        q = q_ref[...]
        qbd = jnp.concatenate([jnp.where(own_head, jnp.broadcast_to(q[n:n + 1, :], (n_heads, d)), 0.0)
                               for n in range(n_new)], axis=0)
        qbd32_ref[0:n_rows, :] = qbd
        qbd32_ref[n_rows:, :] = jnp.zeros((LANES - n_rows, d), F32)
        qbd_ref[...] = (qbd * HEAD_DIM ** -0.5).astype(BF16)
        knt_ref[...] = key_major(kn_ref[...])
        vnt_ref[...] = key_major(vn_ref[...])
        acc_ref[...] = jnp.zeros_like(acc_ref)
        ksum_ref[...] = jnp.zeros_like(ksum_ref)

    @pl.when(t < tk)
    def _():
        qbd = qbd_ref[...]
        lane_i = lax.broadcasted_iota(jnp.int32, (1, LANES), 1)
        for bi in range(pps // ppb):
            ksum = jnp.zeros((d, page), F32)
            for pi in range(ppb):
                i = bi * ppb + pi
                kp = ring_ref[slot, i].reshape(d, page)
                s_ref[t * pps + i] = jnp.dot(qbd, kp.astype(BF16), preferred_element_type=F32)
                ksum = ksum + kp
            onehot = jnp.where(lane_i == t * (pps // ppb) + bi, 1.0, 0.0).astype(F32)
            ksum_ref[...] += jnp.sum(ksum, axis=1, keepdims=True) * onehot

    @pl.when(t == tk - 1)
    def _():
        s_ref[n_pages] = jnp.dot(qbd_ref[...], knt_ref[...].astype(BF16), preferred_element_type=F32)
        gate = jnp.dot(qbd32_ref[...], ksum_ref[...], precision=lax.Precision.HIGHEST,
                       preferred_element_type=F32) * (1.0 / MOBA_BLOCK)
        gate_t = gate.T[:n_blocks]
        blk = lax.broadcasted_iota(jnp.int32, (n_blocks, LANES), 0)
        sel_t = _topk_select(gate_t, blk >= 0, blk, n_blocks)
        sel = jnp.concatenate([jnp.where(sel_t, 1.0, 0.0).astype(F32),
                               jnp.zeros((LANES - n_blocks, LANES), F32)], axis=0).T[:n_rows]
        q_idx = rowinfo_ref[0]
        slope = rowinfo_ref[1]
        lane_f = lax.broadcasted_iota(jnp.int32, (n_rows, LANES), 1).astype(F32)

        for b in range(n_blocks):
            bias_ref[b] = jnp.broadcast_to(jnp.where(sel[:, b:b + 1] > 0.5, 0.0, -jnp.inf), (n_rows, LANES))

        def logits(p):
            dist = (past + q_idx) - (jnp.asarray(p * page, F32) + lane_f)
            return s_ref[p] - slope * dist + bias_ref[p // ppb]

        dist_new = q_idx - lane_f
        s_new = jnp.where((dist_new >= 0) & (lane_f < n_new), s_ref[n_pages] - slope * dist_new, -jnp.inf)
        m = jnp.max(lax.fori_loop(0, n_pages, lambda p, mv: jnp.maximum(mv, logits(p)), s_new, unroll=8),
                    axis=1, keepdims=True)

        def exp_step(p, den):
            e = jnp.exp(logits(p) - m)
            s_ref[p] = e
            return den + e

        e_new = jnp.exp(s_new - m)
        inv = 1.0 / jnp.sum(lax.fori_loop(0, n_pages, exp_step, e_new, unroll=8), axis=1, keepdims=True)
        s_ref[n_pages] = e_new * inv

        def norm_step(p, carry):
            s_ref[p] = s_ref[p] * inv
            return carry

        lax.fori_loop(0, n_pages, norm_step, 0, unroll=8)

    def weighted(p_idx, v_t):
        return lax.dot_general(s_ref[p_idx].astype(BF16), v_t.astype(BF16), _NT, preferred_element_type=F32)

    @pl.when(t >= tk)
    def _():
        acc = acc_ref[...]
        for i in range(pps):
            acc = acc + weighted((t - tk) * pps + i, ring_ref[slot, i].reshape(d, page))
        acc_ref[...] = acc

    @pl.when(t == 2 * tk - 1)
    def _():
        acc = acc_ref[...] + weighted(n_pages, vnt_ref[...])
        outs = [jnp.sum(jnp.where(own_head, acc[n * n_heads:(n + 1) * n_heads, :], 0.0), axis=0, keepdims=True)
                for n in range(n_new)]
        outs.append(jnp.zeros((SAMPLE_PAD - n_new, d), F32))
        o_ref[...] = jnp.concatenate(outs, axis=0).astype(BF16)


def sample_attn(page_table, rowinfo, q, k_new, v_new, cache_kt, cache_vt, n_new, pps=8):
    n_dec, n_pages = page_table.shape
    _, n_heads, _, page = cache_kt.shape
    d = n_heads * HEAD_DIM
    ppb = MOBA_BLOCK // page
    assert n_pages % pps == 0 and pps % ppb == 0 and page == LANES
    assert n_new <= SAMPLE_PAD and n_heads % SUBLANES == 0
    tk = n_pages // pps
    n_blocks = n_pages // ppb
    n_rows = n_heads * n_new
    assert n_blocks <= LANES and n_rows <= LANES and n_dec * 2 * tk >= PAGE_SLOTS

    row_spec = pl.BlockSpec((SAMPLE_PAD, d), lambda b, t, pt: (b, 0))
    pool_spec = pl.BlockSpec(memory_space=pl.ANY)
    grid_spec = pltpu.PrefetchScalarGridSpec(
        num_scalar_prefetch=1,
        grid=(n_dec, 2 * tk),
        in_specs=[pl.BlockSpec((2, n_rows, LANES), lambda b, t, pt: (0, 0, 0)), row_spec, row_spec, row_spec,
                  pool_spec, pool_spec],
        out_specs=row_spec,
        scratch_shapes=[pltpu.VMEM((n_rows, d), BF16),
                        pltpu.VMEM((LANES, d), F32),
                        pltpu.VMEM((n_pages + 1, n_rows, LANES), F32),
                        pltpu.VMEM((d, LANES), F32),
                        pltpu.VMEM((n_rows, d), F32),
                        pltpu.VMEM((d, LANES), F32),
                        pltpu.VMEM((d, LANES), F32),
                        pltpu.VMEM((n_blocks, n_rows, LANES), F32),
                        pltpu.VMEM((PAGE_SLOTS, pps, n_heads, HEAD_DIM, page), F32),
                        pltpu.SemaphoreType.DMA((PAGE_SLOTS,))],
    )
    return pl.pallas_call(
        functools.partial(_sample_attn_body, n_pages=n_pages, pps=pps, n_new=n_new, n_heads=n_heads),
        grid_spec=grid_spec,
        out_shape=jax.ShapeDtypeStruct((n_dec * SAMPLE_PAD, d), BF16),
        compiler_params=_params(("arbitrary", "arbitrary")),
        name="sample_attn",
    )(page_table, rowinfo, q, k_new, v_new, cache_kt, cache_vt)


def _ssd_body(z_ref, xs_ref, bc_ref, dt_ref, cinit_ref, sinit_ref, cw_ref, cb_ref, dtb_ref, alog_ref,
              dsk_ref, gn_ref, y_ref, cout_ref, sout_ref, ext_ref, xc_ref, h_ref, **static):
    for phase in ("start", "main", "end"):
        for p in range(z_ref.shape[0]):
            _ssd_chunk(z_ref.at[p], xs_ref.at[p], bc_ref.at[p], dt_ref.at[p], cinit_ref.at[p], sinit_ref.at[p],
                       cw_ref, cb_ref, dtb_ref, alog_ref, dsk_ref, gn_ref, y_ref.at[p], cout_ref.at[p],
                       sout_ref.at[p], ext_ref.at[p], xc_ref.at[p], h_ref.at[p], phase=phase, **static)


def _ssd_chunk(z_ref, xs_ref, bc_ref, dt_ref, cinit_ref, sinit_ref, cw_ref, cb_ref, dtb_ref, alog_ref,
               dsk_ref, gn_ref, y_ref, cout_ref, sout_ref, ext_ref, xc_ref, h_ref, *, rows, valid, n_chunks,
               phase):
    cl = xc_ref.shape[0]
    c = pl.program_id(1)
    d_inner = xs_ref.shape[1]
    n_heads = d_inner // SSM_HEAD_DIM
    hpg = n_heads // N_GROUPS
    gw = hpg * SSM_HEAD_DIM
    tail = SUBLANES

    if phase == "start":
        @pl.when(c == 0)
        def _():
            ext_ref[0:tail, :] = cinit_ref[...]
            h_ref[...] = sinit_ref[...].reshape(h_ref.shape)
            if rows < cl:
                ext_ref[tail + rows:, :] = jnp.zeros((cl - rows, ext_ref.shape[1]), F32)

        @pl.when(c > 0)
        def _():
            ext_ref[0:tail, :] = ext_ref[cl:cl + tail, :]

        return

    if phase == "end":
        @pl.when(c == n_chunks - 1)
        def _():
            sout_ref[...] = h_ref[...].reshape(sout_ref.shape)
            first = tail + ((valid - (CONV_W - 1)) // SUBLANES) * SUBLANES
            cout_ref[...] = ext_ref[first:first + SUBLANES, :]

        return

    ext_ref[tail:tail + rows, 0:d_inner] = xs_ref[...]
    ext_ref[tail:tail + rows, d_inner:] = bc_ref[...]

    cw = cw_ref[...]
    conv = cb_ref[...] + ext_ref[tail:tail + cl, :] * cw[CONV_W - 1:CONV_W, :]
    for back in range(1, CONV_W):
        conv = conv + ext_ref[tail - back:tail - back + cl, :] * cw[CONV_W - 1 - back:CONV_W - back, :]
    xc_ref[...] = _silu(conv)

    def pad(v):
        if rows == cl:
            return v
        return jnp.concatenate([v, jnp.zeros((cl - rows, v.shape[1]), v.dtype)], axis=0)

    trow = lax.broadcasted_iota(jnp.int32, (cl, LANES), 0)
    dt_raw = pad(dt_ref[...]) + dtb_ref[...]
    dt = jnp.maximum(dt_raw, 0.0) + jnp.log1p(jnp.exp(-jnp.abs(dt_raw)))
    dt = jnp.where(trow < valid, dt, 0.0)
    a = -jnp.exp(alog_ref[...])
    tri_r = lax.broadcasted_iota(jnp.int32, (cl, cl), 0)
    tri_c = lax.broadcasted_iota(jnp.int32, (cl, cl), 1)
    causal = tri_c <= tri_r
    a_cs = jnp.dot(jnp.where(causal, 1.0, 0.0).astype(F32), dt * a, precision=lax.Precision.HIGHEST,
                   preferred_element_type=F32)
    a_cs_t = a_cs.T
    dt_t = dt.T
    z = pad(z_ref[...])
    from_start = jnp.exp(a_cs)
    to_end_all = jnp.exp(a_cs_t[:, cl - 1:cl] - a_cs_t) * dt_t

    for g in range(N_GROUPS):
        bg = xc_ref[:, d_inner + g * D_STATE:d_inner + (g + 1) * D_STATE].astype(BF16)
        cg = xc_ref[:, d_inner + (N_GROUPS + g) * D_STATE:d_inner + (N_GROUPS + g + 1) * D_STATE].astype(BF16)
        cb = lax.dot_general(cg, bg, _NT, preferred_element_type=F32)
        xg = xc_ref[:, g * gw:(g + 1) * gw]
        xg_t = xg.T
        h_old = h_ref[g * gw:(g + 1) * gw, :]
        y_off = lax.dot_general(cg, h_old.astype(BF16), _NT, preferred_element_type=F32)
        ys, xw_rows, h_scaled = [], [], []
        for e in range(hpg):
            h = g * hpg + e
            acs_col = a_cs[:, h:h + 1]
            acs_row = a_cs_t[h:h + 1, :]
            dt_row = dt_t[h:h + 1, :]
            decay = jnp.exp(jnp.where(causal, acs_col - acs_row, -jnp.inf))
            w = (cb * decay * dt_row).astype(BF16)
            xh = xg[:, e * SSM_HEAD_DIM:(e + 1) * SSM_HEAD_DIM]
            y_diag = jnp.dot(w, xh.astype(BF16), preferred_element_type=F32)
            ys.append(y_diag + y_off[:, e * SSM_HEAD_DIM:(e + 1) * SSM_HEAD_DIM] * from_start[:, h:h + 1])
            xw_rows.append(xg_t[e * SSM_HEAD_DIM:(e + 1) * SSM_HEAD_DIM, :] * to_end_all[h:h + 1, :])
            h_scaled.append(h_old[e * SSM_HEAD_DIM:(e + 1) * SSM_HEAD_DIM, :] * from_start[cl - 1:cl, h:h + 1])
        states = jnp.dot(jnp.concatenate(xw_rows, axis=0).astype(BF16), bg, preferred_element_type=F32)
        h_ref[g * gw:(g + 1) * gw, :] = jnp.concatenate(h_scaled, axis=0) + states
        y = jnp.concatenate(ys, axis=1) + dsk_ref[:, g * gw:(g + 1) * gw] * xg
        yz = y * _silu(z[:, g * gw:(g + 1) * gw])
        ms = jnp.mean(yz * yz, axis=-1, keepdims=True)
        yn = yz * lax.rsqrt(ms + EPS) * gn_ref[:, g * gw:(g + 1) * gw]
        y_ref[:, g * gw:(g + 1) * gw] = yn[:rows].astype(BF16)


def ssd_mixer(proj, conv_init, ssm_init, cw, cb, dtb, alog, dsk, gnorm, n_seq, rows, valid, d_inner):
    total_rows = proj.shape[0] // n_seq
    n_chunks = total_rows // rows
    assert rows == SSD_CHUNK or n_chunks == 1
    chunk = rows if n_chunks == 1 else SSD_CHUNK
    assert CONV_W - 1 <= valid <= rows
    n_heads = d_inner // SSM_HEAD_DIM
    conv_dim = d_inner + 2 * N_GROUPS * D_STATE
    assert conv_dim == 2 * d_inner

    par = SSD_SEQS_PER_STEP
    assert n_seq % par == 0
    proj3 = proj.reshape(n_seq, total_rows, proj.shape[1])

    def col(width, idx):
        return pl.BlockSpec((par, rows, width), lambda b, c: (b, c, idx))

    def const(shape):
        return pl.BlockSpec(shape, lambda b, c: (0,) * len(shape))

    def per_seq(shape):
        return pl.BlockSpec((par,) + shape, lambda b, c: (b,) + (0,) * len(shape))

    y, conv_out, ssm_out = pl.pallas_call(
        functools.partial(_ssd_body, rows=rows, valid=valid, n_chunks=n_chunks),
        grid=(n_seq // par, n_chunks),
        in_specs=[col(d_inner, 0), col(d_inner, 1), col(d_inner, 2), col(LANES, 3 * d_inner // LANES),
                  per_seq((SUBLANES, conv_dim)), per_seq((n_heads, SSM_HEAD_DIM, D_STATE)),
                  const((CONV_W, conv_dim)), const((1, conv_dim)), const((1, LANES)), const((1, LANES)),
                  const((1, d_inner)), const((1, d_inner))],
        out_specs=[col(d_inner, 0), per_seq((SUBLANES, conv_dim)), per_seq((n_heads, SSM_HEAD_DIM, D_STATE))],
        out_shape=[jax.ShapeDtypeStruct((n_seq, total_rows, d_inner), BF16),
                   jax.ShapeDtypeStruct((n_seq, SUBLANES, conv_dim), F32),
                   jax.ShapeDtypeStruct((n_seq, n_heads, SSM_HEAD_DIM, D_STATE), F32)],
        scratch_shapes=[pltpu.VMEM((par, SUBLANES + chunk, conv_dim), F32),
                        pltpu.VMEM((par, chunk, conv_dim), F32),
                        pltpu.VMEM((par, n_heads * SSM_HEAD_DIM, D_STATE), F32)],
        compiler_params=_params(("parallel", "arbitrary")),
        name="ssd_mixer",
    )(proj3, proj3, proj3, proj3, conv_init, ssm_init, cw, cb, dtb, alog, dsk, gnorm)
    return y.reshape(n_seq * total_rows, d_inner), conv_out, ssm_out


def _pad_cols(w, n):
    return jnp.pad(w, ((0, 0), (0, n - w.shape[1])))


def _pad_lanes(v):
    return jnp.pad(v, (0, LANES - v.shape[0])).reshape(1, LANES)


def kernel(x_prompt, x_sample, cache_k, cache_v, state_conv, state_ssm, page_table, norm_mix, norm_ffn, w_qkv, w_o, w_in_ssm, conv_w, conv_b, dt_bias, a_log, d_skip, norm_ssm, w_out_ssm, w_gate_up, w_down, norm_final):
    n_seq, seq, d_model = x_prompt.shape
    n_dec, n_new, _ = x_sample.shape
    n_heads = d_model // HEAD_DIM
    d_inner = norm_ssm.shape[1]
    ssm_heads = d_inner // SSM_HEAD_DIM
    conv_dim = conv_w.shape[2]
    tm_p = 512
    tm_s = n_dec * SAMPLE_PAD

    hp = x_prompt.reshape(n_seq * seq, d_model)
    hs = jnp.pad(x_sample, ((0, 0), (0, SAMPLE_PAD - n_new), (0, 0))).reshape(tm_s, d_model)

    wqkv = w_qkv[0]
    wq_hi = _bf16_round(wqkv[:, :d_model])
    w_hi = jnp.concatenate([wq_hi, wqkv[:, d_model:]], axis=1).astype(BF16)
    wq_lo = (wqkv[:, :d_model] - wq_hi).astype(BF16)
    wkvt = wqkv[:, d_model:].T.astype(BF16)
    slopes = jnp.exp2(-8.0 * (jnp.arange(n_heads, dtype=F32) + 1.0) / n_heads)
    kx = _key_extras(slopes, seq, seq // MOBA_BLOCK)
    q_p, ka, va, kt_p, vt_p, xm_p = qkv_prompt(hp, norm_mix[0], w_hi, wq_lo, wkvt, kx, n_seq, seq, tm_p)
    q_s, k_s, v_s, _ = qkv_proj(hs, norm_mix[0], w_hi, wq_lo, tm_s)
    kmean = mm_f32(xm_p.reshape(n_seq * seq // MOBA_BLOCK, d_model), wqkv[:, d_model:2 * d_model])
    qa = moba_prep(q_p, kmean, n_seq, seq, n_heads)
    attn_p = moba_attn(qa, ka, va)
    rows = jnp.arange(n_heads * n_new)
    rowinfo = jnp.stack([(rows // n_heads).astype(F32), slopes[rows % n_heads]])
    rowinfo = jnp.broadcast_to(rowinfo[:, :, None], (2, n_heads * n_new, LANES))
    cache_kt = jnp.transpose(cache_k[0], (0, 2, 3, 1))
    cache_vt = jnp.transpose(cache_v[0], (0, 2, 3, 1))
    attn_s = sample_attn(page_table, rowinfo, q_s, k_s, v_s, cache_kt, cache_vt, n_new)
    wo = w_o[0].astype(BF16)
    hp = mm_res(attn_p, wo, hp, tm_p)
    hs = mm_res(attn_s, wo, hs, tm_s)

    k_prompt = jnp.transpose(kt_p, (0, 3, 1, 2))[None]
    v_prompt = jnp.transpose(vt_p, (0, 3, 1, 2))[None]
    k_sample = k_s.reshape(n_dec, SAMPLE_PAD, n_heads, HEAD_DIM)[None, :, :n_new]
    v_sample = v_s.reshape(n_dec, SAMPLE_PAD, n_heads, HEAD_DIM)[None, :, :n_new]

    def ffn(h, layer, tm, final_g=None):
        act = swiglu_gu(h, norm_ffn[layer], w_gate_up[layer].astype(BF16), tm, 256)
        return mm_res(act, w_down[layer].astype(BF16), h, tm, final_g)

    hp = ffn(hp, 0, tm_p)
    hs = ffn(hs, 0, tm_s)

    in_cols = d_inner + conv_dim + ssm_heads
    assert w_in_ssm.shape[2] == in_cols
    proj_w = ((in_cols + 1279) // 1280) * 1280
    w_in = _pad_cols(w_in_ssm[0], proj_w).astype(BF16)
    proj_p = norm_mm(hp, norm_mix[1], w_in, tm_p, 1280)
    proj_s = norm_mm(hs, norm_mix[1], w_in, tm_s, 1280)
    cb = conv_b[0].reshape(1, conv_dim)
    dtb = _pad_lanes(dt_bias[0])
    alog = _pad_lanes(a_log[0])
    dsk = jnp.repeat(d_skip[0], SSM_HEAD_DIM).reshape(1, d_inner)
    gn = norm_ssm[0].reshape(1, d_inner)
    zero_conv = jnp.zeros((n_seq, SUBLANES, conv_dim), F32)
    zero_ssm = jnp.zeros((n_seq, ssm_heads, SSM_HEAD_DIM, D_STATE), F32)
    y_p, conv_p, ssm_p = ssd_mixer(proj_p, zero_conv, zero_ssm, conv_w[0], cb, dtb, alog, dsk, gn,
                                   n_seq, SSD_CHUNK, SSD_CHUNK, d_inner)
    conv_init_s = jnp.pad(state_conv[0], ((0, 0), (SUBLANES - (CONV_W - 1), 0), (0, 0)))
    y_s, conv_s, ssm_s = ssd_mixer(proj_s, conv_init_s, state_ssm[0], conv_w[0], cb, dtb, alog, dsk, gn,
                                   n_dec, SAMPLE_PAD, n_new, d_inner)
    w_out = w_out_ssm[0].astype(BF16)
    hp = mm_res(y_p, w_out, hp, tm_p)
    hs = mm_res(y_s, w_out, hs, tm_s)
    y_prompt = ffn(hp, 1, tm_p, norm_final).reshape(n_seq, seq, d_model)
    y_sample = ffn(hs, 1, tm_s, norm_final).reshape(n_dec, SAMPLE_PAD, d_model)[:, :n_new]

    lo_p = SUBLANES - (CONV_W - 1)
    lo_s = (n_new - (CONV_W - 1)) % SUBLANES
    return (y_prompt, y_sample, k_prompt, v_prompt, k_sample, v_sample,
            conv_p[None, :, lo_p:lo_p + CONV_W - 1], ssm_p[None],
            conv_s[None, :, lo_s:lo_s + CONV_W - 1], ssm_s[None])
```

```python
import functools
import math

import jax
import jax.numpy as jnp
from jax import lax
from jax.experimental import pallas as pl
from jax.experimental.pallas import tpu as pltpu

F32 = jnp.float32
BF16 = jnp.bfloat16

EPS = 1e-5
HEAD_DIM = 64
MOBA_BLOCK = 256
MOBA_TOPK = 3
SSM_HEAD_DIM = 64
N_GROUPS = 8
D_STATE = 128
CONV_W = 4
SSD_CHUNK = 128
SSD_SEQS_PER_STEP = 2
PAGE_SLOTS = 4
SAMPLE_PAD = 16
AUG = 128
NEG_BIAS = -1e30
LOG2E = math.log2(math.e)
ATTN_GROUP = 4
ATTN_KV_BLOCKS = 4
PREP_BLOCKS = 8
LANES = 128
SUBLANES = 8
VMEM_LIMIT = 48 * 1024 * 1024
VMEM_LIMIT_BIG = 56 * 1024 * 1024

_NT = (((1,), (1,)), ((), ()))


def _silu(x):
    return x / (1.0 + jnp.exp(-x))


def _params(sem, vmem=VMEM_LIMIT):
    return pltpu.CompilerParams(dimension_semantics=sem, vmem_limit_bytes=vmem)


def _rms_bf16(x_ref, g_ref):
    x = x_ref[...]
    ms = jnp.mean(x * x, axis=-1, keepdims=True)
    return (x * lax.rsqrt(ms + EPS) * g_ref[...]).astype(BF16)


def _norm_mm_body(x_ref, g_ref, w_ref, o_ref, *, tn):
    xn = _rms_bf16(x_ref, g_ref)
    for c in range(w_ref.shape[1] // tn):
        o_ref[:, c * tn:(c + 1) * tn] = jnp.dot(xn, w_ref[:, c * tn:(c + 1) * tn], preferred_element_type=F32)


def norm_mm(x, g, w, tm, tn):
    m, d = x.shape
    n = w.shape[1]
    assert n % tn == 0
    return pl.pallas_call(
        functools.partial(_norm_mm_body, tn=tn),
        grid=(m // tm,),
        in_specs=[pl.BlockSpec((tm, d), lambda i: (i, 0)),
                  pl.BlockSpec((1, d), lambda i: (0, 0)),
                  pl.BlockSpec((d, n), lambda i: (0, 0), pipeline_mode=pl.Buffered(1))],
        out_specs=pl.BlockSpec((tm, n), lambda i: (i, 0)),
        out_shape=jax.ShapeDtypeStruct((m, n), F32),
        compiler_params=_params(("parallel",), VMEM_LIMIT_BIG),
        name="norm_mm",
    )(x, g.reshape(1, d), w)


def _swiglu_gu_body(x_ref, g_ref, w_ref, o_ref, *, tn):
    xn = _rms_bf16(x_ref, g_ref)
    dff = o_ref.shape[1]
    for c in range(dff // tn):
        gate = jnp.dot(xn, w_ref[:, c * tn:(c + 1) * tn].astype(BF16), preferred_element_type=F32)
        up = jnp.dot(xn, w_ref[:, dff + c * tn:dff + (c + 1) * tn].astype(BF16), preferred_element_type=F32)
        o_ref[:, c * tn:(c + 1) * tn] = (_silu(gate) * up).astype(BF16)


def swiglu_gu(x, g, w_gu, tm, tn):
    m, d = x.shape
    dff = w_gu.shape[1] // 2
    assert dff % tn == 0
    return pl.pallas_call(
        functools.partial(_swiglu_gu_body, tn=tn),
        grid=(m // tm,),
        in_specs=[pl.BlockSpec((tm, d), lambda i: (i, 0)),
                  pl.BlockSpec((1, d), lambda i: (0, 0)),
                  pl.BlockSpec((d, 2 * dff), lambda i: (0, 0), pipeline_mode=pl.Buffered(1))],
        out_specs=pl.BlockSpec((tm, dff), lambda i: (i, 0)),
        out_shape=jax.ShapeDtypeStruct((m, dff), BF16),
        compiler_params=_params(("parallel",)),
        name="swiglu_gu",
    )(x, g.reshape(1, d), w_gu)


def _mm_res_body(a_ref, w_ref, r_ref, o_ref):
    o_ref[...] = r_ref[...] + jnp.dot(a_ref[...], w_ref[...].astype(BF16), preferred_element_type=F32)


def _mm_res_norm_body(a_ref, w_ref, r_ref, g_ref, o_ref):
    y = r_ref[...] + jnp.dot(a_ref[...], w_ref[...].astype(BF16), preferred_element_type=F32)
    ms = jnp.mean(y * y, axis=-1, keepdims=True)
    o_ref[...] = y * lax.rsqrt(ms + EPS) * g_ref[...]


def mm_res(a, w, res, tm, final_g=None):
    m, k = a.shape
    n = w.shape[1]
    in_specs = [pl.BlockSpec((tm, k), lambda i: (i, 0)),
                pl.BlockSpec((k, n), lambda i: (0, 0), pipeline_mode=pl.Buffered(1)),
                pl.BlockSpec((tm, n), lambda i: (i, 0))]
    args = [a, w, res]
    body = _mm_res_body
    if final_g is not None:
        in_specs.append(pl.BlockSpec((1, n), lambda i: (0, 0)))
        args.append(final_g.reshape(1, n))
        body = _mm_res_norm_body
    return pl.pallas_call(
        body,
        grid=(m // tm,),
        in_specs=in_specs,
        out_specs=pl.BlockSpec((tm, n), lambda i: (i, 0)),
        out_shape=jax.ShapeDtypeStruct((m, n), F32),
        compiler_params=_params(("parallel",)),
        name="mm_res_norm" if final_g is not None else "mm_res",
    )(*args)


def _qkv_body(x_ref, g_ref, w_ref, wq_lo_ref, q_ref, k_ref, v_ref, xm_ref):
    d = x_ref.shape[1]
    x = x_ref[...]
    ms = jnp.mean(x * x, axis=-1, keepdims=True)
    xn = x * lax.rsqrt(ms + EPS) * g_ref[...]
    hi = xn.astype(BF16)
    lo = (xn - hi.astype(F32)).astype(BF16)
    wq = w_ref[:, 0:d]
    q_ref[...] = (jnp.dot(hi, wq, preferred_element_type=F32) + jnp.dot(lo, wq, preferred_element_type=F32)
                  + jnp.dot(hi, wq_lo_ref[...], preferred_element_type=F32))
    k_ref[...] = jnp.dot(hi, w_ref[:, d:2 * d], preferred_element_type=F32)
    v_ref[...] = jnp.dot(hi, w_ref[:, 2 * d:3 * d], preferred_element_type=F32)
    for blk in range(xm_ref.shape[0]):
        xm_ref[blk] = jnp.mean(xn[blk * MOBA_BLOCK:(blk + 1) * MOBA_BLOCK], axis=0, keepdims=True)


def qkv_proj(x, g, w, wq_lo, tm):
    m, d = x.shape
    assert tm % MOBA_BLOCK == 0
    row = pl.BlockSpec((tm, d), lambda i: (i, 0))
    out = jax.ShapeDtypeStruct((m, d), F32)
    return pl.pallas_call(
        _qkv_body,
        grid=(m // tm,),
        in_specs=[row, pl.BlockSpec((1, d), lambda i: (0, 0)),
                  pl.BlockSpec((d, 3 * d), lambda i: (0, 0), pipeline_mode=pl.Buffered(1)),
                  pl.BlockSpec((d, d), lambda i: (0, 0), pipeline_mode=pl.Buffered(1))],
        out_specs=[row, row, row, pl.BlockSpec((tm // MOBA_BLOCK, 1, d), lambda i: (i, 0, 0))],
        out_shape=[out, out, out, jax.ShapeDtypeStruct((m // MOBA_BLOCK, 1, d), F32)],
        compiler_params=_params(("parallel",)),
        name="qkv_proj",
    )(x, g.reshape(1, d), w, wq_lo)


def _qkv_prompt_body(x_ref, g_ref, w_ref, wq_lo_ref, wkvt_ref, kx_ref, q_ref, ka_ref, va_ref, kt_ref, vt_ref,
                     xm_ref):
    d = x_ref.shape[1]
    n_heads = ka_ref.shape[1]
    tm = x_ref.shape[0]
    half = AUG // 2
    x = x_ref[...]
    ms = jnp.mean(x * x, axis=-1, keepdims=True)
    xn = x * lax.rsqrt(ms + EPS) * g_ref[...]
    hi = xn.astype(BF16)
    lo = (xn - hi.astype(F32)).astype(BF16)
    wq = w_ref[:, 0:d]
    q_ref[...] = (jnp.dot(hi, wq, preferred_element_type=F32) + jnp.dot(lo, wq, preferred_element_type=F32)
                  + jnp.dot(hi, wq_lo_ref[...], preferred_element_type=F32))
    k = jnp.dot(hi, w_ref[:, d:2 * d], preferred_element_type=F32)
    v = jnp.dot(hi, w_ref[:, 2 * d:3 * d], preferred_element_type=F32)
    lane = lax.broadcasted_iota(jnp.int32, (tm, AUG), 1)
    for pair in range(n_heads // 2):
        kb = k[:, pair * AUG:(pair + 1) * AUG]
        vb = v[:, pair * AUG:(pair + 1) * AUG]
        for hh in range(2):
            h = 2 * pair + hh
            data = (lane < half) if hh == 0 else (lane >= half)
            ones_lane = half if hh == 0 else 0
            ka_ref[0, h] = jnp.where(data, kb, kx_ref[h].astype(F32)).astype(BF16)
            va_ref[0, h] = jnp.where(data, vb, jnp.where(lane == ones_lane, 1.0, 0.0)).astype(BF16)
    kt = lax.dot_general(wkvt_ref[0:d, :], hi, _NT, preferred_element_type=F32)
    vt = lax.dot_general(wkvt_ref[d:2 * d, :], hi, _NT, preferred_element_type=F32)
    kt_ref[0] = kt.reshape(n_heads, HEAD_DIM, tm)
    vt_ref[0] = vt.reshape(n_heads, HEAD_DIM, tm)
    for blk in range(xm_ref.shape[0]):
        xm_ref[blk] = jnp.mean(xn[blk * MOBA_BLOCK:(blk + 1) * MOBA_BLOCK], axis=0, keepdims=True)


def qkv_prompt(x, g, w, wq_lo, wkvt, kx, n_seq, seq, tm):
    m, d = x.shape
    n_heads = d // HEAD_DIM
    assert tm % MOBA_BLOCK == 0 and seq % tm == 0
    steps = seq // tm
    row = pl.BlockSpec((tm, d), lambda i: (i, 0))
    aug_spec = pl.BlockSpec((1, n_heads, tm, AUG), lambda i: (i // steps, 0, i % steps, 0))
    t_spec = pl.BlockSpec((1, n_heads, HEAD_DIM, tm), lambda i: (i // steps, 0, 0, i % steps))
    aug_shape = jax.ShapeDtypeStruct((n_seq, n_heads, seq, AUG), BF16)
    t_shape = jax.ShapeDtypeStruct((n_seq, n_heads, HEAD_DIM, seq), F32)

    def whole(shape):
        return pl.BlockSpec(shape, lambda i: (0,) * len(shape), pipeline_mode=pl.Buffered(1))

    return pl.pallas_call(
        _qkv_prompt_body,
        grid=(m // tm,),
        in_specs=[row, pl.BlockSpec((1, d), lambda i: (0, 0)), whole((d, 3 * d)), whole((d, d)), whole((2 * d, d)),
                  pl.BlockSpec((n_heads, tm, AUG), lambda i: (0, i % steps, 0))],
        out_specs=[row, aug_spec, aug_spec, t_spec, t_spec,
                   pl.BlockSpec((tm // MOBA_BLOCK, 1, d), lambda i: (i, 0, 0))],
        out_shape=[jax.ShapeDtypeStruct((m, d), F32), aug_shape, aug_shape, t_shape, t_shape,
                   jax.ShapeDtypeStruct((m // MOBA_BLOCK, 1, d), F32)],
        compiler_params=_params(("parallel",), VMEM_LIMIT_BIG),
        name="qkv_prompt",
    )(x, g.reshape(1, d), w, wq_lo, wkvt, kx)


def _mm_f32_body(a_ref, w_ref, o_ref):
    o_ref[...] = jnp.dot(a_ref[...], w_ref[...], precision=lax.Precision.HIGHEST, preferred_element_type=F32)


def mm_f32(a, w):
    m, k = a.shape
    n = w.shape[1]
    return pl.pallas_call(
        _mm_f32_body,
        grid=(1,),
        in_specs=[pl.BlockSpec((m, k), lambda i: (0, 0)), pl.BlockSpec((k, n), lambda i: (0, 0))],
        out_specs=pl.BlockSpec((m, n), lambda i: (0, 0)),
        out_shape=jax.ShapeDtypeStruct((m, n), F32),
        compiler_params=_params(("arbitrary",)),
        name="mm_f32",
    )(a, w)


def _topk_select(gate, valid, blk, n_blocks):
    tiles = n_blocks // SUBLANES

    def over_blocks(x, op):
        acc = x[0:SUBLANES]
        for i in range(1, tiles):
            acc = op(acc, x[i * SUBLANES:(i + 1) * SUBLANES])
        shift = SUBLANES // 2
        while shift:
            acc = op(acc, pltpu.roll(acc, shift, axis=0))
            shift //= 2
        return jnp.concatenate([acc] * tiles, axis=0) if tiles > 1 else acc

    g = jnp.where(valid, gate, -jnp.inf)
    taken = jnp.zeros(g.shape, jnp.int32)
    for _ in range(MOBA_TOPK):
        best = over_blocks(g, jnp.maximum)
        first = over_blocks(jnp.where(g == best, blk, n_blocks), jnp.minimum)
        pick = blk == first
        taken = jnp.where(pick, 1, taken)
        g = jnp.where(pick, -jnp.inf, g)
    return jnp.logical_and(valid, taken > 0)


def _bf16_trunc(x):
    bits = lax.bitcast_convert_type(x, jnp.uint32) & jnp.uint32(0xFFFF0000)
    return lax.bitcast_convert_type(bits, F32)


def _bf16_round(x):
    bits = lax.bitcast_convert_type(x, jnp.uint32)
    bits = (bits + jnp.uint32(0x7FFF) + ((bits >> 16) & jnp.uint32(1))) & jnp.uint32(0xFFFF0000)
    return lax.bitcast_convert_type(bits, F32)


def _split3(x):
    hi = _bf16_trunc(x)
    mid = _bf16_trunc(x - hi)
    return hi, mid, x - hi - mid


def _moba_prep_body(q_ref, kmean_ref, qa_ref, *, n_blocks):
    half = AUG // 2
    kmean = kmean_ref[...]
    blk = lax.broadcasted_iota(jnp.int32, (n_blocks, MOBA_BLOCK), 0)
    lane = lax.broadcasted_iota(jnp.int32, (MOBA_BLOCK, AUG), 1)
    arow = lax.broadcasted_iota(jnp.int32, (SUBLANES, MOBA_BLOCK), 0)
    ones_rows = jnp.where(arow < 3, 1.0, 0.0).astype(F32)
    gap = jnp.zeros((half - n_blocks - SUBLANES, MOBA_BLOCK), F32)
    other = jnp.zeros((half, MOBA_BLOCK), F32)
    for j in range(q_ref.shape[0] // MOBA_BLOCK):
        sb = pl.program_id(1) * (q_ref.shape[0] // MOBA_BLOCK) + j
        rows = slice(j * MOBA_BLOCK, (j + 1) * MOBA_BLOCK)
        valid = blk < sb
        q = q_ref[rows, :] * (LOG2E * HEAD_DIM ** -0.5)
        for hh in range(2):
            first = hh == 0
            data = (lane < half) if first else (lane >= half)
            gate = lax.dot_general(kmean[:, hh * half:(hh + 1) * half], q_ref[rows, hh * half:(hh + 1) * half], _NT,
                                   precision=lax.Precision.HIGHEST, preferred_element_type=F32)
            sel = _topk_select(gate, valid, blk, n_blocks)
            bias_t = jnp.where(sel, 0.0, jnp.where(blk == sb, 0.0, NEG_BIAS)).astype(F32)
            extras_t = [bias_t, ones_rows, gap]
            aug = jnp.concatenate([other] + extras_t if first else extras_t + [other], axis=0).T
            qa_ref[0, hh, rows, :] = jnp.where(data, q, aug).astype(BF16)


def _key_extras(slopes, seq, n_blocks):
    shape = (slopes.shape[0], seq, AUG)
    pos = jnp.arange(seq, dtype=F32)
    hi, mid, lo = _split3(pos[None, :] * (slopes * LOG2E)[:, None])
    head = lax.broadcasted_iota(jnp.int32, shape, 0)
    key = lax.broadcasted_iota(jnp.int32, shape, 1)
    lane = lax.broadcasted_iota(jnp.int32, shape, 2)
    e = lane - jnp.where(head % 2 == 0, AUG // 2, 0)
    kx = jnp.where(e == key // MOBA_BLOCK, 1.0, 0.0)
    for i, part in enumerate((hi, mid, lo)):
        kx = jnp.where(e == n_blocks + i, part[:, :, None], kx)
    return kx.astype(BF16)


def moba_prep(q, kmean, n_seq, seq, n_heads):
    n_blocks = seq // MOBA_BLOCK
    assert n_blocks + SUBLANES <= AUG // 2 and n_blocks % SUBLANES == 0
    hpairs = n_heads // 2
    rows = PREP_BLOCKS * MOBA_BLOCK
    n_steps = seq // rows
    assert seq % rows == 0
    return pl.pallas_call(
        functools.partial(_moba_prep_body, n_blocks=n_blocks),
        grid=(hpairs, n_steps, n_seq),
        in_specs=[pl.BlockSpec((rows, LANES), lambda hp, sb, b: (b * n_steps + sb, hp)),
                  pl.BlockSpec((n_blocks, LANES), lambda hp, sb, b: (b, hp))],
        out_specs=pl.BlockSpec((1, 2, rows, AUG), lambda hp, sb, b: (b, hp, sb, 0)),
        out_shape=jax.ShapeDtypeStruct((n_seq, n_heads, seq, AUG), BF16),
        compiler_params=_params(("parallel", "parallel", "parallel")),
        name="moba_prep",
    )(q, kmean)


def _moba_attn_body(qa_ref, ka_ref, va_ref, o_ref, *, n_blocks):
    t = MOBA_BLOCK
    grp = ATTN_GROUP
    rows = grp * t
    kw = ATTN_KV_BLOCKS
    out_lane = lax.broadcasted_iota(jnp.int32, (rows, AUG), 1)

    def update(q, m, acc, hh, first_blk, n_blk, mask=None):
        width = n_blk * t
        start = pl.multiple_of(first_blk * t, t)
        s = lax.dot_general(q, ka_ref[0, hh, pl.ds(start, width), :], _NT, preferred_element_type=F32)
        if mask is not None:
            s = jnp.where(mask, s, -jnp.inf)
        m_new = jnp.maximum(m, jnp.max(s, axis=-1, keepdims=True))
        p = jnp.exp2(s - m_new)
        acc = jnp.exp2(m - m_new) * acc + jnp.dot(p.astype(BF16), va_ref[0, hh, pl.ds(start, width), :],
                                                  preferred_element_type=F32)
        return m_new, acc

    def group(g, carry):
        base = g * grp
        qs = [qa_ref[0, hh, pl.ds(pl.multiple_of(base * t, t), rows), :] for hh in range(2)]

        def past_tiles(kj, state):
            out = []
            for hh in range(2):
                out.extend(update(qs[hh], state[2 * hh], state[2 * hh + 1], hh, kj * kw, kw))
            return tuple(out)

        init = [jnp.full((rows, 1), -jnp.inf, F32), jnp.zeros((rows, AUG), F32)] * 2
        state = list(lax.fori_loop(0, base // kw, past_tiles, tuple(init)))

        rr = lax.broadcasted_iota(jnp.int32, (t, t), 0)
        cc = lax.broadcasted_iota(jnp.int32, (t, t), 1)
        causal = cc <= rr
        outs = []
        half = grp // 2
        for hh in range(2):
            den_lane = HEAD_DIM if hh == 0 else 0
            m_all, acc_all = state[2 * hh], state[2 * hh + 1]
            m_late, acc_late = update(qs[hh][half * t:], m_all[half * t:], acc_all[half * t:], hh, base, half)
            parts = []
            for a in range(grp):
                sl = slice(a * t, (a + 1) * t)
                if a < half:
                    m, acc, first = m_all[sl], acc_all[sl], 0
                else:
                    late = slice((a - half) * t, (a - half + 1) * t)
                    m, acc, first = m_late[late], acc_late[late], half
                for b in range(first, a + 1):
                    m, acc = update(qs[hh][sl], m, acc, hh, base + b, 1, causal if b == a else None)
                parts.append(acc / acc[:, den_lane:den_lane + 1])
            outs.append(jnp.concatenate(parts, axis=0))
        o_ref[pl.ds(pl.multiple_of(base * t, t), rows), :] = jnp.where(
            out_lane < HEAD_DIM, outs[0], outs[1]).astype(BF16)
        return carry

    lax.fori_loop(0, n_blocks // grp, group, 0)


def moba_attn(qa, ka, va):
    n_seq, n_heads, seq, _ = qa.shape
    n_blocks = seq // MOBA_BLOCK
    assert n_blocks % ATTN_GROUP == 0 and ATTN_GROUP % ATTN_KV_BLOCKS == 0
    spec = pl.BlockSpec((1, 2, seq, AUG), lambda b, hp: (b, hp, 0, 0))
    return pl.pallas_call(
        functools.partial(_moba_attn_body, n_blocks=n_blocks),
        grid=(n_seq, n_heads // 2),
        in_specs=[spec, spec, spec],
        out_specs=pl.BlockSpec((seq, LANES), lambda b, hp: (b, hp)),
        out_shape=jax.ShapeDtypeStruct((n_seq * seq, n_heads * HEAD_DIM), BF16),
        compiler_params=_params(("parallel", "parallel")),
        name="moba_attn",
    )(qa, ka, va)


def _sample_attn_body(pt_ref, rowinfo_ref, q_ref, kn_ref, vn_ref, ck_ref, cv_ref, o_ref, qbd_ref, qbd32_ref, s_ref,
                      ksum_ref, acc_ref, knt_ref, vnt_ref, bias_ref, ring_ref, sem_ref, *, n_pages, pps, n_new,
                      n_heads):
    b = pl.program_id(0)
    t = pl.program_id(1)
    tk = n_pages // pps
    page = ck_ref.shape[3]
    d = q_ref.shape[1]
    past = n_pages * page
    n_blocks = past // MOBA_BLOCK
    ppb = MOBA_BLOCK // page
    n_rows = n_heads * n_new
    steps = 2 * tk
    total = pl.num_programs(0) * steps
    step = b * steps + t
    slot = step % PAGE_SLOTS

    def page_copy(src_ref, pool_page, to_slot, i):
        return pltpu.make_async_copy(src_ref.at[pool_page], ring_ref.at[to_slot, i], sem_ref.at[to_slot])

    def fetch(at):
        seq, tt, to_slot = at // steps, at % steps, at % PAGE_SLOTS

        @pl.when(tt < tk)
        def _():
            for i in range(pps):
                page_copy(ck_ref, pt_ref[seq, tt * pps + i], to_slot, i).start()

        @pl.when(tt >= tk)
        def _():
            for i in range(pps):
                page_copy(cv_ref, pt_ref[seq, (tt - tk) * pps + i], to_slot, i).start()

    @pl.when(step == 0)
    def _():
        for ahead in range(PAGE_SLOTS):
            fetch(ahead)

    @pl.when(jnp.logical_and(step > 0, step + PAGE_SLOTS - 1 < total))
    def _():
        fetch(step + PAGE_SLOTS - 1)

    for i in range(pps):
        page_copy(ck_ref, 0, slot, i).wait()

    head_of_col = lax.broadcasted_iota(jnp.int32, (n_heads, d), 1) // HEAD_DIM
    head_of_row = lax.broadcasted_iota(jnp.int32, (n_heads, d), 0)
    own_head = head_of_col == head_of_row

    def key_major(rows_val):
        padded = jnp.concatenate([rows_val, jnp.zeros((LANES - rows_val.shape[0], d), F32)], axis=0)
        return padded.T

    @pl.when(t == 0)
    def _():
        q = q_ref[...]
        qbd = jnp.concatenate([jnp.where(own_head, jnp.broadcast_to(q[n:n + 1, :], (n_heads, d)), 0.0)
                               for n in range(n_new)], axis=0)
        qbd32_ref[0:n_rows, :] = qbd
        qbd32_ref[n_rows:, :] = jnp.zeros((LANES - n_rows, d), F32)
        qbd_ref[...] = (qbd * HEAD_DIM ** -0.5).astype(BF16)
        knt_ref[...] = key_major(kn_ref[...])
        vnt_ref[...] = key_major(vn_ref[...])
        acc_ref[...] = jnp.zeros_like(acc_ref)
        ksum_ref[...] = jnp.zeros_like(ksum_ref)

    @pl.when(t < tk)
    def _():
        qbd = qbd_ref[...]
        lane_i = lax.broadcasted_iota(jnp.int32, (1, LANES), 1)
        for bi in range(pps // ppb):
            ksum = jnp.zeros((d, page), F32)
            for pi in range(ppb):
                i = bi * ppb + pi
                kp = ring_ref[slot, i].reshape(d, page)
                s_ref[t * pps + i] = jnp.dot(qbd, kp.astype(BF16), preferred_element_type=F32)
                ksum = ksum + kp
            onehot = jnp.where(lane_i == t * (pps // ppb) + bi, 1.0, 0.0).astype(F32)
            ksum_ref[...] += jnp.sum(ksum, axis=1, keepdims=True) * onehot

    @pl.when(t == tk - 1)
    def _():
        s_ref[n_pages] = jnp.dot(qbd_ref[...], knt_ref[...].astype(BF16), preferred_element_type=F32)
        gate = jnp.dot(qbd32_ref[...], ksum_ref[...], precision=lax.Precision.HIGHEST,
                       preferred_element_type=F32) * (1.0 / MOBA_BLOCK)
        gate_t = gate.T[:n_blocks]
        blk = lax.broadcasted_iota(jnp.int32, (n_blocks, LANES), 0)
        sel_t = _topk_select(gate_t, blk >= 0, blk, n_blocks)
        sel = jnp.concatenate([jnp.where(sel_t, 1.0, 0.0).astype(F32),
                               jnp.zeros((LANES - n_blocks, LANES), F32)], axis=0).T[:n_rows]
        q_idx = rowinfo_ref[0]
        slope = rowinfo_ref[1]
        lane_f = lax.broadcasted_iota(jnp.int32, (n_rows, LANES), 1).astype(F32)

        for b in range(n_blocks):
            bias_ref[b] = jnp.broadcast_to(jnp.where(sel[:, b:b + 1] > 0.5, 0.0, -jnp.inf), (n_rows, LANES))

        def logits(p):
            dist = (past + q_idx) - (jnp.asarray(p * page, F32) + lane_f)
            return s_ref[p] - slope * dist + bias_ref[p // ppb]

        dist_new = q_idx - lane_f
        s_new = jnp.where((dist_new >= 0) & (lane_f < n_new), s_ref[n_pages] - slope * dist_new, -jnp.inf)
        m = jnp.max(lax.fori_loop(0, n_pages, lambda p, mv: jnp.maximum(mv, logits(p)), s_new, unroll=8),
                    axis=1, keepdims=True)

        def exp_step(p, den):
            e = jnp.exp(logits(p) - m)
            s_ref[p] = e
            return den + e

        e_new = jnp.exp(s_new - m)
        inv = 1.0 / jnp.sum(lax.fori_loop(0, n_pages, exp_step, e_new, unroll=8), axis=1, keepdims=True)
        s_ref[n_pages] = e_new * inv

        def norm_step(p, carry):
            s_ref[p] = s_ref[p] * inv
            return carry

        lax.fori_loop(0, n_pages, norm_step, 0, unroll=8)

    def weighted(p_idx, v_t):
        return lax.dot_general(s_ref[p_idx].astype(BF16), v_t.astype(BF16), _NT, preferred_element_type=F32)

    @pl.when(t >= tk)
    def _():
        acc = acc_ref[...]
        for i in range(pps):
            acc = acc + weighted((t - tk) * pps + i, ring_ref[slot, i].reshape(d, page))
        acc_ref[...] = acc

    @pl.when(t == 2 * tk - 1)
    def _():
        acc = acc_ref[...] + weighted(n_pages, vnt_ref[...])
        outs = [jnp.sum(jnp.where(own_head, acc[n * n_heads:(n + 1) * n_heads, :], 0.0), axis=0, keepdims=True)
                for n in range(n_new)]
        outs.append(jnp.zeros((SAMPLE_PAD - n_new, d), F32))
        o_ref[...] = jnp.concatenate(outs, axis=0).astype(BF16)


def sample_attn(page_table, rowinfo, q, k_new, v_new, cache_kt, cache_vt, n_new, pps=8):
    n_dec, n_pages = page_table.shape
    _, n_heads, _, page = cache_kt.shape
    d = n_heads * HEAD_DIM
    ppb = MOBA_BLOCK // page
    assert n_pages % pps == 0 and pps % ppb == 0 and page == LANES
    assert n_new <= SAMPLE_PAD and n_heads % SUBLANES == 0
    tk = n_pages // pps
    n_blocks = n_pages // ppb
    n_rows = n_heads * n_new
    assert n_blocks <= LANES and n_rows <= LANES and n_dec * 2 * tk >= PAGE_SLOTS

    row_spec = pl.BlockSpec((SAMPLE_PAD, d), lambda b, t, pt: (b, 0))
    pool_spec = pl.BlockSpec(memory_space=pl.ANY)
    grid_spec = pltpu.PrefetchScalarGridSpec(
        num_scalar_prefetch=1,
        grid=(n_dec, 2 * tk),
        in_specs=[pl.BlockSpec((2, n_rows, LANES), lambda b, t, pt: (0, 0, 0)), row_spec, row_spec, row_spec,
                  pool_spec, pool_spec],
        out_specs=row_spec,
        scratch_shapes=[pltpu.VMEM((n_rows, d), BF16),
                        pltpu.VMEM((LANES, d), F32),
                        pltpu.VMEM((n_pages + 1, n_rows, LANES), F32),
                        pltpu.VMEM((d, LANES), F32),
                        pltpu.VMEM((n_rows, d), F32),
                        pltpu.VMEM((d, LANES), F32),
                        pltpu.VMEM((d, LANES), F32),
                        pltpu.VMEM((n_blocks, n_rows, LANES), F32),
                        pltpu.VMEM((PAGE_SLOTS, pps, n_heads, HEAD_DIM, page), F32),
                        pltpu.SemaphoreType.DMA((PAGE_SLOTS,))],
    )
    return pl.pallas_call(
        functools.partial(_sample_attn_body, n_pages=n_pages, pps=pps, n_new=n_new, n_heads=n_heads),
        grid_spec=grid_spec,
        out_shape=jax.ShapeDtypeStruct((n_dec * SAMPLE_PAD, d), BF16),
        compiler_params=_params(("arbitrary", "arbitrary")),
        name="sample_attn",
    )(page_table, rowinfo, q, k_new, v_new, cache_kt, cache_vt)


def _ssd_body(z_ref, xs_ref, bc_ref, dt_ref, cinit_ref, sinit_ref, cw_ref, cb_ref, dtb_ref, alog_ref,
              dsk_ref, gn_ref, y_ref, cout_ref, sout_ref, ext_ref, xc_ref, h_ref, **static):
    for phase in ("start", "main", "end"):
        for p in range(z_ref.shape[0]):
            _ssd_chunk(z_ref.at[p], xs_ref.at[p], bc_ref.at[p], dt_ref.at[p], cinit_ref.at[p], sinit_ref.at[p],
                       cw_ref, cb_ref, dtb_ref, alog_ref, dsk_ref, gn_ref, y_ref.at[p], cout_ref.at[p],
                       sout_ref.at[p], ext_ref.at[p], xc_ref.at[p], h_ref.at[p], phase=phase, **static)


def _ssd_chunk(z_ref, xs_ref, bc_ref, dt_ref, cinit_ref, sinit_ref, cw_ref, cb_ref, dtb_ref, alog_ref,
               dsk_ref, gn_ref, y_ref, cout_ref, sout_ref, ext_ref, xc_ref, h_ref, *, rows, valid, n_chunks,
               phase):
    cl = xc_ref.shape[0]
    c = pl.program_id(1)
    d_inner = xs_ref.shape[1]
    n_heads = d_inner // SSM_HEAD_DIM
    hpg = n_heads // N_GROUPS
    gw = hpg * SSM_HEAD_DIM
    tail = SUBLANES

    if phase == "start":
        @pl.when(c == 0)
        def _():
            ext_ref[0:tail, :] = cinit_ref[...]
            h_ref[...] = sinit_ref[...].reshape(h_ref.shape)
            if rows < cl:
                ext_ref[tail + rows:, :] = jnp.zeros((cl - rows, ext_ref.shape[1]), F32)

        @pl.when(c > 0)
        def _():
            ext_ref[0:tail, :] = ext_ref[cl:cl + tail, :]

        return

    if phase == "end":
        @pl.when(c == n_chunks - 1)
        def _():
            sout_ref[...] = h_ref[...].reshape(sout_ref.shape)
            first = tail + ((valid - (CONV_W - 1)) // SUBLANES) * SUBLANES
            cout_ref[...] = ext_ref[first:first + SUBLANES, :]

        return

    ext_ref[tail:tail + rows, 0:d_inner] = xs_ref[...]
    ext_ref[tail:tail + rows, d_inner:] = bc_ref[...]

    cw = cw_ref[...]
    conv = cb_ref[...] + ext_ref[tail:tail + cl, :] * cw[CONV_W - 1:CONV_W, :]
    for back in range(1, CONV_W):
        conv = conv + ext_ref[tail - back:tail - back + cl, :] * cw[CONV_W - 1 - back:CONV_W - back, :]
    xc_ref[...] = _silu(conv)

    def pad(v):
        if rows == cl:
            return v
        return jnp.concatenate([v, jnp.zeros((cl - rows, v.shape[1]), v.dtype)], axis=0)

    trow = lax.broadcasted_iota(jnp.int32, (cl, LANES), 0)
    dt_raw = pad(dt_ref[...]) + dtb_ref[...]
    dt = jnp.maximum(dt_raw, 0.0) + jnp.log1p(jnp.exp(-jnp.abs(dt_raw)))
    dt = jnp.where(trow < valid, dt, 0.0)
    a = -jnp.exp(alog_ref[...])
    tri_r = lax.broadcasted_iota(jnp.int32, (cl, cl), 0)
    tri_c = lax.broadcasted_iota(jnp.int32, (cl, cl), 1)
    causal = tri_c <= tri_r
    a_cs = jnp.dot(jnp.where(causal, 1.0, 0.0).astype(F32), dt * a, precision=lax.Precision.HIGHEST,
                   preferred_element_type=F32)
    a_cs_t = a_cs.T
    dt_t = dt.T
    z = pad(z_ref[...])
    from_start = jnp.exp(a_cs)
    to_end_all = jnp.exp(a_cs_t[:, cl - 1:cl] - a_cs_t) * dt_t

    for g in range(N_GROUPS):
        bg = xc_ref[:, d_inner + g * D_STATE:d_inner + (g + 1) * D_STATE].astype(BF16)
        cg = xc_ref[:, d_inner + (N_GROUPS + g) * D_STATE:d_inner + (N_GROUPS + g + 1) * D_STATE].astype(BF16)
        cb = lax.dot_general(cg, bg, _NT, preferred_element_type=F32)
        xg = xc_ref[:, g * gw:(g + 1) * gw]
        xg_t = xg.T
        h_old = h_ref[g * gw:(g + 1) * gw, :]
        y_off = lax.dot_general(cg, h_old.astype(BF16), _NT, preferred_element_type=F32)
        ys, xw_rows, h_scaled = [], [], []
        for e in range(hpg):
            h = g * hpg + e
            acs_col = a_cs[:, h:h + 1]
            acs_row = a_cs_t[h:h + 1, :]
            dt_row = dt_t[h:h + 1, :]
            decay = jnp.exp(jnp.where(causal, acs_col - acs_row, -jnp.inf))
            w = (cb * decay * dt_row).astype(BF16)
            xh = xg[:, e * SSM_HEAD_DIM:(e + 1) * SSM_HEAD_DIM]
            y_diag = jnp.dot(w, xh.astype(BF16), preferred_element_type=F32)
            ys.append(y_diag + y_off[:, e * SSM_HEAD_DIM:(e + 1) * SSM_HEAD_DIM] * from_start[:, h:h + 1])
            xw_rows.append(xg_t[e * SSM_HEAD_DIM:(e + 1) * SSM_HEAD_DIM, :] * to_end_all[h:h + 1, :])
            h_scaled.append(h_old[e * SSM_HEAD_DIM:(e + 1) * SSM_HEAD_DIM, :] * from_start[cl - 1:cl, h:h + 1])
        states = jnp.dot(jnp.concatenate(xw_rows, axis=0).astype(BF16), bg, preferred_element_type=F32)
        h_ref[g * gw:(g + 1) * gw, :] = jnp.concatenate(h_scaled, axis=0) + states
        y = jnp.concatenate(ys, axis=1) + dsk_ref[:, g * gw:(g + 1) * gw] * xg
        yz = y * _silu(z[:, g * gw:(g + 1) * gw])
        ms = jnp.mean(yz * yz, axis=-1, keepdims=True)
        yn = yz * lax.rsqrt(ms + EPS) * gn_ref[:, g * gw:(g + 1) * gw]
        y_ref[:, g * gw:(g + 1) * gw] = yn[:rows].astype(BF16)


def ssd_mixer(proj, conv_init, ssm_init, cw, cb, dtb, alog, dsk, gnorm, n_seq, rows, valid, d_inner):
    total_rows = proj.shape[0] // n_seq
    n_chunks = total_rows // rows
    assert rows == SSD_CHUNK or n_chunks == 1
    chunk = rows if n_chunks == 1 else SSD_CHUNK
    assert CONV_W - 1 <= valid <= rows
    n_heads = d_inner // SSM_HEAD_DIM
    conv_dim = d_inner + 2 * N_GROUPS * D_STATE
    assert conv_dim == 2 * d_inner

    par = SSD_SEQS_PER_STEP
    assert n_seq % par == 0
    proj3 = proj.reshape(n_seq, total_rows, proj.shape[1])

    def col(width, idx):
        return pl.BlockSpec((par, rows, width), lambda b, c: (b, c, idx))

    def const(shape):
        return pl.BlockSpec(shape, lambda b, c: (0,) * len(shape))

    def per_seq(shape):
        return pl.BlockSpec((par,) + shape, lambda b, c: (b,) + (0,) * len(shape))

    y, conv_out, ssm_out = pl.pallas_call(
        functools.partial(_ssd_body, rows=rows, valid=valid, n_chunks=n_chunks),
        grid=(n_seq // par, n_chunks),
        in_specs=[col(d_inner, 0), col(d_inner, 1), col(d_inner, 2), col(LANES, 3 * d_inner // LANES),
                  per_seq((SUBLANES, conv_dim)), per_seq((n_heads, SSM_HEAD_DIM, D_STATE)),
                  const((CONV_W, conv_dim)), const((1, conv_dim)), const((1, LANES)), const((1, LANES)),
                  const((1, d_inner)), const((1, d_inner))],
        out_specs=[col(d_inner, 0), per_seq((SUBLANES, conv_dim)), per_seq((n_heads, SSM_HEAD_DIM, D_STATE))],
        out_shape=[jax.ShapeDtypeStruct((n_seq, total_rows, d_inner), BF16),
                   jax.ShapeDtypeStruct((n_seq, SUBLANES, conv_dim), F32),
                   jax.ShapeDtypeStruct((n_seq, n_heads, SSM_HEAD_DIM, D_STATE), F32)],
        scratch_shapes=[pltpu.VMEM((par, SUBLANES + chunk, conv_dim), F32),
                        pltpu.VMEM((par, chunk, conv_dim), F32),
                        pltpu.VMEM((par, n_heads * SSM_HEAD_DIM, D_STATE), F32)],
        compiler_params=_params(("parallel", "arbitrary")),
        name="ssd_mixer",
    )(proj3, proj3, proj3, proj3, conv_init, ssm_init, cw, cb, dtb, alog, dsk, gnorm)
    return y.reshape(n_seq * total_rows, d_inner), conv_out, ssm_out


def _pad_cols(w, n):
    return jnp.pad(w, ((0, 0), (0, n - w.shape[1])))


def _pad_lanes(v):
    return jnp.pad(v, (0, LANES - v.shape[0])).reshape(1, LANES)


def kernel(x_prompt, x_sample, cache_k, cache_v, state_conv, state_ssm, page_table, norm_mix, norm_ffn, w_qkv, w_o, w_in_ssm, conv_w, conv_b, dt_bias, a_log, d_skip, norm_ssm, w_out_ssm, w_gate_up, w_down, norm_final):
    n_seq, seq, d_model = x_prompt.shape
    n_dec, n_new, _ = x_sample.shape
    n_heads = d_model // HEAD_DIM
    d_inner = norm_ssm.shape[1]
    ssm_heads = d_inner // SSM_HEAD_DIM
    conv_dim = conv_w.shape[2]
    tm_p = 512
    tm_s = n_dec * SAMPLE_PAD

    hp = x_prompt.reshape(n_seq * seq, d_model)
    hs = jnp.pad(x_sample, ((0, 0), (0, SAMPLE_PAD - n_new), (0, 0))).reshape(tm_s, d_model)

    wqkv = w_qkv[0]
    wq_hi = _bf16_round(wqkv[:, :d_model])
    w_hi = jnp.concatenate([wq_hi, wqkv[:, d_model:]], axis=1).astype(BF16)
    wq_lo = (wqkv[:, :d_model] - wq_hi).astype(BF16)
    wkvt = wqkv[:, d_model:].T.astype(BF16)
    slopes = jnp.exp2(-8.0 * (jnp.arange(n_heads, dtype=F32) + 1.0) / n_heads)
    kx = _key_extras(slopes, seq, seq // MOBA_BLOCK)
    q_p, ka, va, kt_p, vt_p, xm_p = qkv_prompt(hp, norm_mix[0], w_hi, wq_lo, wkvt, kx, n_seq, seq, tm_p)
    q_s, k_s, v_s, _ = qkv_proj(hs, norm_mix[0], w_hi, wq_lo, tm_s)
    kmean = mm_f32(xm_p.reshape(n_seq * seq // MOBA_BLOCK, d_model), wqkv[:, d_model:2 * d_model])
    qa = moba_prep(q_p, kmean, n_seq, seq, n_heads)
    attn_p = moba_attn(qa, ka, va)
    rows = jnp.arange(n_heads * n_new)
    rowinfo = jnp.stack([(rows // n_heads).astype(F32), slopes[rows % n_heads]])
    rowinfo = jnp.broadcast_to(rowinfo[:, :, None], (2, n_heads * n_new, LANES))
    cache_kt = jnp.transpose(cache_k[0], (0, 2, 3, 1))
    cache_vt = jnp.transpose(cache_v[0], (0, 2, 3, 1))
    attn_s = sample_attn(page_table, rowinfo, q_s, k_s, v_s, cache_kt, cache_vt, n_new)
    hp = mm_res(attn_p, w_o[0], hp, tm_p)
    hs = mm_res(attn_s, w_o[0], hs, tm_s)

    k_prompt = jnp.transpose(kt_p, (0, 3, 1, 2))[None]
    v_prompt = jnp.transpose(vt_p, (0, 3, 1, 2))[None]
    k_sample = k_s.reshape(n_dec, SAMPLE_PAD, n_heads, HEAD_DIM)[None, :, :n_new]
    v_sample = v_s.reshape(n_dec, SAMPLE_PAD, n_heads, HEAD_DIM)[None, :, :n_new]

    def ffn(h, layer, tm, final_g=None):
        act = swiglu_gu(h, norm_ffn[layer], w_gate_up[layer], tm, 256)
        return mm_res(act, w_down[layer], h, tm, final_g)

    hp = ffn(hp, 0, tm_p)
    hs = ffn(hs, 0, tm_s)

    in_cols = d_inner + conv_dim + ssm_heads
    assert w_in_ssm.shape[2] == in_cols
    proj_w = ((in_cols + 1279) // 1280) * 1280
    w_in = _pad_cols(w_in_ssm[0], proj_w).astype(BF16)
    proj_p = norm_mm(hp, norm_mix[1], w_in, tm_p, 1280)
    proj_s = norm_mm(hs, norm_mix[1], w_in, tm_s, 1280)
    cb = conv_b[0].reshape(1, conv_dim)
    dtb = _pad_lanes(dt_bias[0])
    alog = _pad_lanes(a_log[0])
    dsk = jnp.repeat(d_skip[0], SSM_HEAD_DIM).reshape(1, d_inner)
    gn = norm_ssm[0].reshape(1, d_inner)
    zero_conv = jnp.zeros((n_seq, SUBLANES, conv_dim), F32)
    zero_ssm = jnp.zeros((n_seq, ssm_heads, SSM_HEAD_DIM, D_STATE), F32)
    y_p, conv_p, ssm_p = ssd_mixer(proj_p, zero_conv, zero_ssm, conv_w[0], cb, dtb, alog, dsk, gn,
                                   n_seq, SSD_CHUNK, SSD_CHUNK, d_inner)
    conv_init_s = jnp.pad(state_conv[0], ((0, 0), (SUBLANES - (CONV_W - 1), 0), (0, 0)))
    y_s, conv_s, ssm_s = ssd_mixer(proj_s, conv_init_s, state_ssm[0], conv_w[0], cb, dtb, alog, dsk, gn,
                                   n_dec, SAMPLE_PAD, n_new, d_inner)
    hp = mm_res(y_p, w_out_ssm[0], hp, tm_p)
    hs = mm_res(y_s, w_out_ssm[0], hs, tm_s)
    y_prompt = ffn(hp, 1, tm_p, norm_final).reshape(n_seq, seq, d_model)
    y_sample = ffn(hs, 1, tm_s, norm_final).reshape(n_dec, SAMPLE_PAD, d_model)[:, :n_new]

    lo_p = SUBLANES - (CONV_W - 1)
    lo_s = (n_new - (CONV_W - 1)) % SUBLANES
    return (y_prompt, y_sample, k_prompt, v_prompt, k_sample, v_sample,
            conv_p[None, :, lo_p:lo_p + CONV_W - 1], ssm_p[None],
            conv_s[None, :, lo_s:lo_s + CONV_W - 1], ssm_s[None])
```

```python
import functools
import math

import jax
import jax.numpy as jnp
from jax import lax
from jax.experimental import pallas as pl
from jax.experimental.pallas import tpu as pltpu

F32 = jnp.float32
BF16 = jnp.bfloat16

EPS = 1e-5
HEAD_DIM = 64
MOBA_BLOCK = 256
MOBA_TOPK = 3
SSM_HEAD_DIM = 64
N_GROUPS = 8
D_STATE = 128
CONV_W = 4
SSD_CHUNK = 128
SSD_SEQS_PER_STEP = 2
PAGE_SLOTS = 4
SAMPLE_PAD = 16
AUG = 128
NEG_BIAS = -1e30
LOG2E = math.log2(math.e)
ATTN_GROUP = 4
ATTN_KV_BLOCKS = 4
PREP_BLOCKS = 8
LANES = 128
SUBLANES = 8
VMEM_LIMIT = 48 * 1024 * 1024
VMEM_LIMIT_BIG = 56 * 1024 * 1024

_NT = (((1,), (1,)), ((), ()))


def _silu(x):
    return x / (1.0 + jnp.exp(-x))


def _params(sem, vmem=VMEM_LIMIT):
    return pltpu.CompilerParams(dimension_semantics=sem, vmem_limit_bytes=vmem)


def _rms_bf16(x_ref, g_ref):
    x = x_ref[...]
    ms = jnp.mean(x * x, axis=-1, keepdims=True)
    return (x * lax.rsqrt(ms + EPS) * g_ref[...]).astype(BF16)


def _norm_mm_body(x_ref, g_ref, w_ref, o_ref, *, tn):
    xn = _rms_bf16(x_ref, g_ref)
    for c in range(w_ref.shape[1] // tn):
        o_ref[:, c * tn:(c + 1) * tn] = jnp.dot(xn, w_ref[:, c * tn:(c + 1) * tn], preferred_element_type=F32)


def norm_mm(x, g, w, tm, tn):
    m, d = x.shape
    n = w.shape[1]
    assert n % tn == 0
    return pl.pallas_call(
        functools.partial(_norm_mm_body, tn=tn),
        grid=(m // tm,),
        in_specs=[pl.BlockSpec((tm, d), lambda i: (i, 0)),
                  pl.BlockSpec((1, d), lambda i: (0, 0)),
                  pl.BlockSpec((d, n), lambda i: (0, 0), pipeline_mode=pl.Buffered(1))],
        out_specs=pl.BlockSpec((tm, n), lambda i: (i, 0)),
        out_shape=jax.ShapeDtypeStruct((m, n), F32),
        compiler_params=_params(("parallel",), VMEM_LIMIT_BIG),
        name="norm_mm",
    )(x, g.reshape(1, d), w)


def _swiglu_gu_body(x_ref, g_ref, w_ref, o_ref, *, tn):
    xn = _rms_bf16(x_ref, g_ref)
    dff = o_ref.shape[1]
    for c in range(dff // tn):
        gate = jnp.dot(xn, w_ref[0, :, c * tn:(c + 1) * tn].astype(BF16), preferred_element_type=F32)
        up = jnp.dot(xn, w_ref[0, :, dff + c * tn:dff + (c + 1) * tn].astype(BF16), preferred_element_type=F32)
        o_ref[:, c * tn:(c + 1) * tn] = (_silu(gate) * up).astype(BF16)


def swiglu_gu(x, g, w_gu, layer, tm, tn):
    m, d = x.shape
    dff = w_gu.shape[2] // 2
    assert dff % tn == 0
    return pl.pallas_call(
        functools.partial(_swiglu_gu_body, tn=tn),
        grid=(m // tm,),
        in_specs=[pl.BlockSpec((tm, d), lambda i: (i, 0)),
                  pl.BlockSpec((1, d), lambda i: (0, 0)),
                  pl.BlockSpec((1, d, 2 * dff), lambda i: (layer, 0, 0), pipeline_mode=pl.Buffered(1))],
        out_specs=pl.BlockSpec((tm, dff), lambda i: (i, 0)),
        out_shape=jax.ShapeDtypeStruct((m, dff), BF16),
        compiler_params=_params(("parallel",)),
        name="swiglu_gu",
    )(x, g.reshape(1, d), w_gu)


def _mm_res_body(a_ref, w_ref, r_ref, o_ref):
    o_ref[...] = r_ref[...] + jnp.dot(a_ref[...], w_ref[0].astype(BF16), preferred_element_type=F32)


def _mm_res_norm_body(a_ref, w_ref, r_ref, g_ref, o_ref):
    y = r_ref[...] + jnp.dot(a_ref[...], w_ref[0].astype(BF16), preferred_element_type=F32)
    ms = jnp.mean(y * y, axis=-1, keepdims=True)
    o_ref[...] = y * lax.rsqrt(ms + EPS) * g_ref[...]


def mm_res(a, w, layer, res, tm, final_g=None):
    m, k = a.shape
    n = w.shape[2]
    in_specs = [pl.BlockSpec((tm, k), lambda i: (i, 0)),
                pl.BlockSpec((1, k, n), lambda i: (layer, 0, 0), pipeline_mode=pl.Buffered(1)),
                pl.BlockSpec((tm, n), lambda i: (i, 0))]
    args = [a, w, res]
    body = _mm_res_body
    if final_g is not None:
        in_specs.append(pl.BlockSpec((1, n), lambda i: (0, 0)))
        args.append(final_g.reshape(1, n))
        body = _mm_res_norm_body
    return pl.pallas_call(
        body,
        grid=(m // tm,),
        in_specs=in_specs,
        out_specs=pl.BlockSpec((tm, n), lambda i: (i, 0)),
        out_shape=jax.ShapeDtypeStruct((m, n), F32),
        compiler_params=_params(("parallel",)),
        name="mm_res_norm" if final_g is not None else "mm_res",
    )(*args)


def _qkv_body(x_ref, g_ref, w_ref, wq_lo_ref, q_ref, k_ref, v_ref, xm_ref):
    d = x_ref.shape[1]
    x = x_ref[...]
    ms = jnp.mean(x * x, axis=-1, keepdims=True)
    xn = x * lax.rsqrt(ms + EPS) * g_ref[...]
    hi = xn.astype(BF16)
    lo = (xn - hi.astype(F32)).astype(BF16)
    wq = w_ref[:, 0:d]
    q_ref[...] = (jnp.dot(hi, wq, preferred_element_type=F32) + jnp.dot(lo, wq, preferred_element_type=F32)
                  + jnp.dot(hi, wq_lo_ref[...], preferred_element_type=F32))
    k_ref[...] = jnp.dot(hi, w_ref[:, d:2 * d], preferred_element_type=F32)
    v_ref[...] = jnp.dot(hi, w_ref[:, 2 * d:3 * d], preferred_element_type=F32)
    for blk in range(xm_ref.shape[0]):
        xm_ref[blk] = jnp.mean(xn[blk * MOBA_BLOCK:(blk + 1) * MOBA_BLOCK], axis=0, keepdims=True)


def qkv_proj(x, g, w, wq_lo, tm):
    m, d = x.shape
    assert tm % MOBA_BLOCK == 0
    row = pl.BlockSpec((tm, d), lambda i: (i, 0))
    out = jax.ShapeDtypeStruct((m, d), F32)
    return pl.pallas_call(
        _qkv_body,
        grid=(m // tm,),
        in_specs=[row, pl.BlockSpec((1, d), lambda i: (0, 0)),
                  pl.BlockSpec((d, 3 * d), lambda i: (0, 0), pipeline_mode=pl.Buffered(1)),
                  pl.BlockSpec((d, d), lambda i: (0, 0), pipeline_mode=pl.Buffered(1))],
        out_specs=[row, row, row, pl.BlockSpec((tm // MOBA_BLOCK, 1, d), lambda i: (i, 0, 0))],
        out_shape=[out, out, out, jax.ShapeDtypeStruct((m // MOBA_BLOCK, 1, d), F32)],
        compiler_params=_params(("parallel",)),
        name="qkv_proj",
    )(x, g.reshape(1, d), w, wq_lo)


def _qkv_prompt_body(x_ref, g_ref, w_ref, wq_lo_ref, wkvt_ref, kx_ref, q_ref, ka_ref, va_ref, kt_ref, vt_ref,
                     xm_ref):
    d = x_ref.shape[1]
    n_heads = ka_ref.shape[1]
    tm = x_ref.shape[0]
    half = AUG // 2
    x = x_ref[...]
    ms = jnp.mean(x * x, axis=-1, keepdims=True)
    xn = x * lax.rsqrt(ms + EPS) * g_ref[...]
    hi = xn.astype(BF16)
    lo = (xn - hi.astype(F32)).astype(BF16)
    wq = w_ref[:, 0:d]
    q_ref[...] = (jnp.dot(hi, wq, preferred_element_type=F32) + jnp.dot(lo, wq, preferred_element_type=F32)
                  + jnp.dot(hi, wq_lo_ref[...], preferred_element_type=F32))
    k = jnp.dot(hi, w_ref[:, d:2 * d], preferred_element_type=F32)
    v = jnp.dot(hi, w_ref[:, 2 * d:3 * d], preferred_element_type=F32)
    lane = lax.broadcasted_iota(jnp.int32, (tm, AUG), 1)
    for pair in range(n_heads // 2):
        kb = k[:, pair * AUG:(pair + 1) * AUG]
        vb = v[:, pair * AUG:(pair + 1) * AUG]
        for hh in range(2):
            h = 2 * pair + hh
            data = (lane < half) if hh == 0 else (lane >= half)
            ones_lane = half if hh == 0 else 0
            ka_ref[0, h] = jnp.where(data, kb, kx_ref[h].astype(F32)).astype(BF16)
            va_ref[0, h] = jnp.where(data, vb, jnp.where(lane == ones_lane, 1.0, 0.0)).astype(BF16)
    kt = lax.dot_general(wkvt_ref[0:d, :], hi, _NT, preferred_element_type=F32)
    vt = lax.dot_general(wkvt_ref[d:2 * d, :], hi, _NT, preferred_element_type=F32)
    kt_ref[0] = kt.reshape(n_heads, HEAD_DIM, tm)
    vt_ref[0] = vt.reshape(n_heads, HEAD_DIM, tm)
    for blk in range(xm_ref.shape[0]):
        xm_ref[blk] = jnp.mean(xn[blk * MOBA_BLOCK:(blk + 1) * MOBA_BLOCK], axis=0, keepdims=True)


def qkv_prompt(x, g, w, wq_lo, wkvt, kx, n_seq, seq, tm):
    m, d = x.shape
    n_heads = d // HEAD_DIM
    assert tm % MOBA_BLOCK == 0 and seq % tm == 0
    steps = seq // tm
    row = pl.BlockSpec((tm, d), lambda i: (i, 0))
    aug_spec = pl.BlockSpec((1, n_heads, tm, AUG), lambda i: (i // steps, 0, i % steps, 0))
    t_spec = pl.BlockSpec((1, n_heads, HEAD_DIM, tm), lambda i: (i // steps, 0, 0, i % steps))
    aug_shape = jax.ShapeDtypeStruct((n_seq, n_heads, seq, AUG), BF16)
    t_shape = jax.ShapeDtypeStruct((n_seq, n_heads, HEAD_DIM, seq), F32)

    def whole(shape):
        return pl.BlockSpec(shape, lambda i: (0,) * len(shape), pipeline_mode=pl.Buffered(1))

    return pl.pallas_call(
        _qkv_prompt_body,
        grid=(m // tm,),
        in_specs=[row, pl.BlockSpec((1, d), lambda i: (0, 0)), whole((d, 3 * d)), whole((d, d)), whole((2 * d, d)),
                  pl.BlockSpec((n_heads, tm, AUG), lambda i: (0, i % steps, 0))],
        out_specs=[row, aug_spec, aug_spec, t_spec, t_spec,
                   pl.BlockSpec((tm // MOBA_BLOCK, 1, d), lambda i: (i, 0, 0))],
        out_shape=[jax.ShapeDtypeStruct((m, d), F32), aug_shape, aug_shape, t_shape, t_shape,
                   jax.ShapeDtypeStruct((m // MOBA_BLOCK, 1, d), F32)],
        compiler_params=_params(("parallel",), VMEM_LIMIT_BIG),
        name="qkv_prompt",
    )(x, g.reshape(1, d), w, wq_lo, wkvt, kx)


def _mm_f32_body(a_ref, w_ref, o_ref):
    o_ref[...] = jnp.dot(a_ref[...], w_ref[...], precision=lax.Precision.HIGHEST, preferred_element_type=F32)


def mm_f32(a, w):
    m, k = a.shape
    n = w.shape[1]
    return pl.pallas_call(
        _mm_f32_body,
        grid=(1,),
        in_specs=[pl.BlockSpec((m, k), lambda i: (0, 0)), pl.BlockSpec((k, n), lambda i: (0, 0))],
        out_specs=pl.BlockSpec((m, n), lambda i: (0, 0)),
        out_shape=jax.ShapeDtypeStruct((m, n), F32),
        compiler_params=_params(("arbitrary",)),
        name="mm_f32",
    )(a, w)


def _topk_select(gate, valid, blk, n_blocks):
    tiles = n_blocks // SUBLANES

    def over_blocks(x, op):
        acc = x[0:SUBLANES]
        for i in range(1, tiles):
            acc = op(acc, x[i * SUBLANES:(i + 1) * SUBLANES])
        shift = SUBLANES // 2
        while shift:
            acc = op(acc, pltpu.roll(acc, shift, axis=0))
            shift //= 2
        return jnp.concatenate([acc] * tiles, axis=0) if tiles > 1 else acc

    g = jnp.where(valid, gate, -jnp.inf)
    taken = jnp.zeros(g.shape, jnp.int32)
    for _ in range(MOBA_TOPK):
        best = over_blocks(g, jnp.maximum)
        first = over_blocks(jnp.where(g == best, blk, n_blocks), jnp.minimum)
        pick = blk == first
        taken = jnp.where(pick, 1, taken)
        g = jnp.where(pick, -jnp.inf, g)
    return jnp.logical_and(valid, taken > 0)


def _bf16_trunc(x):
    bits = lax.bitcast_convert_type(x, jnp.uint32) & jnp.uint32(0xFFFF0000)
    return lax.bitcast_convert_type(bits, F32)


def _bf16_round(x):
    bits = lax.bitcast_convert_type(x, jnp.uint32)
    bits = (bits + jnp.uint32(0x7FFF) + ((bits >> 16) & jnp.uint32(1))) & jnp.uint32(0xFFFF0000)
    return lax.bitcast_convert_type(bits, F32)


def _split3(x):
    hi = _bf16_trunc(x)
    mid = _bf16_trunc(x - hi)
    return hi, mid, x - hi - mid


def _moba_prep_body(q_ref, kmean_ref, qa_ref, *, n_blocks):
    half = AUG // 2
    kmean = kmean_ref[...]
    blk = lax.broadcasted_iota(jnp.int32, (n_blocks, MOBA_BLOCK), 0)
    lane = lax.broadcasted_iota(jnp.int32, (MOBA_BLOCK, AUG), 1)
    arow = lax.broadcasted_iota(jnp.int32, (SUBLANES, MOBA_BLOCK), 0)
    ones_rows = jnp.where(arow < 3, 1.0, 0.0).astype(F32)
    gap = jnp.zeros((half - n_blocks - SUBLANES, MOBA_BLOCK), F32)
    other = jnp.zeros((half, MOBA_BLOCK), F32)
    for j in range(q_ref.shape[0] // MOBA_BLOCK):
        sb = pl.program_id(1) * (q_ref.shape[0] // MOBA_BLOCK) + j
        rows = slice(j * MOBA_BLOCK, (j + 1) * MOBA_BLOCK)
        valid = blk < sb
        q = q_ref[rows, :] * (LOG2E * HEAD_DIM ** -0.5)
        for hh in range(2):
            first = hh == 0
            data = (lane < half) if first else (lane >= half)
            gate = lax.dot_general(kmean[:, hh * half:(hh + 1) * half], q_ref[rows, hh * half:(hh + 1) * half], _NT,
                                   precision=lax.Precision.HIGHEST, preferred_element_type=F32)
            sel = _topk_select(gate, valid, blk, n_blocks)
            bias_t = jnp.where(sel, 0.0, jnp.where(blk == sb, 0.0, NEG_BIAS)).astype(F32)
            extras_t = [bias_t, ones_rows, gap]
            aug = jnp.concatenate([other] + extras_t if first else extras_t + [other], axis=0).T
            qa_ref[0, hh, rows, :] = jnp.where(data, q, aug).astype(BF16)


def _key_extras(slopes, seq, n_blocks):
    shape = (slopes.shape[0], seq, AUG)
    pos = jnp.arange(seq, dtype=F32)
    hi, mid, lo = _split3(pos[None, :] * (slopes * LOG2E)[:, None])
    head = lax.broadcasted_iota(jnp.int32, shape, 0)
    key = lax.broadcasted_iota(jnp.int32, shape, 1)
    lane = lax.broadcasted_iota(jnp.int32, shape, 2)
    e = lane - jnp.where(head % 2 == 0, AUG // 2, 0)
    kx = jnp.where(e == key // MOBA_BLOCK, 1.0, 0.0)
    for i, part in enumerate((hi, mid, lo)):
        kx = jnp.where(e == n_blocks + i, part[:, :, None], kx)
    return kx.astype(BF16)


def moba_prep(q, kmean, n_seq, seq, n_heads):
    n_blocks = seq // MOBA_BLOCK
    assert n_blocks + SUBLANES <= AUG // 2 and n_blocks % SUBLANES == 0
    hpairs = n_heads // 2
    rows = PREP_BLOCKS * MOBA_BLOCK
    n_steps = seq // rows
    assert seq % rows == 0
    return pl.pallas_call(
        functools.partial(_moba_prep_body, n_blocks=n_blocks),
        grid=(hpairs, n_steps, n_seq),
        in_specs=[pl.BlockSpec((rows, LANES), lambda hp, sb, b: (b * n_steps + sb, hp)),
                  pl.BlockSpec((n_blocks, LANES), lambda hp, sb, b: (b, hp))],
        out_specs=pl.BlockSpec((1, 2, rows, AUG), lambda hp, sb, b: (b, hp, sb, 0)),
        out_shape=jax.ShapeDtypeStruct((n_seq, n_heads, seq, AUG), BF16),
        compiler_params=_params(("parallel", "parallel", "parallel")),
        name="moba_prep",
    )(q, kmean)


def _moba_attn_body(qa_ref, ka_ref, va_ref, o_ref, *, n_blocks):
    t = MOBA_BLOCK
    grp = ATTN_GROUP
    rows = grp * t
    kw = ATTN_KV_BLOCKS
    out_lane = lax.broadcasted_iota(jnp.int32, (rows, AUG), 1)

    def update(q, m, acc, hh, first_blk, n_blk, mask=None):
        width = n_blk * t
        start = pl.multiple_of(first_blk * t, t)
        s = lax.dot_general(q, ka_ref[0, hh, pl.ds(start, width), :], _NT, preferred_element_type=F32)
        if mask is not None:
            s = jnp.where(mask, s, -jnp.inf)
        m_new = jnp.maximum(m, jnp.max(s, axis=-1, keepdims=True))
        p = jnp.exp2(s - m_new)
        acc = jnp.exp2(m - m_new) * acc + jnp.dot(p.astype(BF16), va_ref[0, hh, pl.ds(start, width), :],
                                                  preferred_element_type=F32)
        return m_new, acc

    def group(g, carry):
        base = g * grp
        qs = [qa_ref[0, hh, pl.ds(pl.multiple_of(base * t, t), rows), :] for hh in range(2)]

        def past_tiles(kj, state):
            out = []
            for hh in range(2):
                out.extend(update(qs[hh], state[2 * hh], state[2 * hh + 1], hh, kj * kw, kw))
            return tuple(out)

        init = [jnp.full((rows, 1), -jnp.inf, F32), jnp.zeros((rows, AUG), F32)] * 2
        state = list(lax.fori_loop(0, base // kw, past_tiles, tuple(init)))

        rr = lax.broadcasted_iota(jnp.int32, (t, t), 0)
        cc = lax.broadcasted_iota(jnp.int32, (t, t), 1)
        causal = cc <= rr
        outs = []
        half = grp // 2
        for hh in range(2):
            den_lane = HEAD_DIM if hh == 0 else 0
            m_all, acc_all = state[2 * hh], state[2 * hh + 1]
            m_late, acc_late = update(qs[hh][half * t:], m_all[half * t:], acc_all[half * t:], hh, base, half)
            parts = []
            for a in range(grp):
                sl = slice(a * t, (a + 1) * t)
                if a < half:
                    m, acc, first = m_all[sl], acc_all[sl], 0
                else:
                    late = slice((a - half) * t, (a - half + 1) * t)
                    m, acc, first = m_late[late], acc_late[late], half
                for b in range(first, a + 1):
                    m, acc = update(qs[hh][sl], m, acc, hh, base + b, 1, causal if b == a else None)
                parts.append(acc / acc[:, den_lane:den_lane + 1])
            outs.append(jnp.concatenate(parts, axis=0))
        o_ref[pl.ds(pl.multiple_of(base * t, t), rows), :] = jnp.where(
            out_lane < HEAD_DIM, outs[0], outs[1]).astype(BF16)
        return carry

    lax.fori_loop(0, n_blocks // grp, group, 0)


def moba_attn(qa, ka, va):
    n_seq, n_heads, seq, _ = qa.shape
    n_blocks = seq // MOBA_BLOCK
    assert n_blocks % ATTN_GROUP == 0 and ATTN_GROUP % ATTN_KV_BLOCKS == 0
    spec = pl.BlockSpec((1, 2, seq, AUG), lambda b, hp: (b, hp, 0, 0))
    return pl.pallas_call(
        functools.partial(_moba_attn_body, n_blocks=n_blocks),
        grid=(n_seq, n_heads // 2),
        in_specs=[spec, spec, spec],
        out_specs=pl.BlockSpec((seq, LANES), lambda b, hp: (b, hp)),
        out_shape=jax.ShapeDtypeStruct((n_seq * seq, n_heads * HEAD_DIM), BF16),
        compiler_params=_params(("parallel", "parallel")),
        name="moba_attn",
    )(qa, ka, va)


def _sample_attn_body(pt_ref, rowinfo_ref, q_ref, kn_ref, vn_ref, ck_ref, cv_ref, o_ref, qbd_ref, qbd32_ref, s_ref,
                      ksum_ref, acc_ref, knt_ref, vnt_ref, bias_ref, ring_ref, sem_ref, *, n_pages, pps, n_new,
                      n_heads):
    b = pl.program_id(0)
    t = pl.program_id(1)
    tk = n_pages // pps
    page = ck_ref.shape[3]
    d = q_ref.shape[1]
    past = n_pages * page
    n_blocks = past // MOBA_BLOCK
    ppb = MOBA_BLOCK // page
    n_rows = n_heads * n_new
    steps = 2 * tk
    total = pl.num_programs(0) * steps
    step = b * steps + t
    slot = step % PAGE_SLOTS

    def page_copy(src_ref, pool_page, to_slot, i):
        return pltpu.make_async_copy(src_ref.at[pool_page], ring_ref.at[to_slot, i], sem_ref.at[to_slot])

    def fetch(at):
        seq, tt, to_slot = at // steps, at % steps, at % PAGE_SLOTS

        @pl.when(tt < tk)
        def _():
            for i in range(pps):
                page_copy(ck_ref, pt_ref[seq, tt * pps + i], to_slot, i).start()

        @pl.when(tt >= tk)
        def _():
            for i in range(pps):
                page_copy(cv_ref, pt_ref[seq, (tt - tk) * pps + i], to_slot, i).start()

    @pl.when(step == 0)
    def _():
        for ahead in range(PAGE_SLOTS):
            fetch(ahead)

    @pl.when(jnp.logical_and(step > 0, step + PAGE_SLOTS - 1 < total))
    def _():
        fetch(step + PAGE_SLOTS - 1)

    for i in range(pps):
        page_copy(ck_ref, 0, slot, i).wait()

    head_of_col = lax.broadcasted_iota(jnp.int32, (n_heads, d), 1) // HEAD_DIM
    head_of_row = lax.broadcasted_iota(jnp.int32, (n_heads, d), 0)
    own_head = head_of_col == head_of_row

    def key_major(rows_val):
        padded = jnp.concatenate([rows_val, jnp.zeros((LANES - rows_val.shape[0], d), F32)], axis=0)
        return padded.T

    @pl.when(t == 0)
    def _():
        q = q_ref[...]
        qbd = jnp.concatenate([jnp.where(own_head, jnp.broadcast_to(q[n:n + 1, :], (n_heads, d)), 0.0)
                               for n in range(n_new)], axis=0)
        qbd32_ref[0:n_rows, :] = qbd
        qbd32_ref[n_rows:, :] = jnp.zeros((LANES - n_rows, d), F32)
        qbd_ref[...] = (qbd * HEAD_DIM ** -0.5).astype(BF16)
        knt_ref[...] = key_major(kn_ref[...])
        vnt_ref[...] = key_major(vn_ref[...])
        acc_ref[...] = jnp.zeros_like(acc_ref)
        ksum_ref[...] = jnp.zeros_like(ksum_ref)

    @pl.when(t < tk)
    def _():
        qbd = qbd_ref[...]
        lane_i = lax.broadcasted_iota(jnp.int32, (1, LANES), 1)
        for bi in range(pps // ppb):
            ksum = jnp.zeros((d, page), F32)
            for pi in range(ppb):
                i = bi * ppb + pi
                kp = ring_ref[slot, i].reshape(d, page)
                s_ref[t * pps + i] = jnp.dot(qbd, kp.astype(BF16), preferred_element_type=F32)
                ksum = ksum + kp
            onehot = jnp.where(lane_i == t * (pps // ppb) + bi, 1.0, 0.0).astype(F32)
            ksum_ref[...] += jnp.sum(ksum, axis=1, keepdims=True) * onehot

    @pl.when(t == tk - 1)
    def _():
        s_ref[n_pages] = jnp.dot(qbd_ref[...], knt_ref[...].astype(BF16), preferred_element_type=F32)
        gate = jnp.dot(qbd32_ref[...], ksum_ref[...], precision=lax.Precision.HIGHEST,
                       preferred_element_type=F32) * (1.0 / MOBA_BLOCK)
        gate_t = gate.T[:n_blocks]
        blk = lax.broadcasted_iota(jnp.int32, (n_blocks, LANES), 0)
        sel_t = _topk_select(gate_t, blk >= 0, blk, n_blocks)
        sel = jnp.concatenate([jnp.where(sel_t, 1.0, 0.0).astype(F32),
                               jnp.zeros((LANES - n_blocks, LANES), F32)], axis=0).T[:n_rows]
        q_idx = rowinfo_ref[0]
        slope = rowinfo_ref[1]
        lane_f = lax.broadcasted_iota(jnp.int32, (n_rows, LANES), 1).astype(F32)

        for b in range(n_blocks):
            bias_ref[b] = jnp.broadcast_to(jnp.where(sel[:, b:b + 1] > 0.5, 0.0, -jnp.inf), (n_rows, LANES))

        def logits(p):
            dist = (past + q_idx) - (jnp.asarray(p * page, F32) + lane_f)
            return s_ref[p] - slope * dist + bias_ref[p // ppb]

        dist_new = q_idx - lane_f
        s_new = jnp.where((dist_new >= 0) & (lane_f < n_new), s_ref[n_pages] - slope * dist_new, -jnp.inf)
        m = jnp.max(lax.fori_loop(0, n_pages, lambda p, mv: jnp.maximum(mv, logits(p)), s_new, unroll=8),
                    axis=1, keepdims=True)

        def exp_step(p, den):
            e = jnp.exp(logits(p) - m)
            s_ref[p] = e
            return den + e

        e_new = jnp.exp(s_new - m)
        inv = 1.0 / jnp.sum(lax.fori_loop(0, n_pages, exp_step, e_new, unroll=8), axis=1, keepdims=True)
        s_ref[n_pages] = e_new * inv

        def norm_step(p, carry):
            s_ref[p] = s_ref[p] * inv
            return carry

        lax.fori_loop(0, n_pages, norm_step, 0, unroll=8)

    def weighted(p_idx, v_t):
        return lax.dot_general(s_ref[p_idx].astype(BF16), v_t.astype(BF16), _NT, preferred_element_type=F32)

    @pl.when(t >= tk)
    def _():
        acc = acc_ref[...]
        for i in range(pps):
            acc = acc + weighted((t - tk) * pps + i, ring_ref[slot, i].reshape(d, page))
        acc_ref[...] = acc

    @pl.when(t == 2 * tk - 1)
    def _():
        acc = acc_ref[...] + weighted(n_pages, vnt_ref[...])
        outs = [jnp.sum(jnp.where(own_head, acc[n * n_heads:(n + 1) * n_heads, :], 0.0), axis=0, keepdims=True)
                for n in range(n_new)]
        outs.append(jnp.zeros((SAMPLE_PAD - n_new, d), F32))
        o_ref[...] = jnp.concatenate(outs, axis=0).astype(BF16)


def sample_attn(page_table, rowinfo, q, k_new, v_new, cache_kt, cache_vt, n_new, pps=8):
    n_dec, n_pages = page_table.shape
    _, n_heads, _, page = cache_kt.shape
    d = n_heads * HEAD_DIM
    ppb = MOBA_BLOCK // page
    assert n_pages % pps == 0 and pps % ppb == 0 and page == LANES
    assert n_new <= SAMPLE_PAD and n_heads % SUBLANES == 0
    tk = n_pages // pps
    n_blocks = n_pages // ppb
    n_rows = n_heads * n_new
    assert n_blocks <= LANES and n_rows <= LANES and n_dec * 2 * tk >= PAGE_SLOTS

    row_spec = pl.BlockSpec((SAMPLE_PAD, d), lambda b, t, pt: (b, 0))
    pool_spec = pl.BlockSpec(memory_space=pl.ANY)
    grid_spec = pltpu.PrefetchScalarGridSpec(
        num_scalar_prefetch=1,
        grid=(n_dec, 2 * tk),
        in_specs=[pl.BlockSpec((2, n_rows, LANES), lambda b, t, pt: (0, 0, 0)), row_spec, row_spec, row_spec,
                  pool_spec, pool_spec],
        out_specs=row_spec,
        scratch_shapes=[pltpu.VMEM((n_rows, d), BF16),
                        pltpu.VMEM((LANES, d), F32),
                        pltpu.VMEM((n_pages + 1, n_rows, LANES), F32),
                        pltpu.VMEM((d, LANES), F32),
                        pltpu.VMEM((n_rows, d), F32),
                        pltpu.VMEM((d, LANES), F32),
                        pltpu.VMEM((d, LANES), F32),
                        pltpu.VMEM((n_blocks, n_rows, LANES), F32),
                        pltpu.VMEM((PAGE_SLOTS, pps, n_heads, HEAD_DIM, page), F32),
                        pltpu.SemaphoreType.DMA((PAGE_SLOTS,))],
    )
    return pl.pallas_call(
        functools.partial(_sample_attn_body, n_pages=n_pages, pps=pps, n_new=n_new, n_heads=n_heads),
        grid_spec=grid_spec,
        out_shape=jax.ShapeDtypeStruct((n_dec * SAMPLE_PAD, d), BF16),
        compiler_params=_params(("arbitrary", "arbitrary")),
        name="sample_attn",
    )(page_table, rowinfo, q, k_new, v_new, cache_kt, cache_vt)


def _ssd_body(z_ref, xs_ref, bc_ref, dt_ref, cinit_ref, sinit_ref, cw_ref, cb_ref, dtb_ref, alog_ref,
              dsk_ref, gn_ref, y_ref, cout_ref, sout_ref, ext_ref, xc_ref, h_ref, **static):
    for phase in ("start", "main", "end"):
        for p in range(z_ref.shape[0]):
            _ssd_chunk(z_ref.at[p], xs_ref.at[p], bc_ref.at[p], dt_ref.at[p], cinit_ref.at[p], sinit_ref.at[p],
                       cw_ref, cb_ref, dtb_ref, alog_ref, dsk_ref, gn_ref, y_ref.at[p], cout_ref.at[p],
                       sout_ref.at[p], ext_ref.at[p], xc_ref.at[p], h_ref.at[p], phase=phase, **static)


def _ssd_chunk(z_ref, xs_ref, bc_ref, dt_ref, cinit_ref, sinit_ref, cw_ref, cb_ref, dtb_ref, alog_ref,
               dsk_ref, gn_ref, y_ref, cout_ref, sout_ref, ext_ref, xc_ref, h_ref, *, rows, valid, n_chunks,
               phase):
    cl = xc_ref.shape[0]
    c = pl.program_id(1)
    d_inner = xs_ref.shape[1]
    n_heads = d_inner // SSM_HEAD_DIM
    hpg = n_heads // N_GROUPS
    gw = hpg * SSM_HEAD_DIM
    tail = SUBLANES

    if phase == "start":
        @pl.when(c == 0)
        def _():
            ext_ref[0:tail, :] = cinit_ref[...]
            h_ref[...] = sinit_ref[...].reshape(h_ref.shape)
            if rows < cl:
                ext_ref[tail + rows:, :] = jnp.zeros((cl - rows, ext_ref.shape[1]), F32)

        @pl.when(c > 0)
        def _():
            ext_ref[0:tail, :] = ext_ref[cl:cl + tail, :]

        return

    if phase == "end":
        @pl.when(c == n_chunks - 1)
        def _():
            sout_ref[...] = h_ref[...].reshape(sout_ref.shape)
            first = tail + ((valid - (CONV_W - 1)) // SUBLANES) * SUBLANES
            cout_ref[...] = ext_ref[first:first + SUBLANES, :]

        return

    ext_ref[tail:tail + rows, 0:d_inner] = xs_ref[...]
    ext_ref[tail:tail + rows, d_inner:] = bc_ref[...]

    cw = cw_ref[...]
    conv = cb_ref[...] + ext_ref[tail:tail + cl, :] * cw[CONV_W - 1:CONV_W, :]
    for back in range(1, CONV_W):
        conv = conv + ext_ref[tail - back:tail - back + cl, :] * cw[CONV_W - 1 - back:CONV_W - back, :]
    xc_ref[...] = _silu(conv)

    def pad(v):
        if rows == cl:
            return v
        return jnp.concatenate([v, jnp.zeros((cl - rows, v.shape[1]), v.dtype)], axis=0)

    trow = lax.broadcasted_iota(jnp.int32, (cl, LANES), 0)
    dt_raw = pad(dt_ref[...]) + dtb_ref[...]
    dt = jnp.maximum(dt_raw, 0.0) + jnp.log1p(jnp.exp(-jnp.abs(dt_raw)))
    dt = jnp.where(trow < valid, dt, 0.0)
    a = -jnp.exp(alog_ref[...])
    tri_r = lax.broadcasted_iota(jnp.int32, (cl, cl), 0)
    tri_c = lax.broadcasted_iota(jnp.int32, (cl, cl), 1)
    causal = tri_c <= tri_r
    a_cs = jnp.dot(jnp.where(causal, 1.0, 0.0).astype(F32), dt * a, precision=lax.Precision.HIGHEST,
                   preferred_element_type=F32)
    a_cs_t = a_cs.T
    dt_t = dt.T
    z = pad(z_ref[...])
    from_start = jnp.exp(a_cs)
    to_end_all = jnp.exp(a_cs_t[:, cl - 1:cl] - a_cs_t) * dt_t

    for g in range(N_GROUPS):
        bg = xc_ref[:, d_inner + g * D_STATE:d_inner + (g + 1) * D_STATE].astype(BF16)
        cg = xc_ref[:, d_inner + (N_GROUPS + g) * D_STATE:d_inner + (N_GROUPS + g + 1) * D_STATE].astype(BF16)
        cb = lax.dot_general(cg, bg, _NT, preferred_element_type=F32)
        xg = xc_ref[:, g * gw:(g + 1) * gw]
        xg_t = xg.T
        h_old = h_ref[g * gw:(g + 1) * gw, :]
        y_off = lax.dot_general(cg, h_old.astype(BF16), _NT, preferred_element_type=F32)
        ys, xw_rows, h_scaled = [], [], []
        for e in range(hpg):
            h = g * hpg + e
            acs_col = a_cs[:, h:h + 1]
            acs_row = a_cs_t[h:h + 1, :]
            dt_row = dt_t[h:h + 1, :]
            decay = jnp.exp(jnp.where(causal, acs_col - acs_row, -jnp.inf))
            w = (cb * decay * dt_row).astype(BF16)
            xh = xg[:, e * SSM_HEAD_DIM:(e + 1) * SSM_HEAD_DIM]
            y_diag = jnp.dot(w, xh.astype(BF16), preferred_element_type=F32)
            ys.append(y_diag + y_off[:, e * SSM_HEAD_DIM:(e + 1) * SSM_HEAD_DIM] * from_start[:, h:h + 1])
            xw_rows.append(xg_t[e * SSM_HEAD_DIM:(e + 1) * SSM_HEAD_DIM, :] * to_end_all[h:h + 1, :])
            h_scaled.append(h_old[e * SSM_HEAD_DIM:(e + 1) * SSM_HEAD_DIM, :] * from_start[cl - 1:cl, h:h + 1])
        states = jnp.dot(jnp.concatenate(xw_rows, axis=0).astype(BF16), bg, preferred_element_type=F32)
        h_ref[g * gw:(g + 1) * gw, :] = jnp.concatenate(h_scaled, axis=0) + states
        y = jnp.concatenate(ys, axis=1) + dsk_ref[:, g * gw:(g + 1) * gw] * xg
        yz = y * _silu(z[:, g * gw:(g + 1) * gw])
        ms = jnp.mean(yz * yz, axis=-1, keepdims=True)
        yn = yz * lax.rsqrt(ms + EPS) * gn_ref[:, g * gw:(g + 1) * gw]
        y_ref[:, g * gw:(g + 1) * gw] = yn[:rows].astype(BF16)


def ssd_mixer(proj, conv_init, ssm_init, cw, cb, dtb, alog, dsk, gnorm, n_seq, rows, valid, d_inner):
    total_rows = proj.shape[0] // n_seq
    n_chunks = total_rows // rows
    assert rows == SSD_CHUNK or n_chunks == 1
    chunk = rows if n_chunks == 1 else SSD_CHUNK
    assert CONV_W - 1 <= valid <= rows
    n_heads = d_inner // SSM_HEAD_DIM
    conv_dim = d_inner + 2 * N_GROUPS * D_STATE
    assert conv_dim == 2 * d_inner

    par = SSD_SEQS_PER_STEP
    assert n_seq % par == 0
    proj3 = proj.reshape(n_seq, total_rows, proj.shape[1])

    def col(width, idx):
        return pl.BlockSpec((par, rows, width), lambda b, c: (b, c, idx))

    def const(shape):
        return pl.BlockSpec(shape, lambda b, c: (0,) * len(shape))

    def per_seq(shape):
        return pl.BlockSpec((par,) + shape, lambda b, c: (b,) + (0,) * len(shape))

    y, conv_out, ssm_out = pl.pallas_call(
        functools.partial(_ssd_body, rows=rows, valid=valid, n_chunks=n_chunks),
        grid=(n_seq // par, n_chunks),
        in_specs=[col(d_inner, 0), col(d_inner, 1), col(d_inner, 2), col(LANES, 3 * d_inner // LANES),
                  per_seq((SUBLANES, conv_dim)), per_seq((n_heads, SSM_HEAD_DIM, D_STATE)),
                  const((CONV_W, conv_dim)), const((1, conv_dim)), const((1, LANES)), const((1, LANES)),
                  const((1, d_inner)), const((1, d_inner))],
        out_specs=[col(d_inner, 0), per_seq((SUBLANES, conv_dim)), per_seq((n_heads, SSM_HEAD_DIM, D_STATE))],
        out_shape=[jax.ShapeDtypeStruct((n_seq, total_rows, d_inner), BF16),
                   jax.ShapeDtypeStruct((n_seq, SUBLANES, conv_dim), F32),
                   jax.ShapeDtypeStruct((n_seq, n_heads, SSM_HEAD_DIM, D_STATE), F32)],
        scratch_shapes=[pltpu.VMEM((par, SUBLANES + chunk, conv_dim), F32),
                        pltpu.VMEM((par, chunk, conv_dim), F32),
                        pltpu.VMEM((par, n_heads * SSM_HEAD_DIM, D_STATE), F32)],
        compiler_params=_params(("parallel", "arbitrary")),
        name="ssd_mixer",
    )(proj3, proj3, proj3, proj3, conv_init, ssm_init, cw, cb, dtb, alog, dsk, gnorm)
    return y.reshape(n_seq * total_rows, d_inner), conv_out, ssm_out


def _pad_cols(w, n):
    return jnp.pad(w, ((0, 0), (0, n - w.shape[1])))


def _pad_lanes(v):
    return jnp.pad(v, (0, LANES - v.shape[0])).reshape(1, LANES)


def kernel(x_prompt, x_sample, cache_k, cache_v, state_conv, state_ssm, page_table, norm_mix, norm_ffn, w_qkv, w_o, w_in_ssm, conv_w, conv_b, dt_bias, a_log, d_skip, norm_ssm, w_out_ssm, w_gate_up, w_down, norm_final):
    n_seq, seq, d_model = x_prompt.shape
    n_dec, n_new, _ = x_sample.shape
    n_heads = d_model // HEAD_DIM
    d_inner = norm_ssm.shape[1]
    ssm_heads = d_inner // SSM_HEAD_DIM
    conv_dim = conv_w.shape[2]
    tm_p = 512
    tm_s = n_dec * SAMPLE_PAD

    hp = x_prompt.reshape(n_seq * seq, d_model)
    hs = jnp.pad(x_sample, ((0, 0), (0, SAMPLE_PAD - n_new), (0, 0))).reshape(tm_s, d_model)

    wqkv = w_qkv[0]
    wq_hi = _bf16_round(wqkv[:, :d_model])
    w_hi = jnp.concatenate([wq_hi, wqkv[:, d_model:]], axis=1).astype(BF16)
    wq_lo = (wqkv[:, :d_model] - wq_hi).astype(BF16)
    wkvt = wqkv[:, d_model:].T.astype(BF16)
    slopes = jnp.exp2(-8.0 * (jnp.arange(n_heads, dtype=F32) + 1.0) / n_heads)
    kx = _key_extras(slopes, seq, seq // MOBA_BLOCK)
    q_p, ka, va, kt_p, vt_p, xm_p = qkv_prompt(hp, norm_mix[0], w_hi, wq_lo, wkvt, kx, n_seq, seq, tm_p)
    q_s, k_s, v_s, _ = qkv_proj(hs, norm_mix[0], w_hi, wq_lo, tm_s)
    kmean = mm_f32(xm_p.reshape(n_seq * seq // MOBA_BLOCK, d_model), wqkv[:, d_model:2 * d_model])
    qa = moba_prep(q_p, kmean, n_seq, seq, n_heads)
    attn_p = moba_attn(qa, ka, va)
    rows = jnp.arange(n_heads * n_new)
    rowinfo = jnp.stack([(rows // n_heads).astype(F32), slopes[rows % n_heads]])
    rowinfo = jnp.broadcast_to(rowinfo[:, :, None], (2, n_heads * n_new, LANES))
    cache_kt = jnp.transpose(cache_k[0], (0, 2, 3, 1))
    cache_vt = jnp.transpose(cache_v[0], (0, 2, 3, 1))
    attn_s = sample_attn(page_table, rowinfo, q_s, k_s, v_s, cache_kt, cache_vt, n_new)
    hp = mm_res(attn_p, w_o, 0, hp, tm_p)
    hs = mm_res(attn_s, w_o, 0, hs, tm_s)

    k_prompt = jnp.transpose(kt_p, (0, 3, 1, 2))[None]
    v_prompt = jnp.transpose(vt_p, (0, 3, 1, 2))[None]
    k_sample = k_s.reshape(n_dec, SAMPLE_PAD, n_heads, HEAD_DIM)[None, :, :n_new]
    v_sample = v_s.reshape(n_dec, SAMPLE_PAD, n_heads, HEAD_DIM)[None, :, :n_new]

    def ffn(h, layer, tm, final_g=None):
        act = swiglu_gu(h, norm_ffn[layer], w_gate_up, layer, tm, 256)
        return mm_res(act, w_down, layer, h, tm, final_g)

    hp = ffn(hp, 0, tm_p)
    hs = ffn(hs, 0, tm_s)

    in_cols = d_inner + conv_dim + ssm_heads
    assert w_in_ssm.shape[2] == in_cols
    proj_w = ((in_cols + 1279) // 1280) * 1280
    w_in = _pad_cols(w_in_ssm[0], proj_w).astype(BF16)
    proj_p = norm_mm(hp, norm_mix[1], w_in, tm_p, 1280)
    proj_s = norm_mm(hs, norm_mix[1], w_in, tm_s, 1280)
    cb = conv_b[0].reshape(1, conv_dim)
    dtb = _pad_lanes(dt_bias[0])
    alog = _pad_lanes(a_log[0])
    dsk = jnp.repeat(d_skip[0], SSM_HEAD_DIM).reshape(1, d_inner)
    gn = norm_ssm[0].reshape(1, d_inner)
    zero_conv = jnp.zeros((n_seq, SUBLANES, conv_dim), F32)
    zero_ssm = jnp.zeros((n_seq, ssm_heads, SSM_HEAD_DIM, D_STATE), F32)
    y_p, conv_p, ssm_p = ssd_mixer(proj_p, zero_conv, zero_ssm, conv_w[0], cb, dtb, alog, dsk, gn,
                                   n_seq, SSD_CHUNK, SSD_CHUNK, d_inner)
    conv_init_s = jnp.pad(state_conv[0], ((0, 0), (SUBLANES - (CONV_W - 1), 0), (0, 0)))
    y_s, conv_s, ssm_s = ssd_mixer(proj_s, conv_init_s, state_ssm[0], conv_w[0], cb, dtb, alog, dsk, gn,
                                   n_dec, SAMPLE_PAD, n_new, d_inner)
    hp = mm_res(y_p, w_out_ssm, 0, hp, tm_p)
    hs = mm_res(y_s, w_out_ssm, 0, hs, tm_s)
    y_prompt = ffn(hp, 1, tm_p, norm_final).reshape(n_seq, seq, d_model)
    y_sample = ffn(hs, 1, tm_s, norm_final).reshape(n_dec, SAMPLE_PAD, d_model)[:, :n_new]

    lo_p = SUBLANES - (CONV_W - 1)
    lo_s = (n_new - (CONV_W - 1)) % SUBLANES
    return (y_prompt, y_sample, k_prompt, v_prompt, k_sample, v_sample,
            conv_p[None, :, lo_p:lo_p + CONV_W - 1], ssm_p[None],
            conv_s[None, :, lo_s:lo_s + CONV_W - 1], ssm_s[None])
```

```python
import functools
import math

import jax
import jax.numpy as jnp
from jax import lax
from jax.experimental import pallas as pl
from jax.experimental.pallas import tpu as pltpu

F32 = jnp.float32
BF16 = jnp.bfloat16

EPS = 1e-5
HEAD_DIM = 64
MOBA_BLOCK = 256
MOBA_TOPK = 3
SSM_HEAD_DIM = 64
N_GROUPS = 8
D_STATE = 128
CONV_W = 4
SSD_CHUNK = 128
SSD_SEQS_PER_STEP = 2
PAGE_SLOTS = 6
SAMPLE_PAD = 16
AUG = 128
NEG_BIAS = -1e30
LOG2E = math.log2(math.e)
ATTN_GROUP = 4
ATTN_KV_BLOCKS = 4
PREP_BLOCKS = 8
LANES = 128
SUBLANES = 8
VMEM_LIMIT = 48 * 1024 * 1024
VMEM_LIMIT_BIG = 56 * 1024 * 1024

_NT = (((1,), (1,)), ((), ()))


def _silu(x):
    return x / (1.0 + jnp.exp(-x))


def _params(sem, vmem=VMEM_LIMIT):
    return pltpu.CompilerParams(dimension_semantics=sem, vmem_limit_bytes=vmem)


def _rms_bf16(x_ref, g_ref):
    x = x_ref[...]
    ms = jnp.mean(x * x, axis=-1, keepdims=True)
    return (x * lax.rsqrt(ms + EPS) * g_ref[...]).astype(BF16)


def _norm_mm_body(x_ref, g_ref, w_ref, o_ref, *, tn):
    xn = _rms_bf16(x_ref, g_ref)
    for c in range(w_ref.shape[1] // tn):
        o_ref[:, c * tn:(c + 1) * tn] = jnp.dot(xn, w_ref[:, c * tn:(c + 1) * tn], preferred_element_type=F32)


def norm_mm(x, g, w, tm, tn):
    m, d = x.shape
    n = w.shape[1]
    assert n % tn == 0
    return pl.pallas_call(
        functools.partial(_norm_mm_body, tn=tn),
        grid=(m // tm,),
        in_specs=[pl.BlockSpec((tm, d), lambda i: (i, 0)),
                  pl.BlockSpec((1, d), lambda i: (0, 0)),
                  pl.BlockSpec((d, n), lambda i: (0, 0), pipeline_mode=pl.Buffered(1))],
        out_specs=pl.BlockSpec((tm, n), lambda i: (i, 0)),
        out_shape=jax.ShapeDtypeStruct((m, n), F32),
        compiler_params=_params(("parallel",), VMEM_LIMIT_BIG),
        name="norm_mm",
    )(x, g.reshape(1, d), w)


def _swiglu_gu_body(x_ref, g_ref, w_ref, o_ref, *, tn):
    xn = _rms_bf16(x_ref, g_ref)
    dff = o_ref.shape[1]
    for c in range(dff // tn):
        gate = jnp.dot(xn, w_ref[0, :, c * tn:(c + 1) * tn].astype(BF16), preferred_element_type=F32)
        up = jnp.dot(xn, w_ref[0, :, dff + c * tn:dff + (c + 1) * tn].astype(BF16), preferred_element_type=F32)
        o_ref[:, c * tn:(c + 1) * tn] = (_silu(gate) * up).astype(BF16)


def swiglu_gu(x, g, w_gu, layer, tm, tn):
    m, d = x.shape
    dff = w_gu.shape[2] // 2
    assert dff % tn == 0
    return pl.pallas_call(
        functools.partial(_swiglu_gu_body, tn=tn),
        grid=(m // tm,),
        in_specs=[pl.BlockSpec((tm, d), lambda i: (i, 0)),
                  pl.BlockSpec((1, d), lambda i: (0, 0)),
                  pl.BlockSpec((1, d, 2 * dff), lambda i: (layer, 0, 0), pipeline_mode=pl.Buffered(1))],
        out_specs=pl.BlockSpec((tm, dff), lambda i: (i, 0)),
        out_shape=jax.ShapeDtypeStruct((m, dff), BF16),
        compiler_params=_params(("parallel",)),
        name="swiglu_gu",
    )(x, g.reshape(1, d), w_gu)


def _mm_res_body(a_ref, w_ref, r_ref, o_ref):
    o_ref[...] = r_ref[...] + jnp.dot(a_ref[...], w_ref[0].astype(BF16), preferred_element_type=F32)


def _mm_res_norm_body(a_ref, w_ref, r_ref, g_ref, o_ref):
    y = r_ref[...] + jnp.dot(a_ref[...], w_ref[0].astype(BF16), preferred_element_type=F32)
    ms = jnp.mean(y * y, axis=-1, keepdims=True)
    o_ref[...] = y * lax.rsqrt(ms + EPS) * g_ref[...]


def mm_res(a, w, layer, res, tm, final_g=None):
    m, k = a.shape
    n = w.shape[2]
    in_specs = [pl.BlockSpec((tm, k), lambda i: (i, 0)),
                pl.BlockSpec((1, k, n), lambda i: (layer, 0, 0), pipeline_mode=pl.Buffered(1)),
                pl.BlockSpec((tm, n), lambda i: (i, 0))]
    args = [a, w, res]
    body = _mm_res_body
    if final_g is not None:
        in_specs.append(pl.BlockSpec((1, n), lambda i: (0, 0)))
        args.append(final_g.reshape(1, n))
        body = _mm_res_norm_body
    return pl.pallas_call(
        body,
        grid=(m // tm,),
        in_specs=in_specs,
        out_specs=pl.BlockSpec((tm, n), lambda i: (i, 0)),
        out_shape=jax.ShapeDtypeStruct((m, n), F32),
        compiler_params=_params(("parallel",)),
        name="mm_res_norm" if final_g is not None else "mm_res",
    )(*args)


def _qkv_body(x_ref, g_ref, w_ref, wq_lo_ref, q_ref, k_ref, v_ref, xm_ref):
    d = x_ref.shape[1]
    x = x_ref[...]
    ms = jnp.mean(x * x, axis=-1, keepdims=True)
    xn = x * lax.rsqrt(ms + EPS) * g_ref[...]
    hi = xn.astype(BF16)
    lo = (xn - hi.astype(F32)).astype(BF16)
    wq = w_ref[:, 0:d]
    q_ref[...] = (jnp.dot(hi, wq, preferred_element_type=F32) + jnp.dot(lo, wq, preferred_element_type=F32)
                  + jnp.dot(hi, wq_lo_ref[...], preferred_element_type=F32))
    k_ref[...] = jnp.dot(hi, w_ref[:, d:2 * d], preferred_element_type=F32)
    v_ref[...] = jnp.dot(hi, w_ref[:, 2 * d:3 * d], preferred_element_type=F32)
    for blk in range(xm_ref.shape[0]):
        xm_ref[blk] = jnp.mean(xn[blk * MOBA_BLOCK:(blk + 1) * MOBA_BLOCK], axis=0, keepdims=True)


def qkv_proj(x, g, w, wq_lo, tm):
    m, d = x.shape
    assert tm % MOBA_BLOCK == 0
    row = pl.BlockSpec((tm, d), lambda i: (i, 0))
    out = jax.ShapeDtypeStruct((m, d), F32)
    return pl.pallas_call(
        _qkv_body,
        grid=(m // tm,),
        in_specs=[row, pl.BlockSpec((1, d), lambda i: (0, 0)),
                  pl.BlockSpec((d, 3 * d), lambda i: (0, 0), pipeline_mode=pl.Buffered(1)),
                  pl.BlockSpec((d, d), lambda i: (0, 0), pipeline_mode=pl.Buffered(1))],
        out_specs=[row, row, row, pl.BlockSpec((tm // MOBA_BLOCK, 1, d), lambda i: (i, 0, 0))],
        out_shape=[out, out, out, jax.ShapeDtypeStruct((m // MOBA_BLOCK, 1, d), F32)],
        compiler_params=_params(("parallel",)),
        name="qkv_proj",
    )(x, g.reshape(1, d), w, wq_lo)


def _qkv_prompt_body(x_ref, g_ref, w_ref, wq_lo_ref, wkvt_ref, kx_ref, q_ref, ka_ref, va_ref, kt_ref, vt_ref,
                     xm_ref):
    d = x_ref.shape[1]
    n_heads = ka_ref.shape[1]
    tm = x_ref.shape[0]
    half = AUG // 2
    x = x_ref[...]
    ms = jnp.mean(x * x, axis=-1, keepdims=True)
    xn = x * lax.rsqrt(ms + EPS) * g_ref[...]
    hi = xn.astype(BF16)
    lo = (xn - hi.astype(F32)).astype(BF16)
    wq = w_ref[:, 0:d]
    q_ref[...] = (jnp.dot(hi, wq, preferred_element_type=F32) + jnp.dot(lo, wq, preferred_element_type=F32)
                  + jnp.dot(hi, wq_lo_ref[...], preferred_element_type=F32))
    k = jnp.dot(hi, w_ref[:, d:2 * d], preferred_element_type=F32)
    v = jnp.dot(hi, w_ref[:, 2 * d:3 * d], preferred_element_type=F32)
    lane = lax.broadcasted_iota(jnp.int32, (tm, AUG), 1)
    for pair in range(n_heads // 2):
        kb = k[:, pair * AUG:(pair + 1) * AUG]
        vb = v[:, pair * AUG:(pair + 1) * AUG]
        for hh in range(2):
            h = 2 * pair + hh
            data = (lane < half) if hh == 0 else (lane >= half)
            ones_lane = half if hh == 0 else 0
            ka_ref[0, h] = jnp.where(data, kb, kx_ref[h].astype(F32)).astype(BF16)
            va_ref[0, h] = jnp.where(data, vb, jnp.where(lane == ones_lane, 1.0, 0.0)).astype(BF16)
    kt = lax.dot_general(wkvt_ref[0:d, :], hi, _NT, preferred_element_type=F32)
    vt = lax.dot_general(wkvt_ref[d:2 * d, :], hi, _NT, preferred_element_type=F32)
    kt_ref[0] = kt.reshape(n_heads, HEAD_DIM, tm)
    vt_ref[0] = vt.reshape(n_heads, HEAD_DIM, tm)
    for blk in range(xm_ref.shape[0]):
        xm_ref[blk] = jnp.mean(xn[blk * MOBA_BLOCK:(blk + 1) * MOBA_BLOCK], axis=0, keepdims=True)


def qkv_prompt(x, g, w, wq_lo, wkvt, kx, n_seq, seq, tm):
    m, d = x.shape
    n_heads = d // HEAD_DIM
    assert tm % MOBA_BLOCK == 0 and seq % tm == 0
    steps = seq // tm
    row = pl.BlockSpec((tm, d), lambda i: (i, 0))
    aug_spec = pl.BlockSpec((1, n_heads, tm, AUG), lambda i: (i // steps, 0, i % steps, 0))
    t_spec = pl.BlockSpec((1, n_heads, HEAD_DIM, tm), lambda i: (i // steps, 0, 0, i % steps))
    aug_shape = jax.ShapeDtypeStruct((n_seq, n_heads, seq, AUG), BF16)
    t_shape = jax.ShapeDtypeStruct((n_seq, n_heads, HEAD_DIM, seq), F32)

    def whole(shape):
        return pl.BlockSpec(shape, lambda i: (0,) * len(shape), pipeline_mode=pl.Buffered(1))

    return pl.pallas_call(
        _qkv_prompt_body,
        grid=(m // tm,),
        in_specs=[row, pl.BlockSpec((1, d), lambda i: (0, 0)), whole((d, 3 * d)), whole((d, d)), whole((2 * d, d)),
                  pl.BlockSpec((n_heads, tm, AUG), lambda i: (0, i % steps, 0))],
        out_specs=[row, aug_spec, aug_spec, t_spec, t_spec,
                   pl.BlockSpec((tm // MOBA_BLOCK, 1, d), lambda i: (i, 0, 0))],
        out_shape=[jax.ShapeDtypeStruct((m, d), F32), aug_shape, aug_shape, t_shape, t_shape,
                   jax.ShapeDtypeStruct((m // MOBA_BLOCK, 1, d), F32)],
        compiler_params=_params(("parallel",), VMEM_LIMIT_BIG),
        name="qkv_prompt",
    )(x, g.reshape(1, d), w, wq_lo, wkvt, kx)


def _mm_f32_body(a_ref, w_ref, o_ref):
    o_ref[...] = jnp.dot(a_ref[...], w_ref[...], precision=lax.Precision.HIGHEST, preferred_element_type=F32)


def mm_f32(a, w):
    m, k = a.shape
    n = w.shape[1]
    return pl.pallas_call(
        _mm_f32_body,
        grid=(1,),
        in_specs=[pl.BlockSpec((m, k), lambda i: (0, 0)), pl.BlockSpec((k, n), lambda i: (0, 0))],
        out_specs=pl.BlockSpec((m, n), lambda i: (0, 0)),
        out_shape=jax.ShapeDtypeStruct((m, n), F32),
        compiler_params=_params(("arbitrary",)),
        name="mm_f32",
    )(a, w)


def _topk_select(gate, valid, blk, n_blocks):
    tiles = n_blocks // SUBLANES

    def over_blocks(x, op):
        acc = x[0:SUBLANES]
        for i in range(1, tiles):
            acc = op(acc, x[i * SUBLANES:(i + 1) * SUBLANES])
        shift = SUBLANES // 2
        while shift:
            acc = op(acc, pltpu.roll(acc, shift, axis=0))
            shift //= 2
        return jnp.concatenate([acc] * tiles, axis=0) if tiles > 1 else acc

    g = jnp.where(valid, gate, -jnp.inf)
    taken = jnp.zeros(g.shape, jnp.int32)
    for _ in range(MOBA_TOPK):
        best = over_blocks(g, jnp.maximum)
        first = over_blocks(jnp.where(g == best, blk, n_blocks), jnp.minimum)
        pick = blk == first
        taken = jnp.where(pick, 1, taken)
        g = jnp.where(pick, -jnp.inf, g)
    return jnp.logical_and(valid, taken > 0)


def _bf16_trunc(x):
    bits = lax.bitcast_convert_type(x, jnp.uint32) & jnp.uint32(0xFFFF0000)
    return lax.bitcast_convert_type(bits, F32)


def _bf16_round(x):
    bits = lax.bitcast_convert_type(x, jnp.uint32)
    bits = (bits + jnp.uint32(0x7FFF) + ((bits >> 16) & jnp.uint32(1))) & jnp.uint32(0xFFFF0000)
    return lax.bitcast_convert_type(bits, F32)


def _split3(x):
    hi = _bf16_trunc(x)
    mid = _bf16_trunc(x - hi)
    return hi, mid, x - hi - mid


def _moba_prep_body(q_ref, kmean_ref, qa_ref, *, n_blocks):
    half = AUG // 2
    kmean = kmean_ref[...]
    blk = lax.broadcasted_iota(jnp.int32, (n_blocks, MOBA_BLOCK), 0)
    lane = lax.broadcasted_iota(jnp.int32, (MOBA_BLOCK, AUG), 1)
    arow = lax.broadcasted_iota(jnp.int32, (SUBLANES, MOBA_BLOCK), 0)
    ones_rows = jnp.where(arow < 3, 1.0, 0.0).astype(F32)
    gap = jnp.zeros((half - n_blocks - SUBLANES, MOBA_BLOCK), F32)
    other = jnp.zeros((half, MOBA_BLOCK), F32)
    for j in range(q_ref.shape[0] // MOBA_BLOCK):
        sb = pl.program_id(1) * (q_ref.shape[0] // MOBA_BLOCK) + j
        rows = slice(j * MOBA_BLOCK, (j + 1) * MOBA_BLOCK)
        valid = blk < sb
        q = q_ref[rows, :] * (LOG2E * HEAD_DIM ** -0.5)
        for hh in range(2):
            first = hh == 0
            data = (lane < half) if first else (lane >= half)
            gate = lax.dot_general(kmean[:, hh * half:(hh + 1) * half], q_ref[rows, hh * half:(hh + 1) * half], _NT,
                                   precision=lax.Precision.HIGHEST, preferred_element_type=F32)
            sel = _topk_select(gate, valid, blk, n_blocks)
            bias_t = jnp.where(sel, 0.0, jnp.where(blk == sb, 0.0, NEG_BIAS)).astype(F32)
            extras_t = [bias_t, ones_rows, gap]
            aug = jnp.concatenate([other] + extras_t if first else extras_t + [other], axis=0).T
            qa_ref[0, hh, rows, :] = jnp.where(data, q, aug).astype(BF16)


def _key_extras(slopes, seq, n_blocks):
    shape = (slopes.shape[0], seq, AUG)
    pos = jnp.arange(seq, dtype=F32)
    hi, mid, lo = _split3(pos[None, :] * (slopes * LOG2E)[:, None])
    head = lax.broadcasted_iota(jnp.int32, shape, 0)
    key = lax.broadcasted_iota(jnp.int32, shape, 1)
    lane = lax.broadcasted_iota(jnp.int32, shape, 2)
    e = lane - jnp.where(head % 2 == 0, AUG // 2, 0)
    kx = jnp.where(e == key // MOBA_BLOCK, 1.0, 0.0)
    for i, part in enumerate((hi, mid, lo)):
        kx = jnp.where(e == n_blocks + i, part[:, :, None], kx)
    return kx.astype(BF16)


def moba_prep(q, kmean, n_seq, seq, n_heads):
    n_blocks = seq // MOBA_BLOCK
    assert n_blocks + SUBLANES <= AUG // 2 and n_blocks % SUBLANES == 0
    hpairs = n_heads // 2
    rows = PREP_BLOCKS * MOBA_BLOCK
    n_steps = seq // rows
    assert seq % rows == 0
    return pl.pallas_call(
        functools.partial(_moba_prep_body, n_blocks=n_blocks),
        grid=(hpairs, n_steps, n_seq),
        in_specs=[pl.BlockSpec((rows, LANES), lambda hp, sb, b: (b * n_steps + sb, hp)),
                  pl.BlockSpec((n_blocks, LANES), lambda hp, sb, b: (b, hp))],
        out_specs=pl.BlockSpec((1, 2, rows, AUG), lambda hp, sb, b: (b, hp, sb, 0)),
        out_shape=jax.ShapeDtypeStruct((n_seq, n_heads, seq, AUG), BF16),
        compiler_params=_params(("parallel", "parallel", "parallel")),
        name="moba_prep",
    )(q, kmean)


def _moba_attn_body(qa_ref, ka_ref, va_ref, o_ref, *, n_blocks):
    t = MOBA_BLOCK
    grp = ATTN_GROUP
    rows = grp * t
    kw = ATTN_KV_BLOCKS
    out_lane = lax.broadcasted_iota(jnp.int32, (rows, AUG), 1)

    def update(q, m, acc, hh, first_blk, n_blk, mask=None):
        width = n_blk * t
        start = pl.multiple_of(first_blk * t, t)
        s = lax.dot_general(q, ka_ref[0, hh, pl.ds(start, width), :], _NT, preferred_element_type=F32)
        if mask is not None:
            s = jnp.where(mask, s, -jnp.inf)
        m_new = jnp.maximum(m, jnp.max(s, axis=-1, keepdims=True))
        p = jnp.exp2(s - m_new)
        acc = jnp.exp2(m - m_new) * acc + jnp.dot(p.astype(BF16), va_ref[0, hh, pl.ds(start, width), :],
                                                  preferred_element_type=F32)
        return m_new, acc

    def group(g, carry):
        base = g * grp
        qs = [qa_ref[0, hh, pl.ds(pl.multiple_of(base * t, t), rows), :] for hh in range(2)]

        def past_tiles(kj, state):
            out = []
            for hh in range(2):
                out.extend(update(qs[hh], state[2 * hh], state[2 * hh + 1], hh, kj * kw, kw))
            return tuple(out)

        init = [jnp.full((rows, 1), -jnp.inf, F32), jnp.zeros((rows, AUG), F32)] * 2
        state = list(lax.fori_loop(0, base // kw, past_tiles, tuple(init)))

        rr = lax.broadcasted_iota(jnp.int32, (t, t), 0)
        cc = lax.broadcasted_iota(jnp.int32, (t, t), 1)
        causal = cc <= rr
        outs = []
        half = grp // 2
        for hh in range(2):
            den_lane = HEAD_DIM if hh == 0 else 0
            m_all, acc_all = state[2 * hh], state[2 * hh + 1]
            m_late, acc_late = update(qs[hh][half * t:], m_all[half * t:], acc_all[half * t:], hh, base, half)
            parts = []
            for a in range(grp):
                sl = slice(a * t, (a + 1) * t)
                if a < half:
                    m, acc, first = m_all[sl], acc_all[sl], 0
                else:
                    late = slice((a - half) * t, (a - half + 1) * t)
                    m, acc, first = m_late[late], acc_late[late], half
                for b in range(first, a + 1):
                    m, acc = update(qs[hh][sl], m, acc, hh, base + b, 1, causal if b == a else None)
                parts.append(acc / acc[:, den_lane:den_lane + 1])
            outs.append(jnp.concatenate(parts, axis=0))
        o_ref[pl.ds(pl.multiple_of(base * t, t), rows), :] = jnp.where(
            out_lane < HEAD_DIM, outs[0], outs[1]).astype(BF16)
        return carry

    lax.fori_loop(0, n_blocks // grp, group, 0)


def moba_attn(qa, ka, va):
    n_seq, n_heads, seq, _ = qa.shape
    n_blocks = seq // MOBA_BLOCK
    assert n_blocks % ATTN_GROUP == 0 and ATTN_GROUP % ATTN_KV_BLOCKS == 0
    spec = pl.BlockSpec((1, 2, seq, AUG), lambda b, hp: (b, hp, 0, 0))
    return pl.pallas_call(
        functools.partial(_moba_attn_body, n_blocks=n_blocks),
        grid=(n_seq, n_heads // 2),
        in_specs=[spec, spec, spec],
        out_specs=pl.BlockSpec((seq, LANES), lambda b, hp: (b, hp)),
        out_shape=jax.ShapeDtypeStruct((n_seq * seq, n_heads * HEAD_DIM), BF16),
        compiler_params=_params(("parallel", "parallel")),
        name="moba_attn",
    )(qa, ka, va)


def _sample_attn_body(pt_ref, rowinfo_ref, q_ref, kn_ref, vn_ref, ck_ref, cv_ref, o_ref, qbd_ref, qbd32_ref, s_ref,
                      ksum_ref, acc_ref, knt_ref, vnt_ref, bias_ref, ring_ref, sem_ref, *, n_pages, pps, n_new,
                      n_heads):
    b = pl.program_id(0)
    t = pl.program_id(1)
    tk = n_pages // pps
    page = ck_ref.shape[3]
    d = q_ref.shape[1]
    past = n_pages * page
    n_blocks = past // MOBA_BLOCK
    ppb = MOBA_BLOCK // page
    n_rows = n_heads * n_new
    steps = 2 * tk
    total = pl.num_programs(0) * steps
    step = b * steps + t
    slot = step % PAGE_SLOTS

    def page_copy(src_ref, pool_page, to_slot, i):
        return pltpu.make_async_copy(src_ref.at[pool_page], ring_ref.at[to_slot, i], sem_ref.at[to_slot])

    def fetch(at):
        seq, tt, to_slot = at // steps, at % steps, at % PAGE_SLOTS

        @pl.when(tt < tk)
        def _():
            for i in range(pps):
                page_copy(ck_ref, pt_ref[seq, tt * pps + i], to_slot, i).start()

        @pl.when(tt >= tk)
        def _():
            for i in range(pps):
                page_copy(cv_ref, pt_ref[seq, (tt - tk) * pps + i], to_slot, i).start()

    @pl.when(step == 0)
    def _():
        for ahead in range(PAGE_SLOTS):
            fetch(ahead)

    @pl.when(jnp.logical_and(step > 0, step + PAGE_SLOTS - 1 < total))
    def _():
        fetch(step + PAGE_SLOTS - 1)

    for i in range(pps):
        page_copy(ck_ref, 0, slot, i).wait()

    head_of_col = lax.broadcasted_iota(jnp.int32, (n_heads, d), 1) // HEAD_DIM
    head_of_row = lax.broadcasted_iota(jnp.int32, (n_heads, d), 0)
    own_head = head_of_col == head_of_row

    def key_major(rows_val):
        padded = jnp.concatenate([rows_val, jnp.zeros((LANES - rows_val.shape[0], d), F32)], axis=0)
        return padded.T

    @pl.when(t == 0)
    def _():
        q = q_ref[...]
        qbd = jnp.concatenate([jnp.where(own_head, jnp.broadcast_to(q[n:n + 1, :], (n_heads, d)), 0.0)
                               for n in range(n_new)], axis=0)
        qbd32_ref[0:n_rows, :] = qbd
        qbd32_ref[n_rows:, :] = jnp.zeros((LANES - n_rows, d), F32)
        qbd_ref[...] = (qbd * HEAD_DIM ** -0.5).astype(BF16)
        knt_ref[...] = key_major(kn_ref[...])
        vnt_ref[...] = key_major(vn_ref[...])
        acc_ref[...] = jnp.zeros_like(acc_ref)
        ksum_ref[...] = jnp.zeros_like(ksum_ref)

    @pl.when(t < tk)
    def _():
        qbd = qbd_ref[...]
        lane_i = lax.broadcasted_iota(jnp.int32, (1, LANES), 1)
        for bi in range(pps // ppb):
            ksum = jnp.zeros((d, page), F32)
            for pi in range(ppb):
                i = bi * ppb + pi
                kp = ring_ref[slot, i].reshape(d, page)
                s_ref[t * pps + i] = jnp.dot(qbd, kp.astype(BF16), preferred_element_type=F32)
                ksum = ksum + kp
            onehot = jnp.where(lane_i == t * (pps // ppb) + bi, 1.0, 0.0).astype(F32)
            ksum_ref[...] += jnp.sum(ksum, axis=1, keepdims=True) * onehot

    @pl.when(t == tk - 1)
    def _():
        s_ref[n_pages] = jnp.dot(qbd_ref[...], knt_ref[...].astype(BF16), preferred_element_type=F32)
        gate = jnp.dot(qbd32_ref[...], ksum_ref[...], precision=lax.Precision.HIGHEST,
                       preferred_element_type=F32) * (1.0 / MOBA_BLOCK)
        gate_t = gate.T[:n_blocks]
        blk = lax.broadcasted_iota(jnp.int32, (n_blocks, LANES), 0)
        sel_t = _topk_select(gate_t, blk >= 0, blk, n_blocks)
        sel = jnp.concatenate([jnp.where(sel_t, 1.0, 0.0).astype(F32),
                               jnp.zeros((LANES - n_blocks, LANES), F32)], axis=0).T[:n_rows]
        q_idx = rowinfo_ref[0]
        slope = rowinfo_ref[1]
        lane_f = lax.broadcasted_iota(jnp.int32, (n_rows, LANES), 1).astype(F32)

        for b in range(n_blocks):
            bias_ref[b] = jnp.broadcast_to(jnp.where(sel[:, b:b + 1] > 0.5, 0.0, -jnp.inf), (n_rows, LANES))

        def logits(p):
            dist = (past + q_idx) - (jnp.asarray(p * page, F32) + lane_f)
            return s_ref[p] - slope * dist + bias_ref[p // ppb]

        dist_new = q_idx - lane_f
        s_new = jnp.where((dist_new >= 0) & (lane_f < n_new), s_ref[n_pages] - slope * dist_new, -jnp.inf)
        m = jnp.max(lax.fori_loop(0, n_pages, lambda p, mv: jnp.maximum(mv, logits(p)), s_new, unroll=8),
                    axis=1, keepdims=True)

        def exp_step(p, den):
            e = jnp.exp(logits(p) - m)
            s_ref[p] = e
            return den + e

        e_new = jnp.exp(s_new - m)
        inv = 1.0 / jnp.sum(lax.fori_loop(0, n_pages, exp_step, e_new, unroll=8), axis=1, keepdims=True)
        s_ref[n_pages] = e_new * inv

        def norm_step(p, carry):
            s_ref[p] = s_ref[p] * inv
            return carry

        lax.fori_loop(0, n_pages, norm_step, 0, unroll=8)

    def weighted(p_idx, v_t):
        return lax.dot_general(s_ref[p_idx].astype(BF16), v_t.astype(BF16), _NT, preferred_element_type=F32)

    @pl.when(t >= tk)
    def _():
        acc = acc_ref[...]
        for i in range(pps):
            acc = acc + weighted((t - tk) * pps + i, ring_ref[slot, i].reshape(d, page))
        acc_ref[...] = acc

    @pl.when(t == 2 * tk - 1)
    def _():
        acc = acc_ref[...] + weighted(n_pages, vnt_ref[...])
        outs = [jnp.sum(jnp.where(own_head, acc[n * n_heads:(n + 1) * n_heads, :], 0.0), axis=0, keepdims=True)
                for n in range(n_new)]
        outs.append(jnp.zeros((SAMPLE_PAD - n_new, d), F32))
        o_ref[...] = jnp.concatenate(outs, axis=0).astype(BF16)


def sample_attn(page_table, rowinfo, q, k_new, v_new, cache_kt, cache_vt, n_new, pps=8):
    n_dec, n_pages = page_table.shape
    _, n_heads, _, page = cache_kt.shape
    d = n_heads * HEAD_DIM
    ppb = MOBA_BLOCK // page
    assert n_pages % pps == 0 and pps % ppb == 0 and page == LANES
    assert n_new <= SAMPLE_PAD and n_heads % SUBLANES == 0
    tk = n_pages // pps
    n_blocks = n_pages // ppb
    n_rows = n_heads * n_new
    assert n_blocks <= LANES and n_rows <= LANES and n_dec * 2 * tk >= PAGE_SLOTS

    row_spec = pl.BlockSpec((SAMPLE_PAD, d), lambda b, t, pt: (b, 0))
    pool_spec = pl.BlockSpec(memory_space=pl.ANY)
    grid_spec = pltpu.PrefetchScalarGridSpec(
        num_scalar_prefetch=1,
        grid=(n_dec, 2 * tk),
        in_specs=[pl.BlockSpec((2, n_rows, LANES), lambda b, t, pt: (0, 0, 0)), row_spec, row_spec, row_spec,
                  pool_spec, pool_spec],
        out_specs=row_spec,
        scratch_shapes=[pltpu.VMEM((n_rows, d), BF16),
                        pltpu.VMEM((LANES, d), F32),
                        pltpu.VMEM((n_pages + 1, n_rows, LANES), F32),
                        pltpu.VMEM((d, LANES), F32),
                        pltpu.VMEM((n_rows, d), F32),
                        pltpu.VMEM((d, LANES), F32),
                        pltpu.VMEM((d, LANES), F32),
                        pltpu.VMEM((n_blocks, n_rows, LANES), F32),
                        pltpu.VMEM((PAGE_SLOTS, pps, n_heads, HEAD_DIM, page), F32),
                        pltpu.SemaphoreType.DMA((PAGE_SLOTS,))],
    )
    return pl.pallas_call(
        functools.partial(_sample_attn_body, n_pages=n_pages, pps=pps, n_new=n_new, n_heads=n_heads),
        grid_spec=grid_spec,
        out_shape=jax.ShapeDtypeStruct((n_dec * SAMPLE_PAD, d), BF16),
        compiler_params=_params(("arbitrary", "arbitrary")),
        name="sample_attn",
    )(page_table, rowinfo, q, k_new, v_new, cache_kt, cache_vt)


def _ssd_body(z_ref, xs_ref, bc_ref, dt_ref, cinit_ref, sinit_ref, cw_ref, cb_ref, dtb_ref, alog_ref,
              dsk_ref, gn_ref, y_ref, cout_ref, sout_ref, ext_ref, xc_ref, h_ref, **static):
    for phase in ("start", "main", "end"):
        for p in range(z_ref.shape[0]):
            _ssd_chunk(z_ref.at[p], xs_ref.at[p], bc_ref.at[p], dt_ref.at[p], cinit_ref.at[p], sinit_ref.at[p],
                       cw_ref, cb_ref, dtb_ref, alog_ref, dsk_ref, gn_ref, y_ref.at[p], cout_ref.at[p],
                       sout_ref.at[p], ext_ref.at[p], xc_ref.at[p], h_ref.at[p], phase=phase, **static)


def _ssd_chunk(z_ref, xs_ref, bc_ref, dt_ref, cinit_ref, sinit_ref, cw_ref, cb_ref, dtb_ref, alog_ref,
               dsk_ref, gn_ref, y_ref, cout_ref, sout_ref, ext_ref, xc_ref, h_ref, *, rows, valid, n_chunks,
               phase):
    cl = xc_ref.shape[0]
    c = pl.program_id(1)
    d_inner = xs_ref.shape[1]
    n_heads = d_inner // SSM_HEAD_DIM
    hpg = n_heads // N_GROUPS
    gw = hpg * SSM_HEAD_DIM
    tail = SUBLANES

    if phase == "start":
        @pl.when(c == 0)
        def _():
            ext_ref[0:tail, :] = cinit_ref[...]
            h_ref[...] = sinit_ref[...].reshape(h_ref.shape)
            if rows < cl:
                ext_ref[tail + rows:, :] = jnp.zeros((cl - rows, ext_ref.shape[1]), F32)

        @pl.when(c > 0)
        def _():
            ext_ref[0:tail, :] = ext_ref[cl:cl + tail, :]

        return

    if phase == "end":
        @pl.when(c == n_chunks - 1)
        def _():
            sout_ref[...] = h_ref[...].reshape(sout_ref.shape)
            first = tail + ((valid - (CONV_W - 1)) // SUBLANES) * SUBLANES
            cout_ref[...] = ext_ref[first:first + SUBLANES, :]

        return

    ext_ref[tail:tail + rows, 0:d_inner] = xs_ref[...]
    ext_ref[tail:tail + rows, d_inner:] = bc_ref[...]

    cw = cw_ref[...]
    conv = cb_ref[...] + ext_ref[tail:tail + cl, :] * cw[CONV_W - 1:CONV_W, :]
    for back in range(1, CONV_W):
        conv = conv + ext_ref[tail - back:tail - back + cl, :] * cw[CONV_W - 1 - back:CONV_W - back, :]
    xc_ref[...] = _silu(conv)

    def pad(v):
        if rows == cl:
            return v
        return jnp.concatenate([v, jnp.zeros((cl - rows, v.shape[1]), v.dtype)], axis=0)

    trow = lax.broadcasted_iota(jnp.int32, (cl, LANES), 0)
    dt_raw = pad(dt_ref[...]) + dtb_ref[...]
    dt = jnp.maximum(dt_raw, 0.0) + jnp.log1p(jnp.exp(-jnp.abs(dt_raw)))
    dt = jnp.where(trow < valid, dt, 0.0)
    a = -jnp.exp(alog_ref[...])
    tri_r = lax.broadcasted_iota(jnp.int32, (cl, cl), 0)
    tri_c = lax.broadcasted_iota(jnp.int32, (cl, cl), 1)
    causal = tri_c <= tri_r
    a_cs = jnp.dot(jnp.where(causal, 1.0, 0.0).astype(F32), dt * a, precision=lax.Precision.HIGHEST,
                   preferred_element_type=F32)
    a_cs_t = a_cs.T
    dt_t = dt.T
    z = pad(z_ref[...])
    from_start = jnp.exp(a_cs)
    to_end_all = jnp.exp(a_cs_t[:, cl - 1:cl] - a_cs_t) * dt_t

    for g in range(N_GROUPS):
        bg = xc_ref[:, d_inner + g * D_STATE:d_inner + (g + 1) * D_STATE].astype(BF16)
        cg = xc_ref[:, d_inner + (N_GROUPS + g) * D_STATE:d_inner + (N_GROUPS + g + 1) * D_STATE].astype(BF16)
        cb = lax.dot_general(cg, bg, _NT, preferred_element_type=F32)
        xg = xc_ref[:, g * gw:(g + 1) * gw]
        xg_t = xg.T
        h_old = h_ref[g * gw:(g + 1) * gw, :]
        y_off = lax.dot_general(cg, h_old.astype(BF16), _NT, preferred_element_type=F32)
        ys, xw_rows, h_scaled = [], [], []
        for e in range(hpg):
            h = g * hpg + e
            acs_col = a_cs[:, h:h + 1]
            acs_row = a_cs_t[h:h + 1, :]
            dt_row = dt_t[h:h + 1, :]
            decay = jnp.exp(jnp.where(causal, acs_col - acs_row, -jnp.inf))
            w = (cb * decay * dt_row).astype(BF16)
            xh = xg[:, e * SSM_HEAD_DIM:(e + 1) * SSM_HEAD_DIM]
            y_diag = jnp.dot(w, xh.astype(BF16), preferred_element_type=F32)
            ys.append(y_diag + y_off[:, e * SSM_HEAD_DIM:(e + 1) * SSM_HEAD_DIM] * from_start[:, h:h + 1])
            xw_rows.append(xg_t[e * SSM_HEAD_DIM:(e + 1) * SSM_HEAD_DIM, :] * to_end_all[h:h + 1, :])
            h_scaled.append(h_old[e * SSM_HEAD_DIM:(e + 1) * SSM_HEAD_DIM, :] * from_start[cl - 1:cl, h:h + 1])
        states = jnp.dot(jnp.concatenate(xw_rows, axis=0).astype(BF16), bg, preferred_element_type=F32)
        h_ref[g * gw:(g + 1) * gw, :] = jnp.concatenate(h_scaled, axis=0) + states
        y = jnp.concatenate(ys, axis=1) + dsk_ref[:, g * gw:(g + 1) * gw] * xg
        yz = y * _silu(z[:, g * gw:(g + 1) * gw])
        ms = jnp.mean(yz * yz, axis=-1, keepdims=True)
        yn = yz * lax.rsqrt(ms + EPS) * gn_ref[:, g * gw:(g + 1) * gw]
        y_ref[:, g * gw:(g + 1) * gw] = yn[:rows].astype(BF16)


def ssd_mixer(proj, conv_init, ssm_init, cw, cb, dtb, alog, dsk, gnorm, n_seq, rows, valid, d_inner):
    total_rows = proj.shape[0] // n_seq
    n_chunks = total_rows // rows
    assert rows == SSD_CHUNK or n_chunks == 1
    chunk = rows if n_chunks == 1 else SSD_CHUNK
    assert CONV_W - 1 <= valid <= rows
    n_heads = d_inner // SSM_HEAD_DIM
    conv_dim = d_inner + 2 * N_GROUPS * D_STATE
    assert conv_dim == 2 * d_inner

    par = SSD_SEQS_PER_STEP
    assert n_seq % par == 0
    proj3 = proj.reshape(n_seq, total_rows, proj.shape[1])

    def col(width, idx):
        return pl.BlockSpec((par, rows, width), lambda b, c: (b, c, idx))

    def const(shape):
        return pl.BlockSpec(shape, lambda b, c: (0,) * len(shape))

    def per_seq(shape):
        return pl.BlockSpec((par,) + shape, lambda b, c: (b,) + (0,) * len(shape))

    y, conv_out, ssm_out = pl.pallas_call(
        functools.partial(_ssd_body, rows=rows, valid=valid, n_chunks=n_chunks),
        grid=(n_seq // par, n_chunks),
        in_specs=[col(d_inner, 0), col(d_inner, 1), col(d_inner, 2), col(LANES, 3 * d_inner // LANES),
                  per_seq((SUBLANES, conv_dim)), per_seq((n_heads, SSM_HEAD_DIM, D_STATE)),
                  const((CONV_W, conv_dim)), const((1, conv_dim)), const((1, LANES)), const((1, LANES)),
                  const((1, d_inner)), const((1, d_inner))],
        out_specs=[col(d_inner, 0), per_seq((SUBLANES, conv_dim)), per_seq((n_heads, SSM_HEAD_DIM, D_STATE))],
        out_shape=[jax.ShapeDtypeStruct((n_seq, total_rows, d_inner), BF16),
                   jax.ShapeDtypeStruct((n_seq, SUBLANES, conv_dim), F32),
                   jax.ShapeDtypeStruct((n_seq, n_heads, SSM_HEAD_DIM, D_STATE), F32)],
        scratch_shapes=[pltpu.VMEM((par, SUBLANES + chunk, conv_dim), F32),
                        pltpu.VMEM((par, chunk, conv_dim), F32),
                        pltpu.VMEM((par, n_heads * SSM_HEAD_DIM, D_STATE), F32)],
        compiler_params=_params(("parallel", "arbitrary")),
        name="ssd_mixer",
    )(proj3, proj3, proj3, proj3, conv_init, ssm_init, cw, cb, dtb, alog, dsk, gnorm)
    return y.reshape(n_seq * total_rows, d_inner), conv_out, ssm_out


def _pad_cols(w, n):
    return jnp.pad(w, ((0, 0), (0, n - w.shape[1])))


def _pad_lanes(v):
    return jnp.pad(v, (0, LANES - v.shape[0])).reshape(1, LANES)


def kernel(x_prompt, x_sample, cache_k, cache_v, state_conv, state_ssm, page_table, norm_mix, norm_ffn, w_qkv, w_o, w_in_ssm, conv_w, conv_b, dt_bias, a_log, d_skip, norm_ssm, w_out_ssm, w_gate_up, w_down, norm_final):
    n_seq, seq, d_model = x_prompt.shape
    n_dec, n_new, _ = x_sample.shape
    n_heads = d_model // HEAD_DIM
    d_inner = norm_ssm.shape[1]
    ssm_heads = d_inner // SSM_HEAD_DIM
    conv_dim = conv_w.shape[2]
    tm_p = 512
    tm_s = n_dec * SAMPLE_PAD

    hp = x_prompt.reshape(n_seq * seq, d_model)
    hs = jnp.pad(x_sample, ((0, 0), (0, SAMPLE_PAD - n_new), (0, 0))).reshape(tm_s, d_model)

    wqkv = w_qkv[0]
    wq_hi = _bf16_round(wqkv[:, :d_model])
    w_hi = jnp.concatenate([wq_hi, wqkv[:, d_model:]], axis=1).astype(BF16)
    wq_lo = (wqkv[:, :d_model] - wq_hi).astype(BF16)
    wkvt = wqkv[:, d_model:].T.astype(BF16)
    slopes = jnp.exp2(-8.0 * (jnp.arange(n_heads, dtype=F32) + 1.0) / n_heads)
    kx = _key_extras(slopes, seq, seq // MOBA_BLOCK)
    q_p, ka, va, kt_p, vt_p, xm_p = qkv_prompt(hp, norm_mix[0], w_hi, wq_lo, wkvt, kx, n_seq, seq, tm_p)
    q_s, k_s, v_s, _ = qkv_proj(hs, norm_mix[0], w_hi, wq_lo, tm_s)
    kmean = mm_f32(xm_p.reshape(n_seq * seq // MOBA_BLOCK, d_model), wqkv[:, d_model:2 * d_model])
    qa = moba_prep(q_p, kmean, n_seq, seq, n_heads)
    attn_p = moba_attn(qa, ka, va)
    rows = jnp.arange(n_heads * n_new)
    rowinfo = jnp.stack([(rows // n_heads).astype(F32), slopes[rows % n_heads]])
    rowinfo = jnp.broadcast_to(rowinfo[:, :, None], (2, n_heads * n_new, LANES))
    cache_kt = jnp.transpose(cache_k[0], (0, 2, 3, 1))
    cache_vt = jnp.transpose(cache_v[0], (0, 2, 3, 1))
    attn_s = sample_attn(page_table, rowinfo, q_s, k_s, v_s, cache_kt, cache_vt, n_new)
    hp = mm_res(attn_p, w_o, 0, hp, tm_p)
    hs = mm_res(attn_s, w_o, 0, hs, tm_s)

    k_prompt = jnp.transpose(kt_p, (0, 3, 1, 2))[None]
    v_prompt = jnp.transpose(vt_p, (0, 3, 1, 2))[None]
    k_sample = k_s.reshape(n_dec, SAMPLE_PAD, n_heads, HEAD_DIM)[None, :, :n_new]
    v_sample = v_s.reshape(n_dec, SAMPLE_PAD, n_heads, HEAD_DIM)[None, :, :n_new]

    def ffn(h, layer, tm, final_g=None):
        act = swiglu_gu(h, norm_ffn[layer], w_gate_up, layer, tm, 256)
        return mm_res(act, w_down, layer, h, tm, final_g)

    hp = ffn(hp, 0, tm_p)
    hs = ffn(hs, 0, tm_s)

    in_cols = d_inner + conv_dim + ssm_heads
    assert w_in_ssm.shape[2] == in_cols
    proj_w = ((in_cols + 1279) // 1280) * 1280
    w_in = _pad_cols(w_in_ssm[0], proj_w).astype(BF16)
    proj_p = norm_mm(hp, norm_mix[1], w_in, tm_p, 1280)
    proj_s = norm_mm(hs, norm_mix[1], w_in, tm_s, 1280)
    cb = conv_b[0].reshape(1, conv_dim)
    dtb = _pad_lanes(dt_bias[0])
    alog = _pad_lanes(a_log[0])
    dsk = jnp.repeat(d_skip[0], SSM_HEAD_DIM).reshape(1, d_inner)
    gn = norm_ssm[0].reshape(1, d_inner)
    zero_conv = jnp.zeros((n_seq, SUBLANES, conv_dim), F32)
    zero_ssm = jnp.zeros((n_seq, ssm_heads, SSM_HEAD_DIM, D_STATE), F32)
    y_p, conv_p, ssm_p = ssd_mixer(proj_p, zero_conv, zero_ssm, conv_w[0], cb, dtb, alog, dsk, gn,
                                   n_seq, SSD_CHUNK, SSD_CHUNK, d_inner)
    conv_init_s = jnp.pad(state_conv[0], ((0, 0), (SUBLANES - (CONV_W - 1), 0), (0, 0)))
    y_s, conv_s, ssm_s = ssd_mixer(proj_s, conv_init_s, state_ssm[0], conv_w[0], cb, dtb, alog, dsk, gn,
                                   n_dec, SAMPLE_PAD, n_new, d_inner)
    hp = mm_res(y_p, w_out_ssm, 0, hp, tm_p)
    hs = mm_res(y_s, w_out_ssm, 0, hs, tm_s)
    y_prompt = ffn(hp, 1, tm_p, norm_final).reshape(n_seq, seq, d_model)
    y_sample = ffn(hs, 1, tm_s, norm_final).reshape(n_dec, SAMPLE_PAD, d_model)[:, :n_new]

    lo_p = SUBLANES - (CONV_W - 1)
    lo_s = (n_new - (CONV_W - 1)) % SUBLANES
    return (y_prompt, y_sample, k_prompt, v_prompt, k_sample, v_sample,
            conv_p[None, :, lo_p:lo_p + CONV_W - 1], ssm_p[None],
            conv_s[None, :, lo_s:lo_s + CONV_W - 1], ssm_s[None])
```
